```python
import jax, jax.numpy as jnp
from jax import lax
import numpy as np

D_MODEL = 1024
BATCH = 8
SEQ = 2048
DEPTH = 1

N_MEM = 256
NORM_EPS = 1e-6
NEG_INF = -1e30

MIX_WIDTH = D_MODEL
MOBA_WIDTH = MIX_WIDTH // 2
RET_WIDTH = MIX_WIDTH - MOBA_WIDTH

MOBA_HEAD_DIM = 64
MOBA_HEADS = MOBA_WIDTH // MOBA_HEAD_DIM
MOBA_BLOCK = 256
MOBA_TOPK = 3
MOBA_Q_BLOCK = 32
ROPE_THETA = 500000.0
ROPE_DIM = MOBA_HEAD_DIM // 4

RET_HEADS = 4
RET_V_DIM = RET_WIDTH // RET_HEADS
RET_QK_DIM = RET_V_DIM // 2
RET_CHUNK = 128
RET_THETA = 10000.0

CROSS_HEADS = 4
CROSS_HEAD_DIM = D_MODEL // CROSS_HEADS

D_FF = -(-8 * D_MODEL // (3 * 256)) * 256

W_MQ = MOBA_WIDTH
W_MK = MOBA_WIDTH
W_MV = MOBA_WIDTH
W_RQ = RET_HEADS * RET_QK_DIM
W_RK = RET_HEADS * RET_QK_DIM
W_RV = RET_WIDTH
W_RG = RET_WIDTH
IN_PROJ_WIDTH = W_MQ + W_MK + W_MV + W_RQ + W_RK + W_RV + W_RG
SPLIT_POINTS = [W_MQ, W_MQ + W_MK, W_MQ + W_MK + W_MV,
                W_MQ + W_MK + W_MV + W_RQ,
                W_MQ + W_MK + W_MV + W_RQ + W_RK,
                W_MQ + W_MK + W_MV + W_RQ + W_RK + W_RV]

kernel_name = "hymba_moba_retnet_sandwich_layer"


def rms_norm(x, g):
    xf = x.astype(jnp.float32)
    y = xf * lax.rsqrt(jnp.mean(xf * xf, axis=-1, keepdims=True) + NORM_EPS)
    return (y * g.astype(jnp.float32)).astype(x.dtype)


def rotary(x, inv_freq, rot_dim):
    S = x.shape[2]
    half = rot_dim // 2
    ang = jnp.arange(S, dtype=jnp.float32)[:, None] * inv_freq[None, :]
    cos, sin = jnp.cos(ang), jnp.sin(ang)
    xr = x[..., :rot_dim].astype(jnp.float32)
    x1, x2 = xr[..., :half], xr[..., half:]
    rot = jnp.concatenate([x1 * cos - x2 * sin, x2 * cos + x1 * sin], axis=-1).astype(x.dtype)
    return jnp.concatenate([rot, x[..., rot_dim:]], axis=-1)


def split_heads(t, n_heads):
    B, S, _ = t.shape
    return t.reshape(B, S, n_heads, -1).transpose(0, 2, 1, 3)


def merge_heads(t):
    B, H, S, d = t.shape
    return t.transpose(0, 2, 1, 3).reshape(B, S, H * d)


def moba_attention(q, k, v):
    B, H, S, dh = q.shape
    L = MOBA_BLOCK
    nb = -(-S // L)
    pad = nb * L - S
    kp = jnp.pad(k, ((0, 0), (0, 0), (0, pad), (0, 0)))
    vp = jnp.pad(v, ((0, 0), (0, 0), (0, pad), (0, 0)))
    k_blocks = kp.reshape(B, H, nb, L, dh)
    v_blocks = vp.reshape(B, H, nb, L, dh)
    k_mean = jnp.mean(k_blocks.astype(jnp.float32), axis=3).astype(q.dtype)
    n_sel = min(MOBA_TOPK, nb)
    scale = dh ** -0.5
    QB = MOBA_Q_BLOCK
    nq = S // QB
    q_blocks = q.reshape(B, H, nq, QB, dh).transpose(2, 0, 1, 3, 4)
    b_idx = jnp.arange(B)[:, None, None, None]
    h_idx = jnp.arange(H)[None, :, None, None]
    block_ids = jnp.arange(nb)
    slot_ids = jnp.arange(n_sel)
    key_offsets = jnp.arange(L)

    def one_query_block(args):
        qb, qi = args
        q_pos = qi * QB + jnp.arange(QB)
        cur = (qi * QB) // L
        gate = jnp.einsum('bhqd,bhnd->bhqn', qb, k_mean).astype(jnp.float32)
        gate = jnp.where((block_ids < cur)[None, None, None, :], gate, NEG_INF)
        _, sel = lax.top_k(gate, n_sel)
        slot_valid = slot_ids < cur
        k_sel = k_blocks[b_idx, h_idx, sel]
        v_sel = v_blocks[b_idx, h_idx, sel]
        s_sel = jnp.einsum('bhqd,bhqnld->bhqnl', qb, k_sel).astype(jnp.float32) * scale
        s_sel = jnp.where(slot_valid[:, None], s_sel, NEG_INF)
        k_own = lax.dynamic_index_in_dim(k_blocks, cur, axis=2, keepdims=False)
        v_own = lax.dynamic_index_in_dim(v_blocks, cur, axis=2, keepdims=False)
        s_own = jnp.einsum('bhqd,bhld->bhql', qb, k_own).astype(jnp.float32) * scale
        own_pos = cur * L + key_offsets
        s_own = jnp.where(own_pos[None, :] <= q_pos[:, None], s_own, NEG_INF)
        logits = jnp.concatenate([s_sel.reshape(B, H, QB, n_sel * L), s_own], axis=-1)
        p = jax.nn.softmax(logits, axis=-1)
        p_sel = p[..., :n_sel * L].reshape(B, H, QB, n_sel, L).astype(v.dtype)
        p_own = p[..., n_sel * L:].astype(v.dtype)
        return (jnp.einsum('bhqnl,bhqnld->bhqd', p_sel, v_sel)
                + jnp.einsum('bhql,bhld->bhqd', p_own, v_own))

    out = lax.map(one_query_block, (q_blocks, jnp.arange(nq)))
    return out.transpose(1, 2, 0, 3, 4).reshape(B, H, S, dh)


def retention(q, k, v):
    B, H, S, dk = q.shape
    dv = v.shape[-1]
    C = RET_CHUNK
    nc = S // C
    log_g = jnp.log(1.0 - jnp.power(2.0, -5.0 - jnp.arange(H, dtype=jnp.float32)))
    idx = jnp.arange(C, dtype=jnp.float32)
    diff = idx[:, None] - idx[None, :]
    inner_decay = jnp.where(diff >= 0, jnp.exp(log_g[:, None, None] * jnp.maximum(diff, 0.0)), 0.0)
    q_decay = jnp.exp(log_g[:, None] * (idx + 1.0))[None, :, :, None]
    k_decay = jnp.exp(log_g[:, None] * (C - 1.0 - idx))[None, :, :, None]
    chunk_decay = jnp.exp(log_g * C)[None, :, None, None]

    def to_chunks(t):
        return t.astype(jnp.float32).reshape(B, H, nc, C, t.shape[-1]).transpose(2, 0, 1, 3, 4)

    def step(state, inp):
        qc, kc, vc = inp
        attn = jnp.einsum('bhid,bhjd->bhij', qc, kc) * inner_decay[None]
        inner = jnp.einsum('bhij,bhjv->bhiv', attn, vc)
        cross = jnp.einsum('bhid,bhdv->bhiv', qc * q_decay, state)
        new_state = state * chunk_decay + jnp.einsum('bhjd,bhjv->bhdv', kc * k_decay, vc)
        return new_state, inner + cross

    state0 = jnp.zeros((B, H, dk, dv), jnp.float32)
    _, out = lax.scan(step, state0, (to_chunks(q), to_chunks(k), to_chunks(v)))
    return out.transpose(1, 2, 0, 3, 4).reshape(B, H, S, dv)


def cross_attention(h, mem_n, w_cq, w_ckv, w_co):
    q = split_heads(h @ w_cq, CROSS_HEADS)
    kv = mem_n @ w_ckv
    k = split_heads(kv[..., :D_MODEL], CROSS_HEADS)
    v = split_heads(kv[..., D_MODEL:], CROSS_HEADS)
    s = jnp.einsum('bhsd,bhmd->bhsm', q, k).astype(jnp.float32) * (CROSS_HEAD_DIM ** -0.5)
    p = jax.nn.softmax(s, axis=-1).astype(v.dtype)
    o = jnp.einsum('bhsm,bhmd->bhsd', p, v)
    return merge_heads(o) @ w_co


def setup_inputs(seed: int = 0) -> dict:
    key = jax.random.key(seed)
    ks = jax.random.split(key, 16)

    def w(k, shape, fan_in):
        return jax.random.normal(k, shape, jnp.float32) * (fan_in ** -0.5)

    def gain(k):
        return 1.0 + 0.05 * jax.random.normal(k, (DEPTH, D_MODEL), jnp.float32)

    return {
        "x": jax.random.normal(ks[0], (BATCH, SEQ, D_MODEL), jnp.float32),
        "mem": jax.random.normal(ks[1], (BATCH, N_MEM, D_MODEL), jnp.float32),
        "g_pre_mix": gain(ks[2]),
        "w_in": w(ks[3], (DEPTH, D_MODEL, IN_PROJ_WIDTH), D_MODEL),
        "w_out": w(ks[4], (DEPTH, MIX_WIDTH, D_MODEL), MIX_WIDTH),
        "g_post_mix": gain(ks[5]),
        "g_pre_cross": gain(ks[6]),
        "g_mem": gain(ks[7]),
        "w_cq": w(ks[8], (DEPTH, D_MODEL, D_MODEL), D_MODEL),
        "w_ckv": w(ks[9], (DEPTH, D_MODEL, 2 * D_MODEL), D_MODEL),
        "w_co": w(ks[10], (DEPTH, D_MODEL, D_MODEL), D_MODEL),
        "g_post_cross": gain(ks[11]),
        "g_pre_ffn": gain(ks[12]),
        "w_gate_up": w(ks[13], (DEPTH, D_MODEL, 2 * D_FF), D_MODEL),
        "w_down": w(ks[14], (DEPTH, D_FF, D_MODEL), D_FF),
        "g_post_ffn": gain(ks[15]),
    }


def reference(x, mem, g_pre_mix, w_in, w_out, g_post_mix, g_pre_cross, g_mem, w_cq, w_ckv,
              w_co, g_post_cross, g_pre_ffn, w_gate_up, w_down, g_post_ffn):
    moba_inv = jnp.power(ROPE_THETA, -jnp.arange(ROPE_DIM // 2, dtype=jnp.float32) * 2.0 / ROPE_DIM)
    ret_inv = 1.0 / jnp.power(RET_THETA, jnp.linspace(0.0, 1.0, RET_QK_DIM // 2, dtype=jnp.float32))
    for l in range(DEPTH):
        h = rms_norm(x, g_pre_mix[l])
        proj = h @ w_in[l]
        mq, mk, mv, rq, rk, rv, rg = jnp.split(proj, SPLIT_POINTS, axis=-1)
        mq = rotary(split_heads(mq, MOBA_HEADS), moba_inv, ROPE_DIM)
        mk = rotary(split_heads(mk, MOBA_HEADS), moba_inv, ROPE_DIM)
        mo = merge_heads(moba_attention(mq, mk, split_heads(mv, MOBA_HEADS)))
        rq = rotary(split_heads(rq, RET_HEADS), ret_inv, RET_QK_DIM)
        rk = rotary(split_heads(rk, RET_HEADS), ret_inv, RET_QK_DIM) * (RET_QK_DIM ** -0.5)
        ro = retention(rq, rk, split_heads(rv, RET_HEADS))
        ro = ro * lax.rsqrt(jnp.mean(ro * ro, axis=-1, keepdims=True) + NORM_EPS)
        ro = jax.nn.silu(rg) * merge_heads(ro).astype(x.dtype)
        mix = jnp.concatenate([mo, ro], axis=-1) @ w_out[l]
        x = x + rms_norm(mix, g_post_mix[l])
        h = rms_norm(x, g_pre_cross[l])
        mem_n = rms_norm(mem, g_mem[l])
        c = cross_attention(h, mem_n, w_cq[l], w_ckv[l], w_co[l])
        x = x + rms_norm(c, g_post_cross[l])
        h = rms_norm(x, g_pre_ffn[l])
        gu = h @ w_gate_up[l]
        f = (jax.nn.silu(gu[..., :D_FF]) * gu[..., D_FF:]) @ w_down[l]
        x = x + rms_norm(f, g_post_ffn[l])
    return x
```

```python
import functools
import math

import jax
import jax.numpy as jnp
from jax import lax
from jax.experimental import pallas as pl
from jax.experimental.pallas import tpu as pltpu

F32 = jnp.float32
BF16 = jnp.bfloat16

NORM_EPS = 1e-6
NEG_INF = -1e30

LANES = 128
VMEM_LIMIT_BYTES = 56 * 1024 * 1024

MOBA_HEAD_DIM = 64
MOBA_HEADS = 8
MOBA_BLOCK = 256
MOBA_TOPK = 3
ROPE_THETA = 500000.0
ROPE_DIM = MOBA_HEAD_DIM // 4

RET_HEADS = 4
RET_QK_DIM = 64
RET_V_DIM = 128
RET_THETA = 10000.0
RET_CHUNK = 256

CROSS_HEADS = 4

ROW_TILE = 512


def _dot(a, b):
    return jnp.dot(a, b, preferred_element_type=F32)


def _dot_nt(a, b):
    return lax.dot_general(a, b, (((1,), (1,)), ((), ())), preferred_element_type=F32)


def _dot_tn(a, b):
    return lax.dot_general(a, b, (((0,), (0,)), ((), ())), preferred_element_type=F32)


def _rms(x, g):
    return x * lax.rsqrt(jnp.mean(x * x, axis=-1, keepdims=True) + NORM_EPS) * g


def _params(n_grid_dims):
    return pltpu.CompilerParams(
        dimension_semantics=("arbitrary",) * n_grid_dims,
        vmem_limit_bytes=VMEM_LIMIT_BYTES,
    )


def _rotary_tables(seq, inv_freq, rot_dim, head_dim, scale):
    half = rot_dim // 2
    ang = jnp.arange(seq, dtype=F32)[:, None] * inv_freq[None, :]
    cos, sin = jnp.cos(ang), jnp.sin(ang)
    pad = head_dim - rot_dim
    a = jnp.concatenate([cos, cos, jnp.ones((seq, pad), F32)], axis=1)
    bm = jnp.concatenate([-sin, jnp.zeros((seq, half + pad), F32)], axis=1)
    bp = jnp.concatenate([jnp.zeros((seq, half), F32), sin, jnp.zeros((seq, pad), F32)], axis=1)
    reps = LANES // head_dim
    tabs = jnp.stack([jnp.tile(t, (1, reps)) for t in (a, bm, bp)], axis=0)
    return tabs * scale


def _rotate(acc, tab_ref, half):
    a, bm, bp = tab_ref[0], tab_ref[1], tab_ref[2]
    outs = []
    for c in range(acc.shape[1] // LANES):
        xs = acc[:, c * LANES:(c + 1) * LANES]
        outs.append(xs * a + pltpu.roll(xs, LANES - half, 1) * bm + pltpu.roll(xs, half, 1) * bp)
    return jnp.concatenate(outs, axis=1)


def _in_proj_kernel(x_ref, g_ref, w_ref, tmq_ref, tmk_ref, trq_ref, trk_ref, o_ref, *, sections):
    h = _rms(x_ref[...], g_ref[...]).astype(BF16)
    tabs = {"mq": (tmq_ref, ROPE_DIM // 2), "mk": (tmk_ref, ROPE_DIM // 2),
            "rq": (trq_ref, RET_QK_DIM // 2), "rk": (trk_ref, RET_QK_DIM // 2)}
    for kind, c0, c1 in sections:
        acc = _dot(h, w_ref[:, c0:c1])
        if kind in tabs:
            tab_ref, half = tabs[kind]
            acc = _rotate(acc, tab_ref, half)
        o_ref[:, c0:c1] = acc.astype(BF16)


def _in_proj(x2d, g, w, tabs, seq, sections):
    t, d = x2d.shape
    n = w.shape[1]
    tm = ROW_TILE
    tiles_per_seq = seq // tm
    tab_spec = pl.BlockSpec((3, tm, LANES), lambda i: (0, i % tiles_per_seq, 0))
    return pl.pallas_call(
        functools.partial(_in_proj_kernel, sections=sections),
        grid=(t // tm,),
        in_specs=[
            pl.BlockSpec((tm, d), lambda i: (i, 0)),
            pl.BlockSpec((1, d), lambda i: (0, 0)),
            pl.BlockSpec((d, n), lambda i: (0, 0)),
            tab_spec, tab_spec, tab_spec, tab_spec,
        ],
        out_specs=pl.BlockSpec((tm, n), lambda i: (i, 0)),
        out_shape=jax.ShapeDtypeStruct((t, n), BF16),
        compiler_params=_params(1),
        name="in_proj",
    )(x2d, g, w, *tabs)


def _moba_kernel(q_ref, k_ref, v_ref, o_ref, qp_ref, kp_ref, vp_ref):
    seq = q_ref.shape[0]
    blk = MOBA_BLOCK
    nb = seq // blk
    hd = MOBA_HEAD_DIM
    q2 = q_ref[...]
    k2 = k_ref[...]
    v2 = v_ref[...]
    lane = lax.broadcasted_iota(jnp.int32, (seq, LANES), 1)
    kblk = lax.broadcasted_iota(jnp.int32, (seq, LANES), 0) // blk
    kmean2 = jnp.mean(k2.astype(F32).reshape(nb, blk, LANES), axis=1)

    lane8 = lax.broadcasted_iota(jnp.int32, (nb, LANES), 1)
    jidx = lax.broadcasted_iota(jnp.int32, (nb, seq), 0)
    qblk = lax.broadcasted_iota(jnp.int32, (nb, seq), 1) // blk
    prow = lax.broadcasted_iota(jnp.int32, (2 * nb, LANES), 0)
    plane = lax.broadcasted_iota(jnp.int32, (2 * nb, LANES), 1)

    for hh in range(2):
        lo, hi = hh * hd, (hh + 1) * hd
        base = hd - lo
        mine = (lane >= lo) & (lane < hi)
        km = jnp.where((lane8 >= lo) & (lane8 < hi), kmean2, 0.0)
        km_hi = km.astype(BF16)
        km_lo = (km - km_hi.astype(F32)).astype(BF16)
        pad = jnp.zeros((nb, LANES), BF16)
        gate = (_dot_nt(jnp.concatenate([km_hi, pad], axis=0), q2)
                + _dot_nt(jnp.concatenate([km_lo, pad], axis=0), q2))[:nb]
        rank = jnp.zeros((nb, seq), jnp.int32)
        for i in range(nb):
            gi = gate[i:i + 1, :]
            beats = ((gi > gate) | ((gi == gate) & (i < jidx))) & (i < qblk)
            rank = rank + beats.astype(jnp.int32)
        allow = ((jidx < qblk) & (rank < MOBA_TOPK)) | (jidx == qblk)
        allow16 = jnp.concatenate([allow.astype(BF16), jnp.zeros((nb, seq), BF16)], axis=0)
        place = ((plane == base + prow) & (prow < nb)).astype(BF16)
        allow_q = _dot_tn(allow16, place)
        feat = (lane >= base) & (lane < base + nb)
        bias = jnp.where(feat, (allow_q - 1.0) * (-NEG_INF), 0.0).astype(BF16)
        qp_ref[hh] = jnp.where(mine, q2, bias)
        kp_ref[hh] = jnp.where(mine, k2, (lane == base + kblk).astype(BF16))
        vp_ref[hh] = jnp.where(mine, v2, jnp.ones_like(v2))

    rr = lax.broadcasted_iota(jnp.int32, (blk, blk), 0)
    cc = lax.broadcasted_iota(jnp.int32, (blk, blk), 1)
    lane_b = lax.broadcasted_iota(jnp.int32, (blk, LANES), 1)
    for c in range(nb):
        r0, r1 = c * blk, (c + 1) * blk
        res = []
        for hh in range(2):
            qc = qp_ref[hh, r0:r1, :]
            s_d = jnp.where(cc <= rr, _dot_nt(qc, kp_ref[hh, r0:r1, :]), NEG_INF)
            m = jnp.max(s_d, axis=1, keepdims=True)
            if c > 0:
                s_p = _dot_nt(qc, kp_ref[hh, 0:r0, :])
                m = jnp.maximum(m, jnp.max(s_p, axis=1, keepdims=True))
                ov = _dot(jnp.exp(s_p - m).astype(BF16), vp_ref[hh, 0:r0, :])
            else:
                ov = jnp.zeros((blk, LANES), F32)
            ov = ov + _dot(jnp.exp(s_d - m).astype(BF16), vp_ref[hh, r0:r1, :])
            res.append(ov / pltpu.roll(ov, hd, 1))
        o_ref[r0:r1, :] = jnp.where(lane_b < hd, res[0], res[1]).astype(BF16)


def _moba(proj, batch, seq, q_col, k_col, v_col):
    t = proj.shape[0]
    pairs = MOBA_HEADS // 2

    def spec(col0):
        cb = col0 // LANES
        return pl.BlockSpec((seq, LANES), lambda b, p: (b, cb + p))

    return pl.pallas_call(
        _moba_kernel,
        grid=(batch, pairs),
        in_specs=[spec(q_col), spec(k_col), spec(v_col)],
        out_specs=pl.BlockSpec((seq, LANES), lambda b, p: (b, p)),
        out_shape=jax.ShapeDtypeStruct((t, pairs * LANES), BF16),
        scratch_shapes=[pltpu.VMEM((2, seq, LANES), BF16)] * 3,
        compiler_params=_params(2),
        name="moba",
    )(proj, proj, proj)


_RET_LOG_G = [math.log(1.0 - 2.0 ** (-5.0 - h)) for h in range(RET_HEADS)]


def _retention_kernel(q_ref, k_ref, v_ref, g_ref, o_ref):
    seq = q_ref.shape[0]
    ch = RET_CHUNK
    dv = RET_V_DIM
    pair = pl.program_id(1)
    lane = lax.broadcasted_iota(jnp.int32, (ch, LANES), 1)
    rowf = lax.broadcasted_iota(jnp.int32, (ch, LANES), 0).astype(F32)
    ri = lax.broadcasted_iota(jnp.int32, (ch, ch), 0)
    ci = lax.broadcasted_iota(jnp.int32, (ch, ch), 1)
    diff = (ri - ci).astype(F32)
    for hh in range(2):
        log_g = jnp.where(pair == 0, _RET_LOG_G[hh], _RET_LOG_G[2 + hh]).astype(F32)
        inner_decay = jnp.where(diff >= 0, jnp.exp(log_g * jnp.maximum(diff, 0.0)), 0.0)
        q_decay = jnp.exp(log_g * (rowf + 1.0))
        k_decay = jnp.exp(log_g * (ch - 1.0 - rowf))
        chunk_decay = jnp.exp(jnp.full((LANES, dv), ch, F32) * log_g)
        mine = (lane >= hh * RET_QK_DIM) & (lane < (hh + 1) * RET_QK_DIM)
        state = jnp.zeros((LANES, dv), F32)
        for n in range(seq // ch):
            r0, r1 = n * ch, (n + 1) * ch
            qc = jnp.where(mine, q_ref[r0:r1, :], jnp.zeros((), BF16))
            kc = k_ref[r0:r1, :]
            vc = v_ref[r0:r1, hh * dv:(hh + 1) * dv]
            attn = _dot_nt(qc, kc) * inner_decay
            out = _dot(attn.astype(BF16), vc)
            out = out + _dot((qc.astype(F32) * q_decay).astype(BF16), state.astype(BF16))
            state = state * chunk_decay + _dot_tn((kc.astype(F32) * k_decay).astype(BF16), vc)
            out = out * lax.rsqrt(jnp.mean(out * out, axis=-1, keepdims=True) + NORM_EPS)
            gate = g_ref[r0:r1, hh * dv:(hh + 1) * dv].astype(F32)
            o_ref[r0:r1, hh * dv:(hh + 1) * dv] = (gate * jax.nn.sigmoid(gate) * out).astype(BF16)


def _retention(proj, batch, seq, q_col, k_col, v_col, g_col):
    t = proj.shape[0]
    pairs = RET_HEADS // 2
    wv = 2 * RET_V_DIM

    def spec(col0, width):
        cb = col0 // width
        return pl.BlockSpec((seq, width), lambda b, p: (b, cb + p))

    return pl.pallas_call(
        _retention_kernel,
        grid=(batch, pairs),
        in_specs=[spec(q_col, LANES), spec(k_col, LANES), spec(v_col, wv), spec(g_col, wv)],
        out_specs=pl.BlockSpec((seq, wv), lambda b, p: (b, p)),
        out_shape=jax.ShapeDtypeStruct((t, pairs * wv), BF16),
        compiler_params=_params(2),
        name="retention",
    )(proj, proj, proj, proj)


def _mem_kv_kernel(m_ref, g_ref, w_ref, o_ref):
    o_ref[...] = _dot(_rms(m_ref[...], g_ref[...]).astype(BF16), w_ref[...]).astype(BF16)


def _mem_kv(mem2d, g, w, rows):
    t, d = mem2d.shape
    n = w.shape[1]
    return pl.pallas_call(
        _mem_kv_kernel,
        grid=(t // rows,),
        in_specs=[pl.BlockSpec((rows, d), lambda i: (i, 0)),
                  pl.BlockSpec((1, d), lambda i: (0, 0)),
                  pl.BlockSpec((d, n), lambda i: (0, 0))],
        out_specs=pl.BlockSpec((rows, n), lambda i: (i, 0)),
        out_shape=jax.ShapeDtypeStruct((t, n), BF16),
        compiler_params=_params(1),
        name="mem_kv",
    )(mem2d, g, w)


def _mix_cross_kernel(mo_ref, ro_ref, x_ref, kv_ref, wo_ref, wq_ref, wc_ref,
                      g_mix_ref, g_pre_ref, g_post_ref, o_ref):
    d = x_ref.shape[1]
    half = mo_ref.shape[1]
    dc = d // CROSS_HEADS
    mix = _dot(mo_ref[...], wo_ref[0:half, :]) + _dot(ro_ref[...], wo_ref[half:, :])
    x1 = x_ref[...] + _rms(mix, g_mix_ref[...])
    h = _rms(x1, g_pre_ref[...]).astype(BF16)
    cq = (_dot(h, wq_ref[...]) * (dc ** -0.5)).astype(BF16)
    heads = []
    for hc in range(CROSS_HEADS):
        c0, c1 = hc * dc, (hc + 1) * dc
        s = _dot_nt(cq[:, c0:c1], kv_ref[:, c0:c1])
        e = jnp.exp(s - jnp.max(s, axis=1, keepdims=True))
        o = _dot(e.astype(BF16), kv_ref[:, d + c0:d + c1])
        heads.append((o / jnp.sum(e, axis=1, keepdims=True)).astype(BF16))
    c = _dot(jnp.concatenate(heads, axis=1), wc_ref[...])
    o_ref[...] = x1 + _rms(c, g_post_ref[...])


def _mix_cross(mo, ro, x2d, kv, w_out, w_cq, w_co, g_mix, g_pre, g_post, seq, n_mem):
    t, d = x2d.shape
    tm = ROW_TILE
    tiles_per_seq = seq // tm
    half = mo.shape[1]
    full = lambda i: (0, 0)
    return pl.pallas_call(
        _mix_cross_kernel,
        grid=(t // tm,),
        in_specs=[
            pl.BlockSpec((tm, half), lambda i: (i, 0)),
            pl.BlockSpec((tm, half), lambda i: (i, 0)),
            pl.BlockSpec((tm, d), lambda i: (i, 0)),
            pl.BlockSpec((n_mem, 2 * d), lambda i: (i // tiles_per_seq, 0)),
            pl.BlockSpec((d, d), full), pl.BlockSpec((d, d), full), pl.BlockSpec((d, d), full),
            pl.BlockSpec((1, d), full), pl.BlockSpec((1, d), full), pl.BlockSpec((1, d), full),
        ],
        out_specs=pl.BlockSpec((tm, d), lambda i: (i, 0)),
        out_shape=jax.ShapeDtypeStruct((t, d), F32),
        compiler_params=_params(1),
        name="mix_cross",
    )(mo, ro, x2d, kv, w_out, w_cq, w_co, g_mix, g_pre, g_post)


def _ffn_kernel(x_ref, wgu_ref, wd_ref, g_pre_ref, g_post_ref, o_ref, *, d_ff, chunk):
    x = x_ref[...]
    h = _rms(x, g_pre_ref[...]).astype(BF16)
    f = jnp.zeros(x.shape, F32)
    for c0 in range(0, d_ff, chunk):
        gate = _dot(h, wgu_ref[:, c0:c0 + chunk])
        up = _dot(h, wgu_ref[:, d_ff + c0:d_ff + c0 + chunk])
        act = (gate * jax.nn.sigmoid(gate) * up).astype(BF16)
        f = f + _dot(act, wd_ref[c0:c0 + chunk, :])
    o_ref[...] = x + _rms(f, g_post_ref[...])


def _ffn(x2d, w_gate_up, w_down, g_pre, g_post):
    t, d = x2d.shape
    d_ff = w_down.shape[0]
    tm = ROW_TILE
    chunk = d_ff // 2
    assert chunk % LANES == 0
    full = lambda i: (0, 0)
    return pl.pallas_call(
        functools.partial(_ffn_kernel, d_ff=d_ff, chunk=chunk),
        grid=(t // tm,),
        in_specs=[
            pl.BlockSpec((tm, d), lambda i: (i, 0)),
            pl.BlockSpec((d, 2 * d_ff), full),
            pl.BlockSpec((d_ff, d), full),
            pl.BlockSpec((1, d), full), pl.BlockSpec((1, d), full),
        ],
        out_specs=pl.BlockSpec((tm, d), lambda i: (i, 0)),
        out_shape=jax.ShapeDtypeStruct((t, d), F32),
        compiler_params=_params(1),
        name="ffn",
    )(x2d, w_gate_up, w_down, g_pre, g_post)


def kernel(x, mem, g_pre_mix, w_in, w_out, g_post_mix, g_pre_cross, g_mem, w_cq, w_ckv, w_co,
           g_post_cross, g_pre_ffn, w_gate_up, w_down, g_post_ffn):
    batch, seq, d = x.shape
    n_mem = mem.shape[1]
    depth = w_in.shape[0]
    moba_w = MOBA_HEADS * MOBA_HEAD_DIM
    ret_qk_w = RET_HEADS * RET_QK_DIM
    ret_v_w = RET_HEADS * RET_V_DIM
    c_mq, c_mk, c_mv = 0, moba_w, 2 * moba_w
    c_rq = 3 * moba_w
    c_rk = c_rq + ret_qk_w
    c_rv = c_rk + ret_qk_w
    c_rg = c_rv + ret_v_w
    n_proj = c_rg + ret_v_w
    assert w_in.shape[2] == n_proj
    sections = (("mq", c_mq, c_mk), ("mk", c_mk, c_mv), ("mv", c_mv, c_rq), ("rq", c_rq, c_rk),
                ("rk", c_rk, c_rv), ("rv", c_rv, c_rg), ("rg", c_rg, n_proj))

    moba_inv = jnp.power(ROPE_THETA, -jnp.arange(ROPE_DIM // 2, dtype=F32) * 2.0 / ROPE_DIM)
    ret_inv = 1.0 / jnp.power(RET_THETA, jnp.linspace(0.0, 1.0, RET_QK_DIM // 2, dtype=F32))
    tabs = (
        _rotary_tables(seq, moba_inv, ROPE_DIM, MOBA_HEAD_DIM, MOBA_HEAD_DIM ** -0.5),
        _rotary_tables(seq, moba_inv, ROPE_DIM, MOBA_HEAD_DIM, 1.0),
        _rotary_tables(seq, ret_inv, RET_QK_DIM, RET_QK_DIM, 1.0),
        _rotary_tables(seq, ret_inv, RET_QK_DIM, RET_QK_DIM, RET_QK_DIM ** -0.5),
    )

    xf = x.reshape(batch * seq, d)
    mem2d = mem.reshape(batch * n_mem, d)
    row = lambda g: g.reshape(1, d)
    for l in range(depth):
        proj = _in_proj(xf, row(g_pre_mix[l]), w_in[l].astype(BF16), tabs, seq, sections)
        mo = _moba(proj, batch, seq, c_mq, c_mk, c_mv)
        ro = _retention(proj, batch, seq, c_rq, c_rk, c_rv, c_rg)
        kv = _mem_kv(mem2d, row(g_mem[l]), w_ckv[l].astype(BF16), n_mem)
        x2 = _mix_cross(mo, ro, xf, kv, w_out[l].astype(BF16), w_cq[l].astype(BF16),
                        w_co[l].astype(BF16), row(g_post_mix[l]), row(g_pre_cross[l]),
                        row(g_post_cross[l]), seq, n_mem)
        xf = _ffn(x2, w_gate_up[l].astype(BF16), w_down[l].astype(BF16),
                  row(g_pre_ffn[l]), row(g_post_ffn[l]))
    return xf.reshape(batch, seq, d)
```

```python
import functools
import math

import jax
import jax.numpy as jnp
from jax import lax
from jax.experimental import pallas as pl
from jax.experimental.pallas import tpu as pltpu

F32 = jnp.float32
BF16 = jnp.bfloat16

NORM_EPS = 1e-6
NEG_INF = -1e30

LANES = 128
VMEM_LIMIT_BYTES = 56 * 1024 * 1024

MOBA_HEAD_DIM = 64
MOBA_HEADS = 8
MOBA_BLOCK = 256
MOBA_TOPK = 3
ROPE_THETA = 500000.0
ROPE_DIM = MOBA_HEAD_DIM // 4

RET_HEADS = 4
RET_QK_DIM = 64
RET_V_DIM = 128
RET_THETA = 10000.0
RET_CHUNK = 256

CROSS_HEADS = 4

ROW_TILE = 512


def _dot(a, b):
    return jnp.dot(a, b, preferred_element_type=F32)


def _dot_nt(a, b):
    return lax.dot_general(a, b, (((1,), (1,)), ((), ())), preferred_element_type=F32)


def _dot_tn(a, b):
    return lax.dot_general(a, b, (((0,), (0,)), ((), ())), preferred_element_type=F32)


def _rms(x, g):
    return x * lax.rsqrt(jnp.mean(x * x, axis=-1, keepdims=True) + NORM_EPS) * g


def _params(n_grid_dims):
    return pltpu.CompilerParams(
        dimension_semantics=("arbitrary",) * n_grid_dims,
        vmem_limit_bytes=VMEM_LIMIT_BYTES,
    )


def _rotary_tables(seq, inv_freq, rot_dim, head_dim, scale):
    half = rot_dim // 2
    ang = jnp.arange(seq, dtype=F32)[:, None] * inv_freq[None, :]
    cos, sin = jnp.cos(ang), jnp.sin(ang)
    pad = head_dim - rot_dim
    a = jnp.concatenate([cos, cos, jnp.ones((seq, pad), F32)], axis=1)
    bm = jnp.concatenate([-sin, jnp.zeros((seq, half + pad), F32)], axis=1)
    bp = jnp.concatenate([jnp.zeros((seq, half), F32), sin, jnp.zeros((seq, pad), F32)], axis=1)
    reps = LANES // head_dim
    tabs = jnp.stack([jnp.tile(t, (1, reps)) for t in (a, bm, bp)], axis=0)
    return tabs * scale


def _rotate(acc, tab_ref, half):
    a, bm, bp = tab_ref[0], tab_ref[1], tab_ref[2]
    outs = []
    for c in range(acc.shape[1] // LANES):
        xs = acc[:, c * LANES:(c + 1) * LANES]
        outs.append(xs * a + pltpu.roll(xs, LANES - half, 1) * bm + pltpu.roll(xs, half, 1) * bp)
    return jnp.concatenate(outs, axis=1)


def _in_proj_kernel(x_ref, g_ref, w_ref, tmq_ref, tmk_ref, trq_ref, trk_ref, o_ref, *, sections):
    h = _rms(x_ref[...], g_ref[...]).astype(BF16)
    tabs = {"mq": (tmq_ref, ROPE_DIM // 2), "mk": (tmk_ref, ROPE_DIM // 2),
            "rq": (trq_ref, RET_QK_DIM // 2), "rk": (trk_ref, RET_QK_DIM // 2)}
    for kind, c0, c1 in sections:
        acc = _dot(h, w_ref[:, c0:c1])
        if kind in tabs:
            tab_ref, half = tabs[kind]
            acc = _rotate(acc, tab_ref, half)
        o_ref[:, c0:c1] = acc.astype(BF16)


def _in_proj(x2d, g, w, tabs, seq, sections):
    t, d = x2d.shape
    n = w.shape[1]
    tm = ROW_TILE
    tiles_per_seq = seq // tm
    tab_spec = pl.BlockSpec((3, tm, LANES), lambda i: (0, i % tiles_per_seq, 0))
    return pl.pallas_call(
        functools.partial(_in_proj_kernel, sections=sections),
        grid=(t // tm,),
        in_specs=[
            pl.BlockSpec((tm, d), lambda i: (i, 0)),
            pl.BlockSpec((1, d), lambda i: (0, 0)),
            pl.BlockSpec((d, n), lambda i: (0, 0)),
            tab_spec, tab_spec, tab_spec, tab_spec,
        ],
        out_specs=pl.BlockSpec((tm, n), lambda i: (i, 0)),
        out_shape=jax.ShapeDtypeStruct((t, n), BF16),
        compiler_params=_params(1),
        name="in_proj",
    )(x2d, g, w, *tabs)


def _moba_kernel(q_ref, k_ref, v_ref, o_ref, qp_ref, kp_ref, vt_ref):
    seq = q_ref.shape[0]
    blk = MOBA_BLOCK
    nb = seq // blk
    hd = MOBA_HEAD_DIM
    q2 = q_ref[...]
    k2 = k_ref[...]
    v2 = v_ref[...]
    lane = lax.broadcasted_iota(jnp.int32, (seq, LANES), 1)
    kblk = lax.broadcasted_iota(jnp.int32, (seq, LANES), 0) // blk
    kmean2 = jnp.mean(k2.astype(F32).reshape(nb, blk, LANES), axis=1)

    lane8 = lax.broadcasted_iota(jnp.int32, (nb, LANES), 1)
    jidx = lax.broadcasted_iota(jnp.int32, (nb, seq), 0)
    qblk = lax.broadcasted_iota(jnp.int32, (nb, seq), 1) // blk
    prow = lax.broadcasted_iota(jnp.int32, (2 * nb, LANES), 0)
    plane = lax.broadcasted_iota(jnp.int32, (2 * nb, LANES), 1)

    for hh in range(2):
        lo, hi = hh * hd, (hh + 1) * hd
        base = hd - lo
        mine = (lane >= lo) & (lane < hi)
        km = jnp.where((lane8 >= lo) & (lane8 < hi), kmean2, 0.0)
        km_hi = km.astype(BF16)
        km_lo = (km - km_hi.astype(F32)).astype(BF16)
        pad = jnp.zeros((nb, LANES), BF16)
        gate = (_dot_nt(jnp.concatenate([km_hi, pad], axis=0), q2)
                + _dot_nt(jnp.concatenate([km_lo, pad], axis=0), q2))[:nb]
        rank = jnp.zeros((nb, seq), jnp.int32)
        for i in range(nb):
            gi = gate[i:i + 1, :]
            beats = ((gi > gate) | ((gi == gate) & (i < jidx))) & (i < qblk)
            rank = rank + beats.astype(jnp.int32)
        allow = ((jidx < qblk) & (rank < MOBA_TOPK)) | (jidx == qblk)
        allow16 = jnp.concatenate([allow.astype(BF16), jnp.zeros((nb, seq), BF16)], axis=0)
        place = ((plane == base + prow) & (prow < nb)).astype(BF16)
        allow_q = _dot_tn(allow16, place)
        feat = (lane >= base) & (lane < base + nb)
        bias = jnp.where(feat, (allow_q - 1.0) * (-NEG_INF), 0.0).astype(BF16)
        qp_ref[hh] = jnp.where(mine, q2, bias)
        kp_ref[hh] = jnp.where(mine, k2, (lane == base + kblk).astype(BF16))
        vt_ref[hh] = jnp.where(mine, v2, jnp.ones_like(v2)).astype(F32).T.astype(BF16)

    kr = lax.broadcasted_iota(jnp.int32, (blk, blk), 0)
    qcol = lax.broadcasted_iota(jnp.int32, (blk, blk), 1)
    def scores(hh, c, j):
        s = _dot_nt(kp_ref[hh, j * blk:(j + 1) * blk, :], qp_ref[hh, c * blk:(c + 1) * blk, :])
        return jnp.where(kr <= qcol, s, NEG_INF) if j == c else s

    for c in range(nb):
        r0, r1 = c * blk, (c + 1) * blk
        s_cur = [scores(hh, c, 0) for hh in range(2)]
        m_run = [None, None]
        acc = [None, None]
        for j in range(c + 1):
            s_nxt = [scores(hh, c, j + 1) for hh in range(2)] if j < c else None
            for hh in range(2):
                m_t = jnp.max(s_cur[hh], axis=0, keepdims=True)
                m_new = m_t if j == 0 else jnp.maximum(m_run[hh], m_t)
                pv = _dot(vt_ref[hh, :, j * blk:(j + 1) * blk], jnp.exp2(s_cur[hh] - m_new).astype(BF16))
                acc[hh] = pv if j == 0 else acc[hh] * jnp.exp2(m_run[hh] - m_new) + pv
                m_run[hh] = m_new
            s_cur = s_nxt
        res = [acc[hh][hh * hd:(hh + 1) * hd, :] / acc[hh][hd - hh * hd:2 * hd - hh * hd, :] for hh in range(2)]
        o_ref[r0:r1, :] = jnp.concatenate(res, axis=0).T.astype(BF16)


def _moba(proj, batch, seq, q_col, k_col, v_col):
    t = proj.shape[0]
    pairs = MOBA_HEADS // 2

    def spec(col0):
        cb = col0 // LANES
        return pl.BlockSpec((seq, LANES), lambda b, p: (b, cb + p))

    return pl.pallas_call(
        _moba_kernel,
        grid=(batch, pairs),
        in_specs=[spec(q_col), spec(k_col), spec(v_col)],
        out_specs=pl.BlockSpec((seq, LANES), lambda b, p: (b, p)),
        out_shape=jax.ShapeDtypeStruct((t, pairs * LANES), BF16),
        scratch_shapes=[pltpu.VMEM((2, seq, LANES), BF16), pltpu.VMEM((2, seq, LANES), BF16),
                        pltpu.VMEM((2, LANES, seq), BF16)],
        compiler_params=_params(2),
        name="moba",
    )(proj, proj, proj)


_RET_LOG_G = [math.log(1.0 - 2.0 ** (-5.0 - h)) for h in range(RET_HEADS)]


def _retention_kernel(q_ref, k_ref, v_ref, g_ref, o_ref):
    seq = q_ref.shape[0]
    ch = RET_CHUNK
    dv = RET_V_DIM
    pair = pl.program_id(1)
    lane = lax.broadcasted_iota(jnp.int32, (ch, LANES), 1)
    rowf = lax.broadcasted_iota(jnp.int32, (ch, LANES), 0).astype(F32)
    ri = lax.broadcasted_iota(jnp.int32, (ch, ch), 0)
    ci = lax.broadcasted_iota(jnp.int32, (ch, ch), 1)
    diff = (ri - ci).astype(F32)
    for hh in range(2):
        log_g = jnp.where(pair == 0, _RET_LOG_G[hh], _RET_LOG_G[2 + hh]).astype(F32)
        inner_decay = jnp.where(diff >= 0, jnp.exp(log_g * jnp.maximum(diff, 0.0)), 0.0)
        q_decay = jnp.exp(log_g * (rowf + 1.0))
        k_decay = jnp.exp(log_g * (ch - 1.0 - rowf))
        chunk_decay = jnp.exp(jnp.full((LANES, dv), ch, F32) * log_g)
        mine = (lane >= hh * RET_QK_DIM) & (lane < (hh + 1) * RET_QK_DIM)
        state = jnp.zeros((LANES, dv), F32)
        for n in range(seq // ch):
            r0, r1 = n * ch, (n + 1) * ch
            qc = jnp.where(mine, q_ref[r0:r1, :], jnp.zeros((), BF16))
            kc = k_ref[r0:r1, :]
            vc = v_ref[r0:r1, hh * dv:(hh + 1) * dv]
            attn = _dot_nt(qc, kc) * inner_decay
            out = _dot(attn.astype(BF16), vc)
            out = out + _dot((qc.astype(F32) * q_decay).astype(BF16), state.astype(BF16))
            state = state * chunk_decay + _dot_tn((kc.astype(F32) * k_decay).astype(BF16), vc)
            out = out * lax.rsqrt(jnp.mean(out * out, axis=-1, keepdims=True) + NORM_EPS)
            gate = g_ref[r0:r1, hh * dv:(hh + 1) * dv].astype(F32)
            o_ref[r0:r1, hh * dv:(hh + 1) * dv] = (gate * jax.nn.sigmoid(gate) * out).astype(BF16)


def _retention(proj, batch, seq, q_col, k_col, v_col, g_col):
    t = proj.shape[0]
    pairs = RET_HEADS // 2
    wv = 2 * RET_V_DIM

    def spec(col0, width):
        cb = col0 // width
        return pl.BlockSpec((seq, width), lambda b, p: (b, cb + p))

    return pl.pallas_call(
        _retention_kernel,
        grid=(batch, pairs),
        in_specs=[spec(q_col, LANES), spec(k_col, LANES), spec(v_col, wv), spec(g_col, wv)],
        out_specs=pl.BlockSpec((seq, wv), lambda b, p: (b, p)),
        out_shape=jax.ShapeDtypeStruct((t, pairs * wv), BF16),
        compiler_params=_params(2),
        name="retention",
    )(proj, proj, proj, proj)


def _mem_kv_kernel(m_ref, g_ref, w_ref, o_ref):
    o_ref[...] = _dot(_rms(m_ref[...], g_ref[...]).astype(BF16), w_ref[...]).astype(BF16)


def _mem_kv(mem2d, g, w, rows):
    t, d = mem2d.shape
    n = w.shape[1]
    return pl.pallas_call(
        _mem_kv_kernel,
        grid=(t // rows,),
        in_specs=[pl.BlockSpec((rows, d), lambda i: (i, 0)),
                  pl.BlockSpec((1, d), lambda i: (0, 0)),
                  pl.BlockSpec((d, n), lambda i: (0, 0))],
        out_specs=pl.BlockSpec((rows, n), lambda i: (i, 0)),
        out_shape=jax.ShapeDtypeStruct((t, n), BF16),
        compiler_params=_params(1),
        name="mem_kv",
    )(mem2d, g, w)


def _mix_cross_kernel(mo_ref, ro_ref, x_ref, kv_ref, wo_ref, wq_ref, wc_ref,
                      g_mix_ref, g_pre_ref, g_post_ref, o_ref):
    d = x_ref.shape[1]
    half = mo_ref.shape[1]
    dc = d // CROSS_HEADS
    mix = _dot(mo_ref[...], wo_ref[0:half, :]) + _dot(ro_ref[...], wo_ref[half:, :])
    x1 = x_ref[...] + _rms(mix, g_mix_ref[...])
    h = _rms(x1, g_pre_ref[...]).astype(BF16)
    cq = (_dot(h, wq_ref[...]) * (dc ** -0.5)).astype(BF16)
    heads = []
    for hc in range(CROSS_HEADS):
        c0, c1 = hc * dc, (hc + 1) * dc
        s = _dot_nt(cq[:, c0:c1], kv_ref[:, c0:c1])
        e = jnp.exp(s - jnp.max(s, axis=1, keepdims=True))
        o = _dot(e.astype(BF16), kv_ref[:, d + c0:d + c1])
        heads.append((o / jnp.sum(e, axis=1, keepdims=True)).astype(BF16))
    c = _dot(jnp.concatenate(heads, axis=1), wc_ref[...])
    o_ref[...] = x1 + _rms(c, g_post_ref[...])


def _mix_cross(mo, ro, x2d, kv, w_out, w_cq, w_co, g_mix, g_pre, g_post, seq, n_mem):
    t, d = x2d.shape
    tm = ROW_TILE
    tiles_per_seq = seq // tm
    half = mo.shape[1]
    full = lambda i: (0, 0)
    return pl.pallas_call(
        _mix_cross_kernel,
        grid=(t // tm,),
        in_specs=[
            pl.BlockSpec((tm, half), lambda i: (i, 0)),
            pl.BlockSpec((tm, half), lambda i: (i, 0)),
            pl.BlockSpec((tm, d), lambda i: (i, 0)),
            pl.BlockSpec((n_mem, 2 * d), lambda i: (i // tiles_per_seq, 0)),
            pl.BlockSpec((d, d), full), pl.BlockSpec((d, d), full), pl.BlockSpec((d, d), full),
            pl.BlockSpec((1, d), full), pl.BlockSpec((1, d), full), pl.BlockSpec((1, d), full),
        ],
        out_specs=pl.BlockSpec((tm, d), lambda i: (i, 0)),
        out_shape=jax.ShapeDtypeStruct((t, d), F32),
        compiler_params=_params(1),
        name="mix_cross",
    )(mo, ro, x2d, kv, w_out, w_cq, w_co, g_mix, g_pre, g_post)


def _ffn_kernel(x_ref, wgu_ref, wd_ref, g_pre_ref, g_post_ref, o_ref, *, d_ff, chunk):
    x = x_ref[...]
    h = _rms(x, g_pre_ref[...]).astype(BF16)
    f = jnp.zeros(x.shape, F32)
    for c0 in range(0, d_ff, chunk):
        gate = _dot(h, wgu_ref[:, c0:c0 + chunk])
        up = _dot(h, wgu_ref[:, d_ff + c0:d_ff + c0 + chunk])
        act = (gate * jax.nn.sigmoid(gate) * up).astype(BF16)
        f = f + _dot(act, wd_ref[c0:c0 + chunk, :])
    o_ref[...] = x + _rms(f, g_post_ref[...])


def _ffn(x2d, w_gate_up, w_down, g_pre, g_post):
    t, d = x2d.shape
    d_ff = w_down.shape[0]
    tm = ROW_TILE
    chunk = d_ff // 2
    assert chunk % LANES == 0
    full = lambda i: (0, 0)
    return pl.pallas_call(
        functools.partial(_ffn_kernel, d_ff=d_ff, chunk=chunk),
        grid=(t // tm,),
        in_specs=[
            pl.BlockSpec((tm, d), lambda i: (i, 0)),
            pl.BlockSpec((d, 2 * d_ff), full),
            pl.BlockSpec((d_ff, d), full),
            pl.BlockSpec((1, d), full), pl.BlockSpec((1, d), full),
        ],
        out_specs=pl.BlockSpec((tm, d), lambda i: (i, 0)),
        out_shape=jax.ShapeDtypeStruct((t, d), F32),
        compiler_params=_params(1),
        name="ffn",
    )(x2d, w_gate_up, w_down, g_pre, g_post)


def kernel(x, mem, g_pre_mix, w_in, w_out, g_post_mix, g_pre_cross, g_mem, w_cq, w_ckv, w_co,
           g_post_cross, g_pre_ffn, w_gate_up, w_down, g_post_ffn):
    batch, seq, d = x.shape
    n_mem = mem.shape[1]
    depth = w_in.shape[0]
    moba_w = MOBA_HEADS * MOBA_HEAD_DIM
    ret_qk_w = RET_HEADS * RET_QK_DIM
    ret_v_w = RET_HEADS * RET_V_DIM
    c_mq, c_mk, c_mv = 0, moba_w, 2 * moba_w
    c_rq = 3 * moba_w
    c_rk = c_rq + ret_qk_w
    c_rv = c_rk + ret_qk_w
    c_rg = c_rv + ret_v_w
    n_proj = c_rg + ret_v_w
    assert w_in.shape[2] == n_proj
    sections = (("mq", c_mq, c_mk), ("mk", c_mk, c_mv), ("mv", c_mv, c_rq), ("rq", c_rq, c_rk),
                ("rk", c_rk, c_rv), ("rv", c_rv, c_rg), ("rg", c_rg, n_proj))

    moba_inv = jnp.power(ROPE_THETA, -jnp.arange(ROPE_DIM // 2, dtype=F32) * 2.0 / ROPE_DIM)
    ret_inv = 1.0 / jnp.power(RET_THETA, jnp.linspace(0.0, 1.0, RET_QK_DIM // 2, dtype=F32))
    tabs = (
        _rotary_tables(seq, moba_inv, ROPE_DIM, MOBA_HEAD_DIM, MOBA_HEAD_DIM ** -0.5 * math.log2(math.e)),
        _rotary_tables(seq, moba_inv, ROPE_DIM, MOBA_HEAD_DIM, 1.0),
        _rotary_tables(seq, ret_inv, RET_QK_DIM, RET_QK_DIM, 1.0),
        _rotary_tables(seq, ret_inv, RET_QK_DIM, RET_QK_DIM, RET_QK_DIM ** -0.5),
    )

    xf = x.reshape(batch * seq, d)
    mem2d = mem.reshape(batch * n_mem, d)
    row = lambda g: g.reshape(1, d)
    for l in range(depth):
        proj = _in_proj(xf, row(g_pre_mix[l]), w_in[l].astype(BF16), tabs, seq, sections)
        mo = _moba(proj, batch, seq, c_mq, c_mk, c_mv)
        ro = _retention(proj, batch, seq, c_rq, c_rk, c_rv, c_rg)
        kv = _mem_kv(mem2d, row(g_mem[l]), w_ckv[l].astype(BF16), n_mem)
        x2 = _mix_cross(mo, ro, xf, kv, w_out[l].astype(BF16), w_cq[l].astype(BF16),
                        w_co[l].astype(BF16), row(g_post_mix[l]), row(g_pre_cross[l]),
                        row(g_post_cross[l]), seq, n_mem)
        xf = _ffn(x2, w_gate_up[l].astype(BF16), w_down[l].astype(BF16),
                  row(g_pre_ffn[l]), row(g_post_ffn[l]))
    return xf.reshape(batch, seq, d)
```

```python
import functools
import math

import jax
import jax.numpy as jnp
from jax import lax
from jax.experimental import pallas as pl
from jax.experimental.pallas import tpu as pltpu

F32 = jnp.float32
BF16 = jnp.bfloat16

NORM_EPS = 1e-6
NEG_INF = -1e30

LANES = 128
VMEM_LIMIT_BYTES = 56 * 1024 * 1024

MOBA_HEAD_DIM = 64
MOBA_HEADS = 8
MOBA_BLOCK = 256
MOBA_TOPK = 3
MOBA_PREFETCH = 2
ROPE_THETA = 500000.0
ROPE_DIM = MOBA_HEAD_DIM // 4

RET_HEADS = 4
RET_QK_DIM = 64
RET_V_DIM = 128
RET_THETA = 10000.0
RET_CHUNK = 256

CROSS_HEADS = 4

ROW_TILE = 512


def _dot(a, b):
    return jnp.dot(a, b, preferred_element_type=F32)


def _dot_nt(a, b):
    return lax.dot_general(a, b, (((1,), (1,)), ((), ())), preferred_element_type=F32)


def _dot_tn(a, b):
    return lax.dot_general(a, b, (((0,), (0,)), ((), ())), preferred_element_type=F32)


def _rms(x, g):
    return x * lax.rsqrt(jnp.mean(x * x, axis=-1, keepdims=True) + NORM_EPS) * g


def _params(n_grid_dims):
    return pltpu.CompilerParams(
        dimension_semantics=("arbitrary",) * n_grid_dims,
        vmem_limit_bytes=VMEM_LIMIT_BYTES,
    )


def _rotary_tables(seq, inv_freq, rot_dim, head_dim, scale):
    half = rot_dim // 2
    ang = jnp.arange(seq, dtype=F32)[:, None] * inv_freq[None, :]
    cos, sin = jnp.cos(ang), jnp.sin(ang)
    pad = head_dim - rot_dim
    a = jnp.concatenate([cos, cos, jnp.ones((seq, pad), F32)], axis=1)
    bm = jnp.concatenate([-sin, jnp.zeros((seq, half + pad), F32)], axis=1)
    bp = jnp.concatenate([jnp.zeros((seq, half), F32), sin, jnp.zeros((seq, pad), F32)], axis=1)
    reps = LANES // head_dim
    tabs = jnp.stack([jnp.tile(t, (1, reps)) for t in (a, bm, bp)], axis=0)
    return tabs * scale


def _rotate(acc, tab_ref, half):
    a, bm, bp = tab_ref[0], tab_ref[1], tab_ref[2]
    outs = []
    for c in range(acc.shape[1] // LANES):
        xs = acc[:, c * LANES:(c + 1) * LANES]
        outs.append(xs * a + pltpu.roll(xs, LANES - half, 1) * bm + pltpu.roll(xs, half, 1) * bp)
    return jnp.concatenate(outs, axis=1)


def _in_proj_kernel(x_ref, g_ref, w_ref, tmq_ref, tmk_ref, trq_ref, trk_ref, o_ref, *, sections):
    h = _rms(x_ref[...], g_ref[...]).astype(BF16)
    tabs = {"mq": (tmq_ref, ROPE_DIM // 2), "mk": (tmk_ref, ROPE_DIM // 2),
            "rq": (trq_ref, RET_QK_DIM // 2), "rk": (trk_ref, RET_QK_DIM // 2)}
    for kind, c0, c1 in sections:
        acc = _dot(h, w_ref[:, c0:c1])
        if kind in tabs:
            tab_ref, half = tabs[kind]
            acc = _rotate(acc, tab_ref, half)
        o_ref[:, c0:c1] = acc.astype(BF16)


def _in_proj(x2d, g, w, tabs, seq, sections):
    t, d = x2d.shape
    n = w.shape[1]
    tm = ROW_TILE
    tiles_per_seq = seq // tm
    tab_spec = pl.BlockSpec((3, tm, LANES), lambda i: (0, i % tiles_per_seq, 0))
    return pl.pallas_call(
        functools.partial(_in_proj_kernel, sections=sections),
        grid=(t // tm,),
        in_specs=[
            pl.BlockSpec((tm, d), lambda i: (i, 0)),
            pl.BlockSpec((1, d), lambda i: (0, 0)),
            pl.BlockSpec((d, n), lambda i: (0, 0)),
            tab_spec, tab_spec, tab_spec, tab_spec,
        ],
        out_specs=pl.BlockSpec((tm, n), lambda i: (i, 0)),
        out_shape=jax.ShapeDtypeStruct((t, n), BF16),
        compiler_params=_params(1),
        name="in_proj",
    )(x2d, g, w, *tabs)


def _moba_kernel(q_ref, k_ref, v_ref, o_ref, qp_ref, vt_ref, masked_ref):
    seq = q_ref.shape[0]
    blk = MOBA_BLOCK
    nb = seq // blk
    hd = MOBA_HEAD_DIM
    q2 = q_ref[...]
    k2 = k_ref[...]
    v2 = v_ref[...]
    lane = lax.broadcasted_iota(jnp.int32, (seq, LANES), 1)
    kmean2 = jnp.mean(k2.astype(F32).reshape(nb, blk, LANES), axis=1)

    lane8 = lax.broadcasted_iota(jnp.int32, (nb, LANES), 1)
    km = jnp.concatenate([jnp.where(lane8 < hd, kmean2, 0.0), jnp.where(lane8 >= hd, kmean2, 0.0)], axis=0)
    km_hi = km.astype(BF16)
    km_lo = (km - km_hi.astype(F32)).astype(BF16)
    gate2 = _dot_nt(jnp.concatenate([km_hi, km_lo], axis=0), q2)
    jidx = lax.broadcasted_iota(jnp.int32, (nb, seq), 0)
    qblk = lax.broadcasted_iota(jnp.int32, (nb, seq), 1) // blk
    for hh in range(2):
        gate = gate2[hh * nb:(hh + 1) * nb] + gate2[(2 + hh) * nb:(3 + hh) * nb]
        rank = jnp.zeros((nb, seq), jnp.int32)
        for i in range(nb):
            gi = gate[i:i + 1, :]
            beats = ((gi > gate) | ((gi == gate) & (i < jidx))) & (i < qblk)
            rank = rank + beats.astype(jnp.int32)
        masked_ref[hh] = ((jidx >= qblk) | (rank >= MOBA_TOPK)).astype(F32)
        mine = (lane >= hh * hd) & (lane < (hh + 1) * hd)
        qp_ref[hh] = jnp.where(mine, q2, jnp.zeros_like(q2))
        vt_ref[hh] = jnp.where(mine, v2, jnp.ones_like(v2)).astype(F32).T.astype(BF16)

    kr = lax.broadcasted_iota(jnp.int32, (blk, blk), 0)
    qcol = lax.broadcasted_iota(jnp.int32, (blk, blk), 1)

    def scores(t):
        c, j = tiles[t]
        out = []
        for hh in range(2):
            s = _dot_nt(k_ref[j * blk:(j + 1) * blk, :], qp_ref[hh, c * blk:(c + 1) * blk, :])
            out.append(jnp.where(kr <= qcol, s, NEG_INF) if j == c else s)
        return out

    tiles = [(c, j) for c in range(nb) for j in range(c + 1)]
    pending = {t: scores(t) for t in range(min(MOBA_PREFETCH, len(tiles)))}
    m_run = [None, None]
    acc = [None, None]
    for t, (c, j) in enumerate(tiles):
        r0, r1 = c * blk, (c + 1) * blk
        if t + MOBA_PREFETCH < len(tiles):
            pending[t + MOBA_PREFETCH] = scores(t + MOBA_PREFETCH)
        s_cur = pending.pop(t)
        for hh in range(2):
            m_t = jnp.max(s_cur[hh], axis=0, keepdims=True)
            if j < c:
                off = masked_ref[hh, j:j + 1, r0:r1] > 0.5
                m_t = jnp.where(off, NEG_INF, m_t)
            m_new = m_t if j == 0 else jnp.maximum(m_run[hh], m_t)
            shift = jnp.where(off, -NEG_INF, m_new) if j < c else m_new
            pv = _dot(vt_ref[hh, :, j * blk:(j + 1) * blk], jnp.exp2(s_cur[hh] - shift).astype(BF16))
            acc[hh] = pv if j == 0 else acc[hh] * jnp.exp2(m_run[hh] - m_new) + pv
            m_run[hh] = m_new
        if j == c:
            res = [acc[hh][hh * hd:(hh + 1) * hd, :] / acc[hh][hd - hh * hd:2 * hd - hh * hd, :]
                   for hh in range(2)]
            o_ref[r0:r1, :] = jnp.concatenate(res, axis=0).T.astype(BF16)


def _moba(proj, batch, seq, q_col, k_col, v_col):
    t = proj.shape[0]
    pairs = MOBA_HEADS // 2

    def spec(col0):
        cb = col0 // LANES
        return pl.BlockSpec((seq, LANES), lambda b, p: (b, cb + p))

    return pl.pallas_call(
        _moba_kernel,
        grid=(batch, pairs),
        in_specs=[spec(q_col), spec(k_col), spec(v_col)],
        out_specs=pl.BlockSpec((seq, LANES), lambda b, p: (b, p)),
        out_shape=jax.ShapeDtypeStruct((t, pairs * LANES), BF16),
        scratch_shapes=[pltpu.VMEM((2, seq, LANES), BF16), pltpu.VMEM((2, LANES, seq), BF16),
                        pltpu.VMEM((2, seq // MOBA_BLOCK, seq), F32)],
        compiler_params=_params(2),
        name="moba",
    )(proj, proj, proj)


_RET_LOG_G = [math.log(1.0 - 2.0 ** (-5.0 - h)) for h in range(RET_HEADS)]


def _retention_kernel(q_ref, k_ref, v_ref, g_ref, o_ref):
    seq = q_ref.shape[0]
    ch = RET_CHUNK
    dv = RET_V_DIM
    pair = pl.program_id(1)
    lane = lax.broadcasted_iota(jnp.int32, (ch, LANES), 1)
    rowf = lax.broadcasted_iota(jnp.int32, (ch, LANES), 0).astype(F32)
    ri = lax.broadcasted_iota(jnp.int32, (ch, ch), 0)
    ci = lax.broadcasted_iota(jnp.int32, (ch, ch), 1)
    diff = (ri - ci).astype(F32)
    for hh in range(2):
        log_g = jnp.where(pair == 0, _RET_LOG_G[hh], _RET_LOG_G[2 + hh]).astype(F32)
        inner_decay = jnp.where(diff >= 0, jnp.exp(log_g * jnp.maximum(diff, 0.0)), 0.0)
        q_decay = jnp.exp(log_g * (rowf + 1.0))
        k_decay = jnp.exp(log_g * (ch - 1.0 - rowf))
        chunk_decay = jnp.exp(jnp.full((LANES, dv), ch, F32) * log_g)
        mine = (lane >= hh * RET_QK_DIM) & (lane < (hh + 1) * RET_QK_DIM)
        state = jnp.zeros((LANES, dv), F32)
        for n in range(seq // ch):
            r0, r1 = n * ch, (n + 1) * ch
            qc = jnp.where(mine, q_ref[r0:r1, :], jnp.zeros((), BF16))
            kc = k_ref[r0:r1, :]
            vc = v_ref[r0:r1, hh * dv:(hh + 1) * dv]
            attn = _dot_nt(qc, kc) * inner_decay
            out = _dot(attn.astype(BF16), vc)
            out = out + _dot((qc.astype(F32) * q_decay).astype(BF16), state.astype(BF16))
            state = state * chunk_decay + _dot_tn((kc.astype(F32) * k_decay).astype(BF16), vc)
            out = out * lax.rsqrt(jnp.mean(out * out, axis=-1, keepdims=True) + NORM_EPS)
            gate = g_ref[r0:r1, hh * dv:(hh + 1) * dv].astype(F32)
            o_ref[r0:r1, hh * dv:(hh + 1) * dv] = (gate * jax.nn.sigmoid(gate) * out).astype(BF16)


def _retention(proj, batch, seq, q_col, k_col, v_col, g_col):
    t = proj.shape[0]
    pairs = RET_HEADS // 2
    wv = 2 * RET_V_DIM

    def spec(col0, width):
        cb = col0 // width
        return pl.BlockSpec((seq, width), lambda b, p: (b, cb + p))

    return pl.pallas_call(
        _retention_kernel,
        grid=(batch, pairs),
        in_specs=[spec(q_col, LANES), spec(k_col, LANES), spec(v_col, wv), spec(g_col, wv)],
        out_specs=pl.BlockSpec((seq, wv), lambda b, p: (b, p)),
        out_shape=jax.ShapeDtypeStruct((t, pairs * wv), BF16),
        compiler_params=_params(2),
        name="retention",
    )(proj, proj, proj, proj)


def _mem_kv_kernel(m_ref, g_ref, w_ref, o_ref):
    o_ref[...] = _dot(_rms(m_ref[...], g_ref[...]).astype(BF16), w_ref[...]).astype(BF16)


def _mem_kv(mem2d, g, w, rows):
    t, d = mem2d.shape
    n = w.shape[1]
    return pl.pallas_call(
        _mem_kv_kernel,
        grid=(t // rows,),
        in_specs=[pl.BlockSpec((rows, d), lambda i: (i, 0)),
                  pl.BlockSpec((1, d), lambda i: (0, 0)),
                  pl.BlockSpec((d, n), lambda i: (0, 0))],
        out_specs=pl.BlockSpec((rows, n), lambda i: (i, 0)),
        out_shape=jax.ShapeDtypeStruct((t, n), BF16),
        compiler_params=_params(1),
        name="mem_kv",
    )(mem2d, g, w)


def _mix_cross_kernel(mo_ref, ro_ref, x_ref, kv_ref, wo_ref, wq_ref, wc_ref,
                      g_mix_ref, g_pre_ref, g_post_ref, o_ref):
    d = x_ref.shape[1]
    half = mo_ref.shape[1]
    dc = d // CROSS_HEADS
    mix = _dot(mo_ref[...], wo_ref[0:half, :]) + _dot(ro_ref[...], wo_ref[half:, :])
    x1 = x_ref[...] + _rms(mix, g_mix_ref[...])
    h = _rms(x1, g_pre_ref[...]).astype(BF16)
    cq = (_dot(h, wq_ref[...]) * (dc ** -0.5)).astype(BF16)
    heads = []
    for hc in range(CROSS_HEADS):
        c0, c1 = hc * dc, (hc + 1) * dc
        s = _dot_nt(cq[:, c0:c1], kv_ref[:, c0:c1])
        e = jnp.exp(s - jnp.max(s, axis=1, keepdims=True))
        o = _dot(e.astype(BF16), kv_ref[:, d + c0:d + c1])
        heads.append((o / jnp.sum(e, axis=1, keepdims=True)).astype(BF16))
    c = _dot(jnp.concatenate(heads, axis=1), wc_ref[...])
    o_ref[...] = x1 + _rms(c, g_post_ref[...])


def _mix_cross(mo, ro, x2d, kv, w_out, w_cq, w_co, g_mix, g_pre, g_post, seq, n_mem):
    t, d = x2d.shape
    tm = ROW_TILE
    tiles_per_seq = seq // tm
    half = mo.shape[1]
    full = lambda i: (0, 0)
    return pl.pallas_call(
        _mix_cross_kernel,
        grid=(t // tm,),
        in_specs=[
            pl.BlockSpec((tm, half), lambda i: (i, 0)),
            pl.BlockSpec((tm, half), lambda i: (i, 0)),
            pl.BlockSpec((tm, d), lambda i: (i, 0)),
            pl.BlockSpec((n_mem, 2 * d), lambda i: (i // tiles_per_seq, 0)),
            pl.BlockSpec((d, d), full), pl.BlockSpec((d, d), full), pl.BlockSpec((d, d), full),
            pl.BlockSpec((1, d), full), pl.BlockSpec((1, d), full), pl.BlockSpec((1, d), full),
        ],
        out_specs=pl.BlockSpec((tm, d), lambda i: (i, 0)),
        out_shape=jax.ShapeDtypeStruct((t, d), F32),
        compiler_params=_params(1),
        name="mix_cross",
    )(mo, ro, x2d, kv, w_out, w_cq, w_co, g_mix, g_pre, g_post)


def _ffn_kernel(x_ref, wgu_ref, wd_ref, g_pre_ref, g_post_ref, o_ref, *, d_ff, chunk):
    x = x_ref[...]
    h = _rms(x, g_pre_ref[...]).astype(BF16)
    f = jnp.zeros(x.shape, F32)
    for c0 in range(0, d_ff, chunk):
        gate = _dot(h, wgu_ref[:, c0:c0 + chunk])
        up = _dot(h, wgu_ref[:, d_ff + c0:d_ff + c0 + chunk])
        act = (gate * jax.nn.sigmoid(gate) * up).astype(BF16)
        f = f + _dot(act, wd_ref[c0:c0 + chunk, :])
    o_ref[...] = x + _rms(f, g_post_ref[...])


def _ffn(x2d, w_gate_up, w_down, g_pre, g_post):
    t, d = x2d.shape
    d_ff = w_down.shape[0]
    tm = ROW_TILE
    chunk = d_ff // 2
    assert chunk % LANES == 0
    full = lambda i: (0, 0)
    return pl.pallas_call(
        functools.partial(_ffn_kernel, d_ff=d_ff, chunk=chunk),
        grid=(t // tm,),
        in_specs=[
            pl.BlockSpec((tm, d), lambda i: (i, 0)),
            pl.BlockSpec((d, 2 * d_ff), full),
            pl.BlockSpec((d_ff, d), full),
            pl.BlockSpec((1, d), full), pl.BlockSpec((1, d), full),
        ],
        out_specs=pl.BlockSpec((tm, d), lambda i: (i, 0)),
        out_shape=jax.ShapeDtypeStruct((t, d), F32),
        compiler_params=_params(1),
        name="ffn",
    )(x2d, w_gate_up, w_down, g_pre, g_post)


def kernel(x, mem, g_pre_mix, w_in, w_out, g_post_mix, g_pre_cross, g_mem, w_cq, w_ckv, w_co,
           g_post_cross, g_pre_ffn, w_gate_up, w_down, g_post_ffn):
    batch, seq, d = x.shape
    n_mem = mem.shape[1]
    depth = w_in.shape[0]
    moba_w = MOBA_HEADS * MOBA_HEAD_DIM
    ret_qk_w = RET_HEADS * RET_QK_DIM
    ret_v_w = RET_HEADS * RET_V_DIM
    c_mq, c_mk, c_mv = 0, moba_w, 2 * moba_w
    c_rq = 3 * moba_w
    c_rk = c_rq + ret_qk_w
    c_rv = c_rk + ret_qk_w
    c_rg = c_rv + ret_v_w
    n_proj = c_rg + ret_v_w
    assert w_in.shape[2] == n_proj
    sections = (("mq", c_mq, c_mk), ("mk", c_mk, c_mv), ("mv", c_mv, c_rq), ("rq", c_rq, c_rk),
                ("rk", c_rk, c_rv), ("rv", c_rv, c_rg), ("rg", c_rg, n_proj))

    moba_inv = jnp.power(ROPE_THETA, -jnp.arange(ROPE_DIM // 2, dtype=F32) * 2.0 / ROPE_DIM)
    ret_inv = 1.0 / jnp.power(RET_THETA, jnp.linspace(0.0, 1.0, RET_QK_DIM // 2, dtype=F32))
    tabs = (
        _rotary_tables(seq, moba_inv, ROPE_DIM, MOBA_HEAD_DIM, MOBA_HEAD_DIM ** -0.5 * math.log2(math.e)),
        _rotary_tables(seq, moba_inv, ROPE_DIM, MOBA_HEAD_DIM, 1.0),
        _rotary_tables(seq, ret_inv, RET_QK_DIM, RET_QK_DIM, 1.0),
        _rotary_tables(seq, ret_inv, RET_QK_DIM, RET_QK_DIM, RET_QK_DIM ** -0.5),
    )

    xf = x.reshape(batch * seq, d)
    mem2d = mem.reshape(batch * n_mem, d)
    row = lambda g: g.reshape(1, d)
    for l in range(depth):
        proj = _in_proj(xf, row(g_pre_mix[l]), w_in[l].astype(BF16), tabs, seq, sections)
        mo = _moba(proj, batch, seq, c_mq, c_mk, c_mv)
        ro = _retention(proj, batch, seq, c_rq, c_rk, c_rv, c_rg)
        kv = _mem_kv(mem2d, row(g_mem[l]), w_ckv[l].astype(BF16), n_mem)
        x2 = _mix_cross(mo, ro, xf, kv, w_out[l].astype(BF16), w_cq[l].astype(BF16),
                        w_co[l].astype(BF16), row(g_post_mix[l]), row(g_pre_cross[l]),
                        row(g_post_cross[l]), seq, n_mem)
        xf = _ffn(x2, w_gate_up[l].astype(BF16), w_down[l].astype(BF16),
                  row(g_pre_ffn[l]), row(g_post_ffn[l]))
    return xf.reshape(batch, seq, d)
```

```python
import functools
import math

import jax
import jax.numpy as jnp
from jax import lax
from jax.experimental import pallas as pl
from jax.experimental.pallas import tpu as pltpu

F32 = jnp.float32
BF16 = jnp.bfloat16

NORM_EPS = 1e-6
NEG_INF = -1e30

LANES = 128
VMEM_LIMIT_BYTES = 56 * 1024 * 1024

MOBA_HEAD_DIM = 64
MOBA_HEADS = 8
MOBA_BLOCK = 256
MOBA_TOPK = 3
MOBA_PREFETCH = 2
ROPE_THETA = 500000.0
ROPE_DIM = MOBA_HEAD_DIM // 4

RET_HEADS = 4
RET_QK_DIM = 64
RET_V_DIM = 128
RET_THETA = 10000.0
RET_CHUNK = 256

CROSS_HEADS = 4

ROW_TILE = 512


def _dot(a, b):
    return jnp.dot(a, b, preferred_element_type=F32)


def _wdot(a, w):
    return jnp.dot(a, w.astype(BF16), preferred_element_type=F32)


def _dot_nt(a, b):
    return lax.dot_general(a, b, (((1,), (1,)), ((), ())), preferred_element_type=F32)


def _dot_tn(a, b):
    return lax.dot_general(a, b, (((0,), (0,)), ((), ())), preferred_element_type=F32)


def _rms(x, g):
    return x * lax.rsqrt(jnp.mean(x * x, axis=-1, keepdims=True) + NORM_EPS) * g


def _params(n_grid_dims):
    return pltpu.CompilerParams(
        dimension_semantics=("arbitrary",) * n_grid_dims,
        vmem_limit_bytes=VMEM_LIMIT_BYTES,
    )


def _rotary_tables(seq, inv_freq, rot_dim, head_dim, scale):
    half = rot_dim // 2
    ang = jnp.arange(seq, dtype=F32)[:, None] * inv_freq[None, :]
    cos, sin = jnp.cos(ang), jnp.sin(ang)
    pad = head_dim - rot_dim
    a = jnp.concatenate([cos, cos, jnp.ones((seq, pad), F32)], axis=1)
    bm = jnp.concatenate([-sin, jnp.zeros((seq, half + pad), F32)], axis=1)
    bp = jnp.concatenate([jnp.zeros((seq, half), F32), sin, jnp.zeros((seq, pad), F32)], axis=1)
    reps = LANES // head_dim
    tabs = jnp.stack([jnp.tile(t, (1, reps)) for t in (a, bm, bp)], axis=0)
    return tabs * scale


def _rotate(acc, tab_ref, half):
    a, bm, bp = tab_ref[0], tab_ref[1], tab_ref[2]
    outs = []
    for c in range(acc.shape[1] // LANES):
        xs = acc[:, c * LANES:(c + 1) * LANES]
        outs.append(xs * a + pltpu.roll(xs, LANES - half, 1) * bm + pltpu.roll(xs, half, 1) * bp)
    return jnp.concatenate(outs, axis=1)


def _in_proj_kernel(x_ref, g_ref, w_ref, tmq_ref, tmk_ref, trq_ref, trk_ref, o_ref, *, sections):
    h = _rms(x_ref[...], g_ref[...]).astype(BF16)
    tabs = {"mq": (tmq_ref, ROPE_DIM // 2), "mk": (tmk_ref, ROPE_DIM // 2),
            "rq": (trq_ref, RET_QK_DIM // 2), "rk": (trk_ref, RET_QK_DIM // 2)}
    for kind, c0, c1 in sections:
        acc = _wdot(h, w_ref[:, c0:c1])
        if kind in tabs:
            tab_ref, half = tabs[kind]
            acc = _rotate(acc, tab_ref, half)
        o_ref[:, c0:c1] = acc.astype(BF16)


def _in_proj(x2d, g, w, tabs, seq, sections):
    t, d = x2d.shape
    n = w.shape[1]
    tm = ROW_TILE
    tiles_per_seq = seq // tm
    tab_spec = pl.BlockSpec((3, tm, LANES), lambda i: (0, i % tiles_per_seq, 0))
    return pl.pallas_call(
        functools.partial(_in_proj_kernel, sections=sections),
        grid=(t // tm,),
        in_specs=[
            pl.BlockSpec((tm, d), lambda i: (i, 0)),
            pl.BlockSpec((1, d), lambda i: (0, 0)),
            pl.BlockSpec((d, n), lambda i: (0, 0)),
            tab_spec, tab_spec, tab_spec, tab_spec,
        ],
        out_specs=pl.BlockSpec((tm, n), lambda i: (i, 0)),
        out_shape=jax.ShapeDtypeStruct((t, n), BF16),
        compiler_params=_params(1),
        name="in_proj",
    )(x2d, g, w, *tabs)


def _moba_kernel(q_ref, k_ref, v_ref, o_ref, qp_ref, vt_ref, masked_ref):
    seq = q_ref.shape[0]
    blk = MOBA_BLOCK
    nb = seq // blk
    hd = MOBA_HEAD_DIM
    q2 = q_ref[...]
    k2 = k_ref[...]
    v2 = v_ref[...]
    lane = lax.broadcasted_iota(jnp.int32, (seq, LANES), 1)
    kmean2 = jnp.mean(k2.astype(F32).reshape(nb, blk, LANES), axis=1)

    lane8 = lax.broadcasted_iota(jnp.int32, (nb, LANES), 1)
    km = jnp.concatenate([jnp.where(lane8 < hd, kmean2, 0.0), jnp.where(lane8 >= hd, kmean2, 0.0)], axis=0)
    km_hi = km.astype(BF16)
    km_lo = (km - km_hi.astype(F32)).astype(BF16)
    gate2 = _dot_nt(jnp.concatenate([km_hi, km_lo], axis=0), q2)
    jidx = lax.broadcasted_iota(jnp.int32, (nb, seq), 0)
    qblk = lax.broadcasted_iota(jnp.int32, (nb, seq), 1) // blk
    for hh in range(2):
        gate = gate2[hh * nb:(hh + 1) * nb] + gate2[(2 + hh) * nb:(3 + hh) * nb]
        rank = jnp.zeros((nb, seq), jnp.int32)
        for i in range(nb):
            gi = gate[i:i + 1, :]
            beats = ((gi > gate) | ((gi == gate) & (i < jidx))) & (i < qblk)
            rank = rank + beats.astype(jnp.int32)
        masked_ref[hh] = ((jidx >= qblk) | (rank >= MOBA_TOPK)).astype(F32)
        mine = (lane >= hh * hd) & (lane < (hh + 1) * hd)
        qp_ref[hh] = jnp.where(mine, q2, jnp.zeros_like(q2))
        vt_ref[hh] = jnp.where(mine, v2, jnp.ones_like(v2)).astype(F32).T.astype(BF16)

    kr = lax.broadcasted_iota(jnp.int32, (blk, blk), 0)
    qcol = lax.broadcasted_iota(jnp.int32, (blk, blk), 1)

    def scores(t):
        c, j = tiles[t]
        out = []
        for hh in range(2):
            s = _dot_nt(k_ref[j * blk:(j + 1) * blk, :], qp_ref[hh, c * blk:(c + 1) * blk, :])
            out.append(jnp.where(kr <= qcol, s, NEG_INF) if j == c else s)
        return out

    tiles = [(c, j) for c in range(nb) for j in range(c + 1)]
    pending = {t: scores(t) for t in range(min(MOBA_PREFETCH, len(tiles)))}
    m_run = [None, None]
    acc = [None, None]
    for t, (c, j) in enumerate(tiles):
        r0, r1 = c * blk, (c + 1) * blk
        if t + MOBA_PREFETCH < len(tiles):
            pending[t + MOBA_PREFETCH] = scores(t + MOBA_PREFETCH)
        s_cur = pending.pop(t)
        for hh in range(2):
            m_t = jnp.max(s_cur[hh], axis=0, keepdims=True)
            if j < c:
                off = masked_ref[hh, j:j + 1, r0:r1] > 0.5
                m_t = jnp.where(off, NEG_INF, m_t)
            m_new = m_t if j == 0 else jnp.maximum(m_run[hh], m_t)
            shift = jnp.where(off, -NEG_INF, m_new) if j < c else m_new
            pv = _dot(vt_ref[hh, :, j * blk:(j + 1) * blk], jnp.exp2(s_cur[hh] - shift).astype(BF16))
            acc[hh] = pv if j == 0 else acc[hh] * jnp.exp2(m_run[hh] - m_new) + pv
            m_run[hh] = m_new
        if j == c:
            res = [acc[hh][hh * hd:(hh + 1) * hd, :] / acc[hh][hd - hh * hd:2 * hd - hh * hd, :]
                   for hh in range(2)]
            o_ref[r0:r1, :] = jnp.concatenate(res, axis=0).T.astype(BF16)


def _moba(proj, batch, seq, q_col, k_col, v_col):
    t = proj.shape[0]
    pairs = MOBA_HEADS // 2

    def spec(col0):
        cb = col0 // LANES
        return pl.BlockSpec((seq, LANES), lambda b, p: (b, cb + p))

    return pl.pallas_call(
        _moba_kernel,
        grid=(batch, pairs),
        in_specs=[spec(q_col), spec(k_col), spec(v_col)],
        out_specs=pl.BlockSpec((seq, LANES), lambda b, p: (b, p)),
        out_shape=jax.ShapeDtypeStruct((t, pairs * LANES), BF16),
        scratch_shapes=[pltpu.VMEM((2, seq, LANES), BF16), pltpu.VMEM((2, LANES, seq), BF16),
                        pltpu.VMEM((2, seq // MOBA_BLOCK, seq), F32)],
        compiler_params=_params(2),
        name="moba",
    )(proj, proj, proj)


_RET_LOG_G = [math.log(1.0 - 2.0 ** (-5.0 - h)) for h in range(RET_HEADS)]


def _retention_kernel(q_ref, k_ref, v_ref, g_ref, o_ref):
    seq = q_ref.shape[0]
    ch = RET_CHUNK
    dv = RET_V_DIM
    pair = pl.program_id(1)
    lane = lax.broadcasted_iota(jnp.int32, (ch, LANES), 1)
    rowf = lax.broadcasted_iota(jnp.int32, (ch, LANES), 0).astype(F32)
    ri = lax.broadcasted_iota(jnp.int32, (ch, ch), 0)
    ci = lax.broadcasted_iota(jnp.int32, (ch, ch), 1)
    diff = (ri - ci).astype(F32)
    for hh in range(2):
        log_g = jnp.where(pair == 0, _RET_LOG_G[hh], _RET_LOG_G[2 + hh]).astype(F32)
        inner_decay = jnp.where(diff >= 0, jnp.exp(log_g * jnp.maximum(diff, 0.0)), 0.0)
        q_decay = jnp.exp(log_g * (rowf + 1.0))
        k_decay = jnp.exp(log_g * (ch - 1.0 - rowf))
        chunk_decay = jnp.exp(jnp.full((LANES, dv), ch, F32) * log_g)
        mine = (lane >= hh * RET_QK_DIM) & (lane < (hh + 1) * RET_QK_DIM)
        state = jnp.zeros((LANES, dv), F32)
        for n in range(seq // ch):
            r0, r1 = n * ch, (n + 1) * ch
            qc = jnp.where(mine, q_ref[r0:r1, :], jnp.zeros((), BF16))
            kc = k_ref[r0:r1, :]
            vc = v_ref[r0:r1, hh * dv:(hh + 1) * dv]
            attn = _dot_nt(qc, kc) * inner_decay
            out = _dot(attn.astype(BF16), vc)
            out = out + _dot((qc.astype(F32) * q_decay).astype(BF16), state.astype(BF16))
            state = state * chunk_decay + _dot_tn((kc.astype(F32) * k_decay).astype(BF16), vc)
            out = out * lax.rsqrt(jnp.mean(out * out, axis=-1, keepdims=True) + NORM_EPS)
            gate = g_ref[r0:r1, hh * dv:(hh + 1) * dv].astype(F32)
            o_ref[r0:r1, hh * dv:(hh + 1) * dv] = (gate * jax.nn.sigmoid(gate) * out).astype(BF16)


def _retention(proj, batch, seq, q_col, k_col, v_col, g_col):
    t = proj.shape[0]
    pairs = RET_HEADS // 2
    wv = 2 * RET_V_DIM

    def spec(col0, width):
        cb = col0 // width
        return pl.BlockSpec((seq, width), lambda b, p: (b, cb + p))

    return pl.pallas_call(
        _retention_kernel,
        grid=(batch, pairs),
        in_specs=[spec(q_col, LANES), spec(k_col, LANES), spec(v_col, wv), spec(g_col, wv)],
        out_specs=pl.BlockSpec((seq, wv), lambda b, p: (b, p)),
        out_shape=jax.ShapeDtypeStruct((t, pairs * wv), BF16),
        compiler_params=_params(2),
        name="retention",
    )(proj, proj, proj, proj)


def _mem_kv_kernel(m_ref, g_ref, w_ref, o_ref):
    o_ref[...] = _wdot(_rms(m_ref[...], g_ref[...]).astype(BF16), w_ref[...]).astype(BF16)


def _mem_kv(mem2d, g, w, rows):
    t, d = mem2d.shape
    n = w.shape[1]
    return pl.pallas_call(
        _mem_kv_kernel,
        grid=(t // rows,),
        in_specs=[pl.BlockSpec((rows, d), lambda i: (i, 0)),
                  pl.BlockSpec((1, d), lambda i: (0, 0)),
                  pl.BlockSpec((d, n), lambda i: (0, 0))],
        out_specs=pl.BlockSpec((rows, n), lambda i: (i, 0)),
        out_shape=jax.ShapeDtypeStruct((t, n), BF16),
        compiler_params=_params(1),
        name="mem_kv",
    )(mem2d, g, w)


def _mix_cross_kernel(mo_ref, ro_ref, x_ref, kv_ref, wo_ref, wq_ref, wc_ref,
                      g_mix_ref, g_pre_ref, g_post_ref, o_ref):
    d = x_ref.shape[1]
    half = mo_ref.shape[1]
    dc = d // CROSS_HEADS
    mix = _wdot(mo_ref[...], wo_ref[0:half, :]) + _wdot(ro_ref[...], wo_ref[half:, :])
    x1 = x_ref[...] + _rms(mix, g_mix_ref[...])
    h = _rms(x1, g_pre_ref[...]).astype(BF16)
    cq = (_wdot(h, wq_ref[...]) * (dc ** -0.5)).astype(BF16)
    heads = []
    for hc in range(CROSS_HEADS):
        c0, c1 = hc * dc, (hc + 1) * dc
        s = _dot_nt(cq[:, c0:c1], kv_ref[:, c0:c1])
        e = jnp.exp(s - jnp.max(s, axis=1, keepdims=True))
        o = _dot(e.astype(BF16), kv_ref[:, d + c0:d + c1])
        heads.append((o / jnp.sum(e, axis=1, keepdims=True)).astype(BF16))
    c = _wdot(jnp.concatenate(heads, axis=1), wc_ref[...])
    o_ref[...] = x1 + _rms(c, g_post_ref[...])


def _mix_cross(mo, ro, x2d, kv, w_out, w_cq, w_co, g_mix, g_pre, g_post, seq, n_mem):
    t, d = x2d.shape
    tm = ROW_TILE
    tiles_per_seq = seq // tm
    half = mo.shape[1]
    full = lambda i: (0, 0)
    return pl.pallas_call(
        _mix_cross_kernel,
        grid=(t // tm,),
        in_specs=[
            pl.BlockSpec((tm, half), lambda i: (i, 0)),
            pl.BlockSpec((tm, half), lambda i: (i, 0)),
            pl.BlockSpec((tm, d), lambda i: (i, 0)),
            pl.BlockSpec((n_mem, 2 * d), lambda i: (i // tiles_per_seq, 0)),
            pl.BlockSpec((d, d), full), pl.BlockSpec((d, d), full), pl.BlockSpec((d, d), full),
            pl.BlockSpec((1, d), full), pl.BlockSpec((1, d), full), pl.BlockSpec((1, d), full),
        ],
        out_specs=pl.BlockSpec((tm, d), lambda i: (i, 0)),
        out_shape=jax.ShapeDtypeStruct((t, d), F32),
        compiler_params=_params(1),
        name="mix_cross",
    )(mo, ro, x2d, kv, w_out, w_cq, w_co, g_mix, g_pre, g_post)


def _ffn_kernel(x_ref, wgu_ref, wd_ref, g_pre_ref, g_post_ref, o_ref, *, d_ff, chunk):
    x = x_ref[...]
    h = _rms(x, g_pre_ref[...]).astype(BF16)
    f = jnp.zeros(x.shape, F32)
    for c0 in range(0, d_ff, chunk):
        gate = _wdot(h, wgu_ref[:, c0:c0 + chunk])
        up = _wdot(h, wgu_ref[:, d_ff + c0:d_ff + c0 + chunk])
        act = (gate * jax.nn.sigmoid(gate) * up).astype(BF16)
        f = f + _wdot(act, wd_ref[c0:c0 + chunk, :])
    o_ref[...] = x + _rms(f, g_post_ref[...])


def _ffn(x2d, w_gate_up, w_down, g_pre, g_post):
    t, d = x2d.shape
    d_ff = w_down.shape[0]
    tm = ROW_TILE
    chunk = d_ff // 2
    assert chunk % LANES == 0
    full = lambda i: (0, 0)
    return pl.pallas_call(
        functools.partial(_ffn_kernel, d_ff=d_ff, chunk=chunk),
        grid=(t // tm,),
        in_specs=[
            pl.BlockSpec((tm, d), lambda i: (i, 0)),
            pl.BlockSpec((d, 2 * d_ff), full, pipeline_mode=pl.Buffered(1)),
            pl.BlockSpec((d_ff, d), full, pipeline_mode=pl.Buffered(1)),
            pl.BlockSpec((1, d), full), pl.BlockSpec((1, d), full),
        ],
        out_specs=pl.BlockSpec((tm, d), lambda i: (i, 0)),
        out_shape=jax.ShapeDtypeStruct((t, d), F32),
        compiler_params=_params(1),
        name="ffn",
    )(x2d, w_gate_up, w_down, g_pre, g_post)


def kernel(x, mem, g_pre_mix, w_in, w_out, g_post_mix, g_pre_cross, g_mem, w_cq, w_ckv, w_co,
           g_post_cross, g_pre_ffn, w_gate_up, w_down, g_post_ffn):
    batch, seq, d = x.shape
    n_mem = mem.shape[1]
    depth = w_in.shape[0]
    moba_w = MOBA_HEADS * MOBA_HEAD_DIM
    ret_qk_w = RET_HEADS * RET_QK_DIM
    ret_v_w = RET_HEADS * RET_V_DIM
    c_mq, c_mk, c_mv = 0, moba_w, 2 * moba_w
    c_rq = 3 * moba_w
    c_rk = c_rq + ret_qk_w
    c_rv = c_rk + ret_qk_w
    c_rg = c_rv + ret_v_w
    n_proj = c_rg + ret_v_w
    assert w_in.shape[2] == n_proj
    sections = (("mq", c_mq, c_mk), ("mk", c_mk, c_mv), ("mv", c_mv, c_rq), ("rq", c_rq, c_rk),
                ("rk", c_rk, c_rv), ("rv", c_rv, c_rg), ("rg", c_rg, n_proj))

    moba_inv = jnp.power(ROPE_THETA, -jnp.arange(ROPE_DIM // 2, dtype=F32) * 2.0 / ROPE_DIM)
    ret_inv = 1.0 / jnp.power(RET_THETA, jnp.linspace(0.0, 1.0, RET_QK_DIM // 2, dtype=F32))
    tabs = (
        _rotary_tables(seq, moba_inv, ROPE_DIM, MOBA_HEAD_DIM, MOBA_HEAD_DIM ** -0.5 * math.log2(math.e)),
        _rotary_tables(seq, moba_inv, ROPE_DIM, MOBA_HEAD_DIM, 1.0),
        _rotary_tables(seq, ret_inv, RET_QK_DIM, RET_QK_DIM, 1.0),
        _rotary_tables(seq, ret_inv, RET_QK_DIM, RET_QK_DIM, RET_QK_DIM ** -0.5),
    )

    xf = x.reshape(batch * seq, d)
    mem2d = mem.reshape(batch * n_mem, d)
    row = lambda g: g.reshape(1, d)
    for l in range(depth):
        proj = _in_proj(xf, row(g_pre_mix[l]), w_in[l], tabs, seq, sections)
        mo = _moba(proj, batch, seq, c_mq, c_mk, c_mv)
        ro = _retention(proj, batch, seq, c_rq, c_rk, c_rv, c_rg)
        kv = _mem_kv(mem2d, row(g_mem[l]), w_ckv[l], n_mem)
        x2 = _mix_cross(mo, ro, xf, kv, w_out[l], w_cq[l], w_co[l], row(g_post_mix[l]),
                        row(g_pre_cross[l]), row(g_post_cross[l]), seq, n_mem)
        xf = _ffn(x2, w_gate_up[l], w_down[l], row(g_pre_ffn[l]), row(g_post_ffn[l]))
    return xf.reshape(batch, seq, d)
```

```python
import functools
import math

import jax
import jax.numpy as jnp
from jax import lax
from jax.experimental import pallas as pl
from jax.experimental.pallas import tpu as pltpu

F32 = jnp.float32
BF16 = jnp.bfloat16

NORM_EPS = 1e-6
NEG_INF = -1e30

LANES = 128
MXU_DIM = 256
VMEM_LIMIT_BYTES = 56 * 1024 * 1024

MOBA_HEAD_DIM = 64
MOBA_HEADS = 8
MOBA_BLOCK = 256
MOBA_TOPK = 3
MOBA_PREFETCH = 2
ROPE_THETA = 500000.0
ROPE_DIM = MOBA_HEAD_DIM // 4

RET_HEADS = 4
RET_QK_DIM = 64
RET_V_DIM = 128
RET_THETA = 10000.0
RET_CHUNK = 256

CROSS_HEADS = 4

ROW_TILE = 512


def _dot(a, b):
    return jnp.dot(a, b, preferred_element_type=F32)


def _wdot(a, w):
    return jnp.dot(a, w.astype(BF16), preferred_element_type=F32)


def _dot_nt(a, b):
    return lax.dot_general(a, b, (((1,), (1,)), ((), ())), preferred_element_type=F32)


def _dot_tn(a, b):
    return lax.dot_general(a, b, (((0,), (0,)), ((), ())), preferred_element_type=F32)


def _rms(x, g):
    return x * lax.rsqrt(jnp.mean(x * x, axis=-1, keepdims=True) + NORM_EPS) * g


def _params(n_grid_dims):
    return pltpu.CompilerParams(
        dimension_semantics=("arbitrary",) * n_grid_dims,
        vmem_limit_bytes=VMEM_LIMIT_BYTES,
    )


def _rotary_tables(seq, inv_freq, rot_dim, head_dim, scale):
    half = rot_dim // 2
    ang = jnp.arange(seq, dtype=F32)[:, None] * inv_freq[None, :]
    cos, sin = jnp.cos(ang), jnp.sin(ang)
    pad = head_dim - rot_dim
    a = jnp.concatenate([cos, cos, jnp.ones((seq, pad), F32)], axis=1)
    bm = jnp.concatenate([-sin, jnp.zeros((seq, half + pad), F32)], axis=1)
    bp = jnp.concatenate([jnp.zeros((seq, half), F32), sin, jnp.zeros((seq, pad), F32)], axis=1)
    reps = LANES // head_dim
    tabs = jnp.stack([jnp.tile(t, (1, reps)) for t in (a, bm, bp)], axis=0)
    return tabs * scale


def _rotate(acc, tab_ref, rows, half):
    a, bm, bp = tab_ref[0, rows, :], tab_ref[1, rows, :], tab_ref[2, rows, :]
    outs = []
    for c in range(acc.shape[1] // LANES):
        xs = acc[:, c * LANES:(c + 1) * LANES]
        outs.append(xs * a + pltpu.roll(xs, LANES - half, 1) * bm + pltpu.roll(xs, half, 1) * bp)
    return jnp.concatenate(outs, axis=1)


def _in_proj_kernel(x_ref, g_ref, w_ref, tmq_ref, tmk_ref, trq_ref, trk_ref, o_ref, *, sections):
    rows = x_ref.shape[0] // 2
    sl = [slice(r * rows, (r + 1) * rows) for r in range(2)]
    h = [_rms(x_ref[s, :], g_ref[...]).astype(BF16) for s in sl]
    tabs = {"mq": (tmq_ref, ROPE_DIM // 2), "mk": (tmk_ref, ROPE_DIM // 2),
            "rq": (trq_ref, RET_QK_DIM // 2), "rk": (trk_ref, RET_QK_DIM // 2)}
    for kind, c0, c1 in sections:
        for r in range(2):
            acc = _wdot(h[r], w_ref[:, c0:c1])
            if kind in tabs:
                tab_ref, half = tabs[kind]
                acc = _rotate(acc, tab_ref, sl[r], half)
            o_ref[sl[r], c0:c1] = acc.astype(BF16)


def _in_proj(x2d, g, w, tabs, seq, sections):
    t, d = x2d.shape
    n = w.shape[1]
    tm = ROW_TILE
    tiles_per_seq = seq // tm
    tab_spec = pl.BlockSpec((3, tm, LANES), lambda i: (0, i % tiles_per_seq, 0))
    return pl.pallas_call(
        functools.partial(_in_proj_kernel, sections=sections),
        grid=(t // tm,),
        in_specs=[
            pl.BlockSpec((tm, d), lambda i: (i, 0)),
            pl.BlockSpec((1, d), lambda i: (0, 0)),
            pl.BlockSpec((d, n), lambda i: (0, 0)),
            tab_spec, tab_spec, tab_spec, tab_spec,
        ],
        out_specs=pl.BlockSpec((tm, n), lambda i: (i, 0)),
        out_shape=jax.ShapeDtypeStruct((t, n), BF16),
        compiler_params=_params(1),
        name="in_proj",
    )(x2d, g, w, *tabs)


def _moba_kernel(q_ref, k_ref, v_ref, o_ref, qp_ref, vt_ref, masked_ref):
    seq = q_ref.shape[0]
    blk = MOBA_BLOCK
    nb = seq // blk
    hd = MOBA_HEAD_DIM
    q2 = q_ref[...]
    k2 = k_ref[...]
    v2 = v_ref[...]
    lane = lax.broadcasted_iota(jnp.int32, (seq, LANES), 1)
    kmean2 = jnp.mean(k2.astype(F32).reshape(nb, blk, LANES), axis=1)

    lane8 = lax.broadcasted_iota(jnp.int32, (nb, LANES), 1)
    km = jnp.concatenate([jnp.where(lane8 < hd, kmean2, 0.0), jnp.where(lane8 >= hd, kmean2, 0.0)], axis=0)
    km_hi = km.astype(BF16)
    km_lo = (km - km_hi.astype(F32)).astype(BF16)
    gate2 = _dot_nt(jnp.concatenate([km_hi, km_lo], axis=0), q2)
    jidx = lax.broadcasted_iota(jnp.int32, (nb, seq), 0)
    qblk = lax.broadcasted_iota(jnp.int32, (nb, seq), 1) // blk
    for hh in range(2):
        gate = gate2[hh * nb:(hh + 1) * nb] + gate2[(2 + hh) * nb:(3 + hh) * nb]
        rank = jnp.zeros((nb, seq), jnp.int32)
        for i in range(nb):
            gi = gate[i:i + 1, :]
            beats = ((gi > gate) | ((gi == gate) & (i < jidx))) & (i < qblk)
            rank = rank + beats.astype(jnp.int32)
        masked_ref[hh] = ((jidx >= qblk) | (rank >= MOBA_TOPK)).astype(F32)
        mine = (lane >= hh * hd) & (lane < (hh + 1) * hd)
        qp_ref[hh] = jnp.where(mine, q2, jnp.zeros_like(q2))
        vt_ref[hh] = jnp.where(mine, v2, jnp.ones_like(v2)).astype(F32).T.astype(BF16)

    kr = lax.broadcasted_iota(jnp.int32, (blk, blk), 0)
    qcol = lax.broadcasted_iota(jnp.int32, (blk, blk), 1)

    def scores(t):
        c, j = tiles[t]
        out = []
        for hh in range(2):
            s = _dot_nt(k_ref[j * blk:(j + 1) * blk, :], qp_ref[hh, c * blk:(c + 1) * blk, :])
            out.append(jnp.where(kr <= qcol, s, NEG_INF) if j == c else s)
        return out

    tiles = [(c, j) for c in range(nb) for j in range(c + 1)]
    pending = {t: scores(t) for t in range(min(MOBA_PREFETCH, len(tiles)))}
    m_run = [None, None]
    acc = [None, None]
    for t, (c, j) in enumerate(tiles):
        r0, r1 = c * blk, (c + 1) * blk
        if t + MOBA_PREFETCH < len(tiles):
            pending[t + MOBA_PREFETCH] = scores(t + MOBA_PREFETCH)
        s_cur = pending.pop(t)
        for hh in range(2):
            m_t = jnp.max(s_cur[hh], axis=0, keepdims=True)
            if j < c:
                off = masked_ref[hh, j:j + 1, r0:r1] > 0.5
                m_t = jnp.where(off, NEG_INF, m_t)
            m_new = m_t if j == 0 else jnp.maximum(m_run[hh], m_t)
            shift = jnp.where(off, -NEG_INF, m_new) if j < c else m_new
            pv = _dot(vt_ref[hh, :, j * blk:(j + 1) * blk], jnp.exp2(s_cur[hh] - shift).astype(BF16))
            acc[hh] = pv if j == 0 else acc[hh] * jnp.exp2(m_run[hh] - m_new) + pv
            m_run[hh] = m_new
        if j == c:
            res = [acc[hh][hh * hd:(hh + 1) * hd, :] / acc[hh][hd - hh * hd:2 * hd - hh * hd, :]
                   for hh in range(2)]
            o_ref[r0:r1, :] = jnp.concatenate(res, axis=0).T.astype(BF16)


def _moba(proj, batch, seq, q_col, k_col, v_col):
    t = proj.shape[0]
    pairs = MOBA_HEADS // 2

    def spec(col0):
        cb = col0 // LANES
        return pl.BlockSpec((seq, LANES), lambda b, p: (b, cb + p))

    return pl.pallas_call(
        _moba_kernel,
        grid=(batch, pairs),
        in_specs=[spec(q_col), spec(k_col), spec(v_col)],
        out_specs=pl.BlockSpec((seq, LANES), lambda b, p: (b, p)),
        out_shape=jax.ShapeDtypeStruct((t, pairs * LANES), BF16),
        scratch_shapes=[pltpu.VMEM((2, seq, LANES), BF16), pltpu.VMEM((2, LANES, seq), BF16),
                        pltpu.VMEM((2, seq // MOBA_BLOCK, seq), F32)],
        compiler_params=_params(2),
        name="moba",
    )(proj, proj, proj)


_RET_LOG_G = [math.log(1.0 - 2.0 ** (-5.0 - h)) for h in range(RET_HEADS)]


def _retention_kernel(q_ref, k_ref, v_ref, g_ref, o_ref):
    seq = q_ref.shape[0]
    ch = RET_CHUNK
    dv = RET_V_DIM
    pair = pl.program_id(1)
    lane = lax.broadcasted_iota(jnp.int32, (ch, LANES), 1)
    rowf = lax.broadcasted_iota(jnp.int32, (ch, LANES), 0).astype(F32)
    ri = lax.broadcasted_iota(jnp.int32, (ch, ch), 0)
    ci = lax.broadcasted_iota(jnp.int32, (ch, ch), 1)
    diff = (ri - ci).astype(F32)
    for hh in range(2):
        log_g = jnp.where(pair == 0, _RET_LOG_G[hh], _RET_LOG_G[2 + hh]).astype(F32)
        inner_decay = jnp.where(diff >= 0, jnp.exp(log_g * jnp.maximum(diff, 0.0)), 0.0)
        q_decay = jnp.exp(log_g * (rowf + 1.0))
        k_decay = jnp.exp(log_g * (ch - 1.0 - rowf))
        chunk_decay = jnp.exp(jnp.full((LANES, dv), ch, F32) * log_g)
        mine = (lane >= hh * RET_QK_DIM) & (lane < (hh + 1) * RET_QK_DIM)
        state = jnp.zeros((LANES, dv), F32)
        for n in range(seq // ch):
            r0, r1 = n * ch, (n + 1) * ch
            qc = jnp.where(mine, q_ref[r0:r1, :], jnp.zeros((), BF16))
            kc = k_ref[r0:r1, :]
            vc = v_ref[r0:r1, hh * dv:(hh + 1) * dv]
            attn = _dot_nt(qc, kc) * inner_decay
            out = _dot(attn.astype(BF16), vc)
            out = out + _dot((qc.astype(F32) * q_decay).astype(BF16), state.astype(BF16))
            state = state * chunk_decay + _dot_tn((kc.astype(F32) * k_decay).astype(BF16), vc)
            out = out * lax.rsqrt(jnp.mean(out * out, axis=-1, keepdims=True) + NORM_EPS)
            gate = g_ref[r0:r1, hh * dv:(hh + 1) * dv].astype(F32)
            o_ref[r0:r1, hh * dv:(hh + 1) * dv] = (gate * jax.nn.sigmoid(gate) * out).astype(BF16)


def _retention(proj, batch, seq, q_col, k_col, v_col, g_col):
    t = proj.shape[0]
    pairs = RET_HEADS // 2
    wv = 2 * RET_V_DIM

    def spec(col0, width):
        cb = col0 // width
        return pl.BlockSpec((seq, width), lambda b, p: (b, cb + p))

    return pl.pallas_call(
        _retention_kernel,
        grid=(batch, pairs),
        in_specs=[spec(q_col, LANES), spec(k_col, LANES), spec(v_col, wv), spec(g_col, wv)],
        out_specs=pl.BlockSpec((seq, wv), lambda b, p: (b, p)),
        out_shape=jax.ShapeDtypeStruct((t, pairs * wv), BF16),
        compiler_params=_params(2),
        name="retention",
    )(proj, proj, proj, proj)


def _mem_kv_kernel(m_ref, g_ref, w_ref, o_ref):
    o_ref[...] = _wdot(_rms(m_ref[...], g_ref[...]).astype(BF16), w_ref[...]).astype(BF16)


def _mem_kv(mem2d, g, w, rows):
    t, d = mem2d.shape
    n = w.shape[1]
    return pl.pallas_call(
        _mem_kv_kernel,
        grid=(t // rows,),
        in_specs=[pl.BlockSpec((rows, d), lambda i: (i, 0)),
                  pl.BlockSpec((1, d), lambda i: (0, 0)),
                  pl.BlockSpec((d, n), lambda i: (0, 0))],
        out_specs=pl.BlockSpec((rows, n), lambda i: (i, 0)),
        out_shape=jax.ShapeDtypeStruct((t, n), BF16),
        compiler_params=_params(1),
        name="mem_kv",
    )(mem2d, g, w)


def _mix_cross_kernel(mo_ref, ro_ref, x_ref, kv_ref, wo_ref, wq_ref, wc_ref,
                      g_mix_ref, g_pre_ref, g_post_ref, o_ref):
    d = x_ref.shape[1]
    half = mo_ref.shape[1]
    dc = d // CROSS_HEADS
    rows = x_ref.shape[0] // 2
    sl = [slice(r * rows, (r + 1) * rows) for r in range(2)]
    mix = [_wdot(mo_ref[s, :], wo_ref[0:half, :]) + _wdot(ro_ref[s, :], wo_ref[half:, :]) for s in sl]
    x1 = [x_ref[s, :] + _rms(m, g_mix_ref[...]) for s, m in zip(sl, mix)]
    h = [_rms(x, g_pre_ref[...]).astype(BF16) for x in x1]
    cq = [(_wdot(hh, wq_ref[...]) * (dc ** -0.5)).astype(BF16) for hh in h]
    att = []
    for r in range(2):
        heads = []
        for hc in range(CROSS_HEADS):
            c0, c1 = hc * dc, (hc + 1) * dc
            s = _dot_nt(cq[r][:, c0:c1], kv_ref[:, c0:c1])
            e = jnp.exp(s - jnp.max(s, axis=1, keepdims=True))
            o = _dot(e.astype(BF16), kv_ref[:, d + c0:d + c1])
            heads.append((o / jnp.sum(e, axis=1, keepdims=True)).astype(BF16))
        att.append(jnp.concatenate(heads, axis=1))
    c = [_wdot(a, wc_ref[...]) for a in att]
    for r in range(2):
        o_ref[sl[r], :] = x1[r] + _rms(c[r], g_post_ref[...])


def _mix_cross(mo, ro, x2d, kv, w_out, w_cq, w_co, g_mix, g_pre, g_post, seq, n_mem):
    t, d = x2d.shape
    tm = ROW_TILE
    tiles_per_seq = seq // tm
    half = mo.shape[1]
    full = lambda i: (0, 0)
    return pl.pallas_call(
        _mix_cross_kernel,
        grid=(t // tm,),
        in_specs=[
            pl.BlockSpec((tm, half), lambda i: (i, 0)),
            pl.BlockSpec((tm, half), lambda i: (i, 0)),
            pl.BlockSpec((tm, d), lambda i: (i, 0)),
            pl.BlockSpec((n_mem, 2 * d), lambda i: (i // tiles_per_seq, 0)),
            pl.BlockSpec((d, d), full), pl.BlockSpec((d, d), full), pl.BlockSpec((d, d), full),
            pl.BlockSpec((1, d), full), pl.BlockSpec((1, d), full), pl.BlockSpec((1, d), full),
        ],
        out_specs=pl.BlockSpec((tm, d), lambda i: (i, 0)),
        out_shape=jax.ShapeDtypeStruct((t, d), F32),
        compiler_params=_params(1),
        name="mix_cross",
    )(mo, ro, x2d, kv, w_out, w_cq, w_co, g_mix, g_pre, g_post)


def _ffn_kernel(x_ref, wgu_ref, wd_ref, g_pre_ref, g_post_ref, o_ref, *, d_ff, bounds):
    rows = x_ref.shape[0] // 2
    xs = [x_ref[r * rows:(r + 1) * rows, :] for r in range(2)]
    hs = [_rms(x, g_pre_ref[...]).astype(BF16) for x in xs]
    fs = [None, None]
    for c0, c1 in bounds:
        for r in range(2):
            gate = _wdot(hs[r], wgu_ref[:, c0:c1])
            up = _wdot(hs[r], wgu_ref[:, d_ff + c0:d_ff + c1])
            act = (gate * jax.nn.sigmoid(gate) * up).astype(BF16)
            down = _wdot(act, wd_ref[c0:c1, :])
            fs[r] = down if fs[r] is None else fs[r] + down
    for r in range(2):
        o_ref[r * rows:(r + 1) * rows, :] = xs[r] + _rms(fs[r], g_post_ref[...])


def _ffn(x2d, w_gate_up, w_down, g_pre, g_post):
    t, d = x2d.shape
    d_ff = w_down.shape[0]
    tm = ROW_TILE
    assert d_ff % MXU_DIM == 0
    mid = (d_ff // MXU_DIM // 2) * MXU_DIM
    bounds = ((0, mid), (mid, d_ff))
    full = lambda i: (0, 0)
    return pl.pallas_call(
        functools.partial(_ffn_kernel, d_ff=d_ff, bounds=bounds),
        grid=(t // tm,),
        in_specs=[
            pl.BlockSpec((tm, d), lambda i: (i, 0)),
            pl.BlockSpec((d, 2 * d_ff), full, pipeline_mode=pl.Buffered(1)),
            pl.BlockSpec((d_ff, d), full, pipeline_mode=pl.Buffered(1)),
            pl.BlockSpec((1, d), full), pl.BlockSpec((1, d), full),
        ],
        out_specs=pl.BlockSpec((tm, d), lambda i: (i, 0)),
        out_shape=jax.ShapeDtypeStruct((t, d), F32),
        compiler_params=_params(1),
        name="ffn",
    )(x2d, w_gate_up, w_down, g_pre, g_post)


def kernel(x, mem, g_pre_mix, w_in, w_out, g_post_mix, g_pre_cross, g_mem, w_cq, w_ckv, w_co,
           g_post_cross, g_pre_ffn, w_gate_up, w_down, g_post_ffn):
    batch, seq, d = x.shape
    n_mem = mem.shape[1]
    depth = w_in.shape[0]
    moba_w = MOBA_HEADS * MOBA_HEAD_DIM
    ret_qk_w = RET_HEADS * RET_QK_DIM
    ret_v_w = RET_HEADS * RET_V_DIM
    c_mq, c_mk, c_mv = 0, moba_w, 2 * moba_w
    c_rq = 3 * moba_w
    c_rk = c_rq + ret_qk_w
    c_rv = c_rk + ret_qk_w
    c_rg = c_rv + ret_v_w
    n_proj = c_rg + ret_v_w
    assert w_in.shape[2] == n_proj
    sections = (("mq", c_mq, c_mk), ("mk", c_mk, c_mv), ("mv", c_mv, c_rq), ("rq", c_rq, c_rk),
                ("rk", c_rk, c_rv), ("rv", c_rv, c_rg), ("rg", c_rg, n_proj))

    moba_inv = jnp.power(ROPE_THETA, -jnp.arange(ROPE_DIM // 2, dtype=F32) * 2.0 / ROPE_DIM)
    ret_inv = 1.0 / jnp.power(RET_THETA, jnp.linspace(0.0, 1.0, RET_QK_DIM // 2, dtype=F32))
    tabs = (
        _rotary_tables(seq, moba_inv, ROPE_DIM, MOBA_HEAD_DIM, MOBA_HEAD_DIM ** -0.5 * math.log2(math.e)),
        _rotary_tables(seq, moba_inv, ROPE_DIM, MOBA_HEAD_DIM, 1.0),
        _rotary_tables(seq, ret_inv, RET_QK_DIM, RET_QK_DIM, 1.0),
        _rotary_tables(seq, ret_inv, RET_QK_DIM, RET_QK_DIM, RET_QK_DIM ** -0.5),
    )

    xf = x.reshape(batch * seq, d)
    mem2d = mem.reshape(batch * n_mem, d)
    row = lambda g: g.reshape(1, d)
    for l in range(depth):
        proj = _in_proj(xf, row(g_pre_mix[l]), w_in[l], tabs, seq, sections)
        mo = _moba(proj, batch, seq, c_mq, c_mk, c_mv)
        ro = _retention(proj, batch, seq, c_rq, c_rk, c_rv, c_rg)
        kv = _mem_kv(mem2d, row(g_mem[l]), w_ckv[l], n_mem)
        x2 = _mix_cross(mo, ro, xf, kv, w_out[l], w_cq[l], w_co[l], row(g_post_mix[l]),
                        row(g_pre_cross[l]), row(g_post_cross[l]), seq, n_mem)
        xf = _ffn(x2, w_gate_up[l], w_down[l], row(g_pre_ffn[l]), row(g_post_ffn[l]))
    return xf.reshape(batch, seq, d)
```

```python
import functools
import math

import jax
import jax.numpy as jnp
import numpy as np
from jax import lax
from jax.experimental import pallas as pl
from jax.experimental.pallas import tpu as pltpu

F32 = jnp.float32
BF16 = jnp.bfloat16

NORM_EPS = 1e-6
NEG_INF = -1e30

LANES = 128
MXU_DIM = 256
VMEM_LIMIT_BYTES = 56 * 1024 * 1024

MOBA_HEAD_DIM = 64
MOBA_HEADS = 8
MOBA_BLOCK = 256
MOBA_TOPK = 3
MOBA_PREFETCH = 2
ROPE_THETA = 500000.0
ROPE_DIM = MOBA_HEAD_DIM // 4

RET_HEADS = 4
RET_QK_DIM = 64
RET_V_DIM = 128
RET_THETA = 10000.0
RET_CHUNK = 256

CROSS_HEADS = 4

IN_PROJ_ROWS = 1024
MEM_KV_ROWS = 1024
MIX_ROWS = 1024
FFN_ROWS = 512


def _dot(a, b):
    return jnp.dot(a, b, preferred_element_type=F32)


def _wdot(a, w):
    return jnp.dot(a, w.astype(BF16), preferred_element_type=F32)


def _dot_nt(a, b):
    return lax.dot_general(a, b, (((1,), (1,)), ((), ())), preferred_element_type=F32)


def _dot_tn(a, b):
    return lax.dot_general(a, b, (((0,), (0,)), ((), ())), preferred_element_type=F32)


def _rms(x, g):
    return x * lax.rsqrt(jnp.mean(x * x, axis=-1, keepdims=True) + NORM_EPS) * g


def _params(n_grid_dims):
    return pltpu.CompilerParams(
        dimension_semantics=("arbitrary",) * n_grid_dims,
        vmem_limit_bytes=VMEM_LIMIT_BYTES,
    )


def _rotary_tables(seq, inv_freq, rot_dim, head_dim, scale):
    half = rot_dim // 2
    ang = np.arange(seq, dtype=np.float64)[:, None] * inv_freq[None, :]
    cos, sin = np.cos(ang), np.sin(ang)
    pad = head_dim - rot_dim
    a = np.concatenate([cos, cos, np.ones((seq, pad))], axis=1)
    bm = np.concatenate([-sin, np.zeros((seq, half + pad))], axis=1)
    bp = np.concatenate([np.zeros((seq, half)), sin, np.zeros((seq, pad))], axis=1)
    reps = LANES // head_dim
    tabs = np.stack([np.tile(t, (1, reps)) for t in (a, bm, bp)], axis=0)
    return jnp.asarray((tabs * scale).astype(np.float32))


def _rotate(acc, tab_ref, rows, half):
    a, bm, bp = tab_ref[0, rows, :], tab_ref[1, rows, :], tab_ref[2, rows, :]
    outs = []
    for c in range(acc.shape[1] // LANES):
        xs = acc[:, c * LANES:(c + 1) * LANES]
        outs.append(xs * a + pltpu.roll(xs, LANES - half, 1) * bm + pltpu.roll(xs, half, 1) * bp)
    return jnp.concatenate(outs, axis=1)


def _in_proj_kernel(x_ref, g_ref, w_ref, tmq_ref, tmk_ref, trq_ref, trk_ref, o_ref, *, sections):
    rows = x_ref.shape[0] // 2
    sl = [slice(r * rows, (r + 1) * rows) for r in range(2)]
    h = [_rms(x_ref[s, :], g_ref[...]).astype(BF16) for s in sl]
    tabs = {"mq": (tmq_ref, ROPE_DIM // 2), "mk": (tmk_ref, ROPE_DIM // 2),
            "rq": (trq_ref, RET_QK_DIM // 2), "rk": (trk_ref, RET_QK_DIM // 2)}
    for kind, c0, c1 in sections:
        for r in range(2):
            acc = _wdot(h[r], w_ref[:, c0:c1])
            if kind in tabs:
                tab_ref, half = tabs[kind]
                acc = _rotate(acc, tab_ref, sl[r], half)
            o_ref[sl[r], c0:c1] = acc.astype(BF16)


def _in_proj(x2d, g, w, tabs, seq, sections):
    t, d = x2d.shape
    n = w.shape[1]
    tm = IN_PROJ_ROWS
    tiles_per_seq = seq // tm
    tab_spec = pl.BlockSpec((3, tm, LANES), lambda i: (0, i % tiles_per_seq, 0))
    return pl.pallas_call(
        functools.partial(_in_proj_kernel, sections=sections),
        grid=(t // tm,),
        in_specs=[
            pl.BlockSpec((tm, d), lambda i: (i, 0)),
            pl.BlockSpec((1, d), lambda i: (0, 0)),
            pl.BlockSpec((d, n), lambda i: (0, 0), pipeline_mode=pl.Buffered(1)),
            tab_spec, tab_spec, tab_spec, tab_spec,
        ],
        out_specs=pl.BlockSpec((tm, n), lambda i: (i, 0)),
        out_shape=jax.ShapeDtypeStruct((t, n), BF16),
        compiler_params=_params(1),
        name="in_proj",
    )(x2d, g, w, *tabs)


def _moba_kernel(q_ref, k_ref, v_ref, o_ref, qp_ref, vt_ref, masked_ref):
    seq = q_ref.shape[0]
    blk = MOBA_BLOCK
    nb = seq // blk
    hd = MOBA_HEAD_DIM
    q2 = q_ref[...]
    k2 = k_ref[...]
    v2 = v_ref[...]
    lane = lax.broadcasted_iota(jnp.int32, (seq, LANES), 1)
    kmean2 = jnp.mean(k2.astype(F32).reshape(nb, blk, LANES), axis=1)

    lane8 = lax.broadcasted_iota(jnp.int32, (nb, LANES), 1)
    km = jnp.concatenate([jnp.where(lane8 < hd, kmean2, 0.0), jnp.where(lane8 >= hd, kmean2, 0.0)], axis=0)
    km_hi = km.astype(BF16)
    km_lo = (km - km_hi.astype(F32)).astype(BF16)
    gate2 = _dot_nt(jnp.concatenate([km_hi, km_lo], axis=0), q2)
    jidx = lax.broadcasted_iota(jnp.int32, (nb, seq), 0)
    qblk = lax.broadcasted_iota(jnp.int32, (nb, seq), 1) // blk
    for hh in range(2):
        gate = gate2[hh * nb:(hh + 1) * nb] + gate2[(2 + hh) * nb:(3 + hh) * nb]
        rank = jnp.zeros((nb, seq), jnp.int32)
        for i in range(nb):
            gi = gate[i:i + 1, :]
            beats = ((gi > gate) | ((gi == gate) & (i < jidx))) & (i < qblk)
            rank = rank + beats.astype(jnp.int32)
        masked_ref[hh] = ((jidx >= qblk) | (rank >= MOBA_TOPK)).astype(F32)
        mine = (lane >= hh * hd) & (lane < (hh + 1) * hd)
        qp_ref[hh] = jnp.where(mine, q2, jnp.zeros_like(q2))
        vt_ref[hh] = jnp.where(mine, v2, jnp.ones_like(v2)).astype(F32).T.astype(BF16)

    kr = lax.broadcasted_iota(jnp.int32, (blk, blk), 0)
    qcol = lax.broadcasted_iota(jnp.int32, (blk, blk), 1)

    def scores(t):
        c, j = tiles[t]
        out = []
        for hh in range(2):
            s = _dot_nt(k_ref[j * blk:(j + 1) * blk, :], qp_ref[hh, c * blk:(c + 1) * blk, :])
            out.append(jnp.where(kr <= qcol, s, NEG_INF) if j == c else s)
        return out

    tiles = [(c, j) for c in range(nb) for j in range(c + 1)]
    pending = {t: scores(t) for t in range(min(MOBA_PREFETCH, len(tiles)))}
    m_run = [None, None]
    acc = [None, None]
    for t, (c, j) in enumerate(tiles):
        r0, r1 = c * blk, (c + 1) * blk
        if t + MOBA_PREFETCH < len(tiles):
            pending[t + MOBA_PREFETCH] = scores(t + MOBA_PREFETCH)
        s_cur = pending.pop(t)
        for hh in range(2):
            m_t = jnp.max(s_cur[hh], axis=0, keepdims=True)
            if j < c:
                off = masked_ref[hh, j:j + 1, r0:r1] > 0.5
                m_t = jnp.where(off, NEG_INF, m_t)
            m_new = m_t if j == 0 else jnp.maximum(m_run[hh], m_t)
            shift = jnp.where(off, -NEG_INF, m_new) if j < c else m_new
            pv = _dot(vt_ref[hh, :, j * blk:(j + 1) * blk], jnp.exp2(s_cur[hh] - shift).astype(BF16))
            acc[hh] = pv if j == 0 else acc[hh] * jnp.exp2(m_run[hh] - m_new) + pv
            m_run[hh] = m_new
        if j == c:
            res = [acc[hh][hh * hd:(hh + 1) * hd, :] / acc[hh][hd - hh * hd:2 * hd - hh * hd, :]
                   for hh in range(2)]
            o_ref[r0:r1, :] = jnp.concatenate(res, axis=0).T.astype(BF16)


def _moba(proj, batch, seq, q_col, k_col, v_col):
    t = proj.shape[0]
    pairs = MOBA_HEADS // 2

    def spec(col0):
        cb = col0 // LANES
        return pl.BlockSpec((seq, LANES), lambda b, p: (b, cb + p))

    return pl.pallas_call(
        _moba_kernel,
        grid=(batch, pairs),
        in_specs=[spec(q_col), spec(k_col), spec(v_col)],
        out_specs=pl.BlockSpec((seq, LANES), lambda b, p: (b, p)),
        out_shape=jax.ShapeDtypeStruct((t, pairs * LANES), BF16),
        scratch_shapes=[pltpu.VMEM((2, seq, LANES), BF16), pltpu.VMEM((2, LANES, seq), BF16),
                        pltpu.VMEM((2, seq // MOBA_BLOCK, seq), F32)],
        compiler_params=_params(2),
        name="moba",
    )(proj, proj, proj)


_RET_LOG_G = [math.log(1.0 - 2.0 ** (-5.0 - h)) for h in range(RET_HEADS)]


def _retention_kernel(q_ref, k_ref, v_ref, g_ref, o_ref):
    seq = q_ref.shape[0]
    ch = RET_CHUNK
    dv = RET_V_DIM
    pair = pl.program_id(1)
    lane = lax.broadcasted_iota(jnp.int32, (ch, LANES), 1)
    rowf = lax.broadcasted_iota(jnp.int32, (ch, LANES), 0).astype(F32)
    ri = lax.broadcasted_iota(jnp.int32, (ch, ch), 0)
    ci = lax.broadcasted_iota(jnp.int32, (ch, ch), 1)
    diff = (ri - ci).astype(F32)
    for hh in range(2):
        log_g = jnp.where(pair == 0, _RET_LOG_G[hh], _RET_LOG_G[2 + hh]).astype(F32)
        inner_decay = jnp.where(diff >= 0, jnp.exp(log_g * jnp.maximum(diff, 0.0)), 0.0)
        q_decay = jnp.exp(log_g * (rowf + 1.0))
        k_decay = jnp.exp(log_g * (ch - 1.0 - rowf))
        chunk_decay = jnp.exp(jnp.full((LANES, dv), ch, F32) * log_g)
        mine = (lane >= hh * RET_QK_DIM) & (lane < (hh + 1) * RET_QK_DIM)
        state = jnp.zeros((LANES, dv), F32)
        for n in range(seq // ch):
            r0, r1 = n * ch, (n + 1) * ch
            qc = jnp.where(mine, q_ref[r0:r1, :], jnp.zeros((), BF16))
            kc = k_ref[r0:r1, :]
            vc = v_ref[r0:r1, hh * dv:(hh + 1) * dv]
            attn = _dot_nt(qc, kc) * inner_decay
            out = _dot(attn.astype(BF16), vc)
            out = out + _dot((qc.astype(F32) * q_decay).astype(BF16), state.astype(BF16))
            state = state * chunk_decay + _dot_tn((kc.astype(F32) * k_decay).astype(BF16), vc)
            out = out * lax.rsqrt(jnp.mean(out * out, axis=-1, keepdims=True) + NORM_EPS)
            gate = g_ref[r0:r1, hh * dv:(hh + 1) * dv].astype(F32)
            o_ref[r0:r1, hh * dv:(hh + 1) * dv] = (gate * jax.nn.sigmoid(gate) * out).astype(BF16)


def _retention(proj, batch, seq, q_col, k_col, v_col, g_col):
    t = proj.shape[0]
    pairs = RET_HEADS // 2
    wv = 2 * RET_V_DIM

    def spec(col0, width):
        cb = col0 // width
        return pl.BlockSpec((seq, width), lambda b, p: (b, cb + p))

    return pl.pallas_call(
        _retention_kernel,
        grid=(batch, pairs),
        in_specs=[spec(q_col, LANES), spec(k_col, LANES), spec(v_col, wv), spec(g_col, wv)],
        out_specs=pl.BlockSpec((seq, wv), lambda b, p: (b, p)),
        out_shape=jax.ShapeDtypeStruct((t, pairs * wv), BF16),
        compiler_params=_params(2),
        name="retention",
    )(proj, proj, proj, proj)


def _mem_kv_kernel(m_ref, g_ref, w_ref, o_ref):
    o_ref[...] = _wdot(_rms(m_ref[...], g_ref[...]).astype(BF16), w_ref[...]).astype(BF16)


def _mem_kv(mem2d, g, w, rows):
    t, d = mem2d.shape
    n = w.shape[1]
    return pl.pallas_call(
        _mem_kv_kernel,
        grid=(t // rows,),
        in_specs=[pl.BlockSpec((rows, d), lambda i: (i, 0)),
                  pl.BlockSpec((1, d), lambda i: (0, 0)),
                  pl.BlockSpec((d, n), lambda i: (0, 0), pipeline_mode=pl.Buffered(1))],
        out_specs=pl.BlockSpec((rows, n), lambda i: (i, 0)),
        out_shape=jax.ShapeDtypeStruct((t, n), BF16),
        compiler_params=_params(1),
        name="mem_kv",
    )(mem2d, g, w)


def _mix_cross_kernel(mo_ref, ro_ref, x_ref, kv_ref, wo_ref, wq_ref, wc_ref,
                      g_mix_ref, g_pre_ref, g_post_ref, o_ref):
    d = x_ref.shape[1]
    half = mo_ref.shape[1]
    dc = d // CROSS_HEADS
    rows = x_ref.shape[0] // 2
    sl = [slice(r * rows, (r + 1) * rows) for r in range(2)]
    mix = [_wdot(mo_ref[s, :], wo_ref[0:half, :]) + _wdot(ro_ref[s, :], wo_ref[half:, :]) for s in sl]
    x1 = [x_ref[s, :] + _rms(m, g_mix_ref[...]) for s, m in zip(sl, mix)]
    h = [_rms(x, g_pre_ref[...]).astype(BF16) for x in x1]
    cq = [(_wdot(hh, wq_ref[...]) * (dc ** -0.5)).astype(BF16) for hh in h]
    att = []
    for r in range(2):
        heads = []
        for hc in range(CROSS_HEADS):
            c0, c1 = hc * dc, (hc + 1) * dc
            s = _dot_nt(cq[r][:, c0:c1], kv_ref[:, c0:c1])
            e = jnp.exp(s - jnp.max(s, axis=1, keepdims=True))
            o = _dot(e.astype(BF16), kv_ref[:, d + c0:d + c1])
            heads.append((o / jnp.sum(e, axis=1, keepdims=True)).astype(BF16))
        att.append(jnp.concatenate(heads, axis=1))
    c = [_wdot(a, wc_ref[...]) for a in att]
    for r in range(2):
        o_ref[sl[r], :] = x1[r] + _rms(c[r], g_post_ref[...])


def _mix_cross(mo, ro, x2d, kv, w_out, w_cq, w_co, g_mix, g_pre, g_post, seq, n_mem):
    t, d = x2d.shape
    tm = MIX_ROWS
    tiles_per_seq = seq // tm
    half = mo.shape[1]
    full = lambda i: (0, 0)
    weight = pl.BlockSpec((d, d), full, pipeline_mode=pl.Buffered(1))
    return pl.pallas_call(
        _mix_cross_kernel,
        grid=(t // tm,),
        in_specs=[
            pl.BlockSpec((tm, half), lambda i: (i, 0)),
            pl.BlockSpec((tm, half), lambda i: (i, 0)),
            pl.BlockSpec((tm, d), lambda i: (i, 0)),
            pl.BlockSpec((n_mem, 2 * d), lambda i: (i // tiles_per_seq, 0)),
            weight, weight, weight,
            pl.BlockSpec((1, d), full), pl.BlockSpec((1, d), full), pl.BlockSpec((1, d), full),
        ],
        out_specs=pl.BlockSpec((tm, d), lambda i: (i, 0)),
        out_shape=jax.ShapeDtypeStruct((t, d), F32),
        compiler_params=_params(1),
        name="mix_cross",
    )(mo, ro, x2d, kv, w_out, w_cq, w_co, g_mix, g_pre, g_post)


def _ffn_kernel(x_ref, wgu_ref, wd_ref, g_pre_ref, g_post_ref, o_ref, *, d_ff, bounds):
    rows = x_ref.shape[0] // 2
    xs = [x_ref[r * rows:(r + 1) * rows, :] for r in range(2)]
    hs = [_rms(x, g_pre_ref[...]).astype(BF16) for x in xs]
    fs = [None, None]
    for c0, c1 in bounds:
        for r in range(2):
            gate = _wdot(hs[r], wgu_ref[:, c0:c1])
            up = _wdot(hs[r], wgu_ref[:, d_ff + c0:d_ff + c1])
            act = (gate * jax.nn.sigmoid(gate) * up).astype(BF16)
            down = _wdot(act, wd_ref[c0:c1, :])
            fs[r] = down if fs[r] is None else fs[r] + down
    for r in range(2):
        o_ref[r * rows:(r + 1) * rows, :] = xs[r] + _rms(fs[r], g_post_ref[...])


def _ffn(x2d, w_gate_up, w_down, g_pre, g_post):
    t, d = x2d.shape
    d_ff = w_down.shape[0]
    tm = FFN_ROWS
    assert d_ff % MXU_DIM == 0
    mid = (d_ff // MXU_DIM // 2) * MXU_DIM
    bounds = ((0, mid), (mid, d_ff))
    full = lambda i: (0, 0)
    return pl.pallas_call(
        functools.partial(_ffn_kernel, d_ff=d_ff, bounds=bounds),
        grid=(t // tm,),
        in_specs=[
            pl.BlockSpec((tm, d), lambda i: (i, 0)),
            pl.BlockSpec((d, 2 * d_ff), full, pipeline_mode=pl.Buffered(1)),
            pl.BlockSpec((d_ff, d), full, pipeline_mode=pl.Buffered(1)),
            pl.BlockSpec((1, d), full), pl.BlockSpec((1, d), full),
        ],
        out_specs=pl.BlockSpec((tm, d), lambda i: (i, 0)),
        out_shape=jax.ShapeDtypeStruct((t, d), F32),
        compiler_params=_params(1),
        name="ffn",
    )(x2d, w_gate_up, w_down, g_pre, g_post)


def kernel(x, mem, g_pre_mix, w_in, w_out, g_post_mix, g_pre_cross, g_mem, w_cq, w_ckv, w_co,
           g_post_cross, g_pre_ffn, w_gate_up, w_down, g_post_ffn):
    batch, seq, d = x.shape
    n_mem = mem.shape[1]
    depth = w_in.shape[0]
    moba_w = MOBA_HEADS * MOBA_HEAD_DIM
    ret_qk_w = RET_HEADS * RET_QK_DIM
    ret_v_w = RET_HEADS * RET_V_DIM
    c_mq, c_mk, c_mv = 0, moba_w, 2 * moba_w
    c_rq = 3 * moba_w
    c_rk = c_rq + ret_qk_w
    c_rv = c_rk + ret_qk_w
    c_rg = c_rv + ret_v_w
    n_proj = c_rg + ret_v_w
    assert w_in.shape[2] == n_proj
    sections = (("mq", c_mq, c_mk), ("mk", c_mk, c_mv), ("mv", c_mv, c_rq), ("rq", c_rq, c_rk),
                ("rk", c_rk, c_rv), ("rv", c_rv, c_rg), ("rg", c_rg, n_proj))

    moba_inv = np.power(ROPE_THETA, -np.arange(ROPE_DIM // 2, dtype=np.float64) * 2.0 / ROPE_DIM)
    ret_inv = 1.0 / np.power(RET_THETA, np.linspace(0.0, 1.0, RET_QK_DIM // 2))
    tabs = (
        _rotary_tables(seq, moba_inv, ROPE_DIM, MOBA_HEAD_DIM, MOBA_HEAD_DIM ** -0.5 * math.log2(math.e)),
        _rotary_tables(seq, moba_inv, ROPE_DIM, MOBA_HEAD_DIM, 1.0),
        _rotary_tables(seq, ret_inv, RET_QK_DIM, RET_QK_DIM, 1.0),
        _rotary_tables(seq, ret_inv, RET_QK_DIM, RET_QK_DIM, RET_QK_DIM ** -0.5),
    )

    xf = x.reshape(batch * seq, d)
    mem2d = mem.reshape(batch * n_mem, d)
    row = lambda g: g.reshape(1, d)
    for l in range(depth):
        proj = _in_proj(xf, row(g_pre_mix[l]), w_in[l], tabs, seq, sections)
        mo = _moba(proj, batch, seq, c_mq, c_mk, c_mv)
        ro = _retention(proj, batch, seq, c_rq, c_rk, c_rv, c_rg)
        kv = _mem_kv(mem2d, row(g_mem[l]), w_ckv[l], MEM_KV_ROWS)
        x2 = _mix_cross(mo, ro, xf, kv, w_out[l], w_cq[l], w_co[l], row(g_post_mix[l]),
                        row(g_pre_cross[l]), row(g_post_cross[l]), seq, n_mem)
        xf = _ffn(x2, w_gate_up[l], w_down[l], row(g_pre_ffn[l]), row(g_post_ffn[l]))
    return xf.reshape(batch, seq, d)
```

```python
import functools
import math

import jax
import jax.numpy as jnp
import numpy as np
from jax import lax
from jax.experimental import pallas as pl
from jax.experimental.pallas import tpu as pltpu

F32 = jnp.float32
BF16 = jnp.bfloat16

NORM_EPS = 1e-6
NEG_INF = -1e30

LANES = 128
MXU_DIM = 256
VMEM_LIMIT_BYTES = 56 * 1024 * 1024

MOBA_HEAD_DIM = 64
MOBA_HEADS = 8
MOBA_BLOCK = 256
MOBA_TOPK = 3
MOBA_PREFETCH = 2
ROPE_THETA = 500000.0
ROPE_DIM = MOBA_HEAD_DIM // 4

RET_HEADS = 4
RET_QK_DIM = 64
RET_V_DIM = 128
RET_THETA = 10000.0
RET_CHUNK = 256

CROSS_HEADS = 4

IN_PROJ_ROWS = 1024
MEM_KV_ROWS = 1024
MIX_ROWS = 1024
FFN_ROWS = 512


def _dot(a, b):
    return jnp.dot(a, b, preferred_element_type=F32)


def _wdot(a, w):
    return jnp.dot(a, w.astype(BF16), preferred_element_type=F32)


def _dot_nt(a, b):
    return lax.dot_general(a, b, (((1,), (1,)), ((), ())), preferred_element_type=F32)


def _dot_tn(a, b):
    return lax.dot_general(a, b, (((0,), (0,)), ((), ())), preferred_element_type=F32)


def _rms(x, g):
    return x * lax.rsqrt(jnp.mean(x * x, axis=-1, keepdims=True) + NORM_EPS) * g


def _params(n_grid_dims):
    return pltpu.CompilerParams(
        dimension_semantics=("arbitrary",) * n_grid_dims,
        vmem_limit_bytes=VMEM_LIMIT_BYTES,
    )


def _rotary_tables(seq, inv_freq, rot_dim, head_dim, scale):
    half = rot_dim // 2
    ang = np.arange(seq, dtype=np.float64)[:, None] * inv_freq[None, :]
    cos, sin = np.cos(ang), np.sin(ang)
    pad = head_dim - rot_dim
    a = np.concatenate([cos, cos, np.ones((seq, pad))], axis=1)
    bm = np.concatenate([-sin, np.zeros((seq, half + pad))], axis=1)
    bp = np.concatenate([np.zeros((seq, half)), sin, np.zeros((seq, pad))], axis=1)
    reps = LANES // head_dim
    tabs = np.stack([np.tile(t, (1, reps)) for t in (a, bm, bp)], axis=0)
    return jnp.asarray((tabs * scale).astype(np.float32))


def _rotate(acc, tab_ref, rows, half):
    a, bm, bp = tab_ref[0, rows, :], tab_ref[1, rows, :], tab_ref[2, rows, :]
    outs = []
    for c in range(acc.shape[1] // LANES):
        xs = acc[:, c * LANES:(c + 1) * LANES]
        outs.append(xs * a + pltpu.roll(xs, LANES - half, 1) * bm + pltpu.roll(xs, half, 1) * bp)
    return jnp.concatenate(outs, axis=1)


def _in_proj_kernel(x_ref, g_ref, w_ref, tmq_ref, tmk_ref, trq_ref, trk_ref, o_ref, *, sections, seq):
    tm = x_ref.shape[0]
    rows = tm // 2
    sl = [slice(r * rows, (r + 1) * rows) for r in range(2)]
    pos0 = (pl.program_id(0) % (seq // tm)) * tm
    pos = [pl.ds(pl.multiple_of(pos0 + r * rows, rows), rows) for r in range(2)]
    h = [_rms(x_ref[s, :], g_ref[...]).astype(BF16) for s in sl]
    tabs = {"mq": (tmq_ref, ROPE_DIM // 2), "mk": (tmk_ref, ROPE_DIM // 2),
            "rq": (trq_ref, RET_QK_DIM // 2), "rk": (trk_ref, RET_QK_DIM // 2)}
    for kind, c0, c1 in sections:
        for r in range(2):
            acc = _wdot(h[r], w_ref[:, c0:c1])
            if kind in tabs:
                tab_ref, half = tabs[kind]
                acc = _rotate(acc, tab_ref, pos[r], half)
            o_ref[sl[r], c0:c1] = acc.astype(BF16)


def _in_proj(x2d, g, w, tabs, seq, sections):
    t, d = x2d.shape
    n = w.shape[1]
    tm = IN_PROJ_ROWS
    assert seq % tm == 0
    tab_spec = pl.BlockSpec((3, seq, LANES), lambda i: (0, 0, 0), pipeline_mode=pl.Buffered(1))
    return pl.pallas_call(
        functools.partial(_in_proj_kernel, sections=sections, seq=seq),
        grid=(t // tm,),
        in_specs=[
            pl.BlockSpec((tm, d), lambda i: (i, 0)),
            pl.BlockSpec((1, d), lambda i: (0, 0)),
            pl.BlockSpec((d, n), lambda i: (0, 0), pipeline_mode=pl.Buffered(1)),
            tab_spec, tab_spec, tab_spec, tab_spec,
        ],
        out_specs=pl.BlockSpec((tm, n), lambda i: (i, 0)),
        out_shape=jax.ShapeDtypeStruct((t, n), BF16),
        compiler_params=_params(1),
        name="in_proj",
    )(x2d, g, w, *tabs)


def _moba_kernel(q_ref, k_ref, v_ref, o_ref, qp_ref, vt_ref, masked_ref):
    seq = q_ref.shape[0]
    blk = MOBA_BLOCK
    nb = seq // blk
    hd = MOBA_HEAD_DIM
    q2 = q_ref[...]
    k2 = k_ref[...]
    v2 = v_ref[...]
    lane = lax.broadcasted_iota(jnp.int32, (seq, LANES), 1)
    kmean2 = jnp.mean(k2.astype(F32).reshape(nb, blk, LANES), axis=1)

    lane8 = lax.broadcasted_iota(jnp.int32, (nb, LANES), 1)
    km = jnp.concatenate([jnp.where(lane8 < hd, kmean2, 0.0), jnp.where(lane8 >= hd, kmean2, 0.0)], axis=0)
    km_hi = km.astype(BF16)
    km_lo = (km - km_hi.astype(F32)).astype(BF16)
    gate2 = _dot_nt(jnp.concatenate([km_hi, km_lo], axis=0), q2)
    jidx = lax.broadcasted_iota(jnp.int32, (nb, seq), 0)
    qblk = lax.broadcasted_iota(jnp.int32, (nb, seq), 1) // blk
    for hh in range(2):
        gate = gate2[hh * nb:(hh + 1) * nb] + gate2[(2 + hh) * nb:(3 + hh) * nb]
        rank = jnp.zeros((nb, seq), jnp.int32)
        for i in range(nb):
            gi = gate[i:i + 1, :]
            beats = ((gi > gate) | ((gi == gate) & (i < jidx))) & (i < qblk)
            rank = rank + beats.astype(jnp.int32)
        masked_ref[hh] = ((jidx >= qblk) | (rank >= MOBA_TOPK)).astype(F32)
        mine = (lane >= hh * hd) & (lane < (hh + 1) * hd)
        qp_ref[hh] = jnp.where(mine, q2, jnp.zeros_like(q2))
        vt_ref[hh] = jnp.where(mine, v2, jnp.ones_like(v2)).astype(F32).T.astype(BF16)

    kr = lax.broadcasted_iota(jnp.int32, (blk, blk), 0)
    qcol = lax.broadcasted_iota(jnp.int32, (blk, blk), 1)

    def scores(t):
        c, j = tiles[t]
        out = []
        for hh in range(2):
            s = _dot_nt(k_ref[j * blk:(j + 1) * blk, :], qp_ref[hh, c * blk:(c + 1) * blk, :])
            out.append(jnp.where(kr <= qcol, s, NEG_INF) if j == c else s)
        return out

    tiles = [(c, j) for c in range(nb) for j in range(c + 1)]
    pending = {t: scores(t) for t in range(min(MOBA_PREFETCH, len(tiles)))}
    m_run = [None, None]
    acc = [None, None]
    for t, (c, j) in enumerate(tiles):
        r0, r1 = c * blk, (c + 1) * blk
        if t + MOBA_PREFETCH < len(tiles):
            pending[t + MOBA_PREFETCH] = scores(t + MOBA_PREFETCH)
        s_cur = pending.pop(t)
        for hh in range(2):
            m_t = jnp.max(s_cur[hh], axis=0, keepdims=True)
            if j < c:
                off = masked_ref[hh, j:j + 1, r0:r1] > 0.5
                m_t = jnp.where(off, NEG_INF, m_t)
            m_new = m_t if j == 0 else jnp.maximum(m_run[hh], m_t)
            shift = jnp.where(off, -NEG_INF, m_new) if j < c else m_new
            pv = _dot(vt_ref[hh, :, j * blk:(j + 1) * blk], jnp.exp2(s_cur[hh] - shift).astype(BF16))
            acc[hh] = pv if j == 0 else acc[hh] * jnp.exp2(m_run[hh] - m_new) + pv
            m_run[hh] = m_new
        if j == c:
            res = [acc[hh][hh * hd:(hh + 1) * hd, :] / acc[hh][hd - hh * hd:2 * hd - hh * hd, :]
                   for hh in range(2)]
            o_ref[r0:r1, :] = jnp.concatenate(res, axis=0).T.astype(BF16)


def _moba(proj, batch, seq, q_col, k_col, v_col):
    t = proj.shape[0]
    pairs = MOBA_HEADS // 2

    def spec(col0):
        cb = col0 // LANES
        return pl.BlockSpec((seq, LANES), lambda b, p: (b, cb + p))

    return pl.pallas_call(
        _moba_kernel,
        grid=(batch, pairs),
        in_specs=[spec(q_col), spec(k_col), spec(v_col)],
        out_specs=pl.BlockSpec((seq, LANES), lambda b, p: (b, p)),
        out_shape=jax.ShapeDtypeStruct((t, pairs * LANES), BF16),
        scratch_shapes=[pltpu.VMEM((2, seq, LANES), BF16), pltpu.VMEM((2, LANES, seq), BF16),
                        pltpu.VMEM((2, seq // MOBA_BLOCK, seq), F32)],
        compiler_params=_params(2),
        name="moba",
    )(proj, proj, proj)


_RET_LOG_G = [math.log(1.0 - 2.0 ** (-5.0 - h)) for h in range(RET_HEADS)]


def _retention_kernel(q_ref, k_ref, v_ref, g_ref, o_ref):
    seq = q_ref.shape[0]
    ch = RET_CHUNK
    dv = RET_V_DIM
    pair = pl.program_id(1)
    lane = lax.broadcasted_iota(jnp.int32, (ch, LANES), 1)
    rowf = lax.broadcasted_iota(jnp.int32, (ch, LANES), 0).astype(F32)
    ri = lax.broadcasted_iota(jnp.int32, (ch, ch), 0)
    ci = lax.broadcasted_iota(jnp.int32, (ch, ch), 1)
    diff = (ri - ci).astype(F32)
    for hh in range(2):
        log_g = jnp.where(pair == 0, _RET_LOG_G[hh], _RET_LOG_G[2 + hh]).astype(F32)
        inner_decay = jnp.where(diff >= 0, jnp.exp(log_g * jnp.maximum(diff, 0.0)), 0.0)
        q_decay = jnp.exp(log_g * (rowf + 1.0))
        k_decay = jnp.exp(log_g * (ch - 1.0 - rowf))
        chunk_decay = jnp.exp(jnp.full((LANES, dv), ch, F32) * log_g)
        mine = (lane >= hh * RET_QK_DIM) & (lane < (hh + 1) * RET_QK_DIM)
        state = jnp.zeros((LANES, dv), F32)
        for n in range(seq // ch):
            r0, r1 = n * ch, (n + 1) * ch
            qc = jnp.where(mine, q_ref[r0:r1, :], jnp.zeros((), BF16))
            kc = k_ref[r0:r1, :]
            vc = v_ref[r0:r1, hh * dv:(hh + 1) * dv]
            attn = _dot_nt(qc, kc) * inner_decay
            out = _dot(attn.astype(BF16), vc)
            out = out + _dot((qc.astype(F32) * q_decay).astype(BF16), state.astype(BF16))
            state = state * chunk_decay + _dot_tn((kc.astype(F32) * k_decay).astype(BF16), vc)
            out = out * lax.rsqrt(jnp.mean(out * out, axis=-1, keepdims=True) + NORM_EPS)
            gate = g_ref[r0:r1, hh * dv:(hh + 1) * dv].astype(F32)
            o_ref[r0:r1, hh * dv:(hh + 1) * dv] = (gate * jax.nn.sigmoid(gate) * out).astype(BF16)


def _retention(proj, batch, seq, q_col, k_col, v_col, g_col):
    t = proj.shape[0]
    pairs = RET_HEADS // 2
    wv = 2 * RET_V_DIM

    def spec(col0, width):
        cb = col0 // width
        return pl.BlockSpec((seq, width), lambda b, p: (b, cb + p))

    return pl.pallas_call(
        _retention_kernel,
        grid=(batch, pairs),
        in_specs=[spec(q_col, LANES), spec(k_col, LANES), spec(v_col, wv), spec(g_col, wv)],
        out_specs=pl.BlockSpec((seq, wv), lambda b, p: (b, p)),
        out_shape=jax.ShapeDtypeStruct((t, pairs * wv), BF16),
        compiler_params=_params(2),
        name="retention",
    )(proj, proj, proj, proj)


def _mem_kv_kernel(m_ref, g_ref, w_ref, o_ref):
    o_ref[...] = _wdot(_rms(m_ref[...], g_ref[...]).astype(BF16), w_ref[...]).astype(BF16)


def _mem_kv(mem2d, g, w, rows):
    t, d = mem2d.shape
    n = w.shape[1]
    return pl.pallas_call(
        _mem_kv_kernel,
        grid=(t // rows,),
        in_specs=[pl.BlockSpec((rows, d), lambda i: (i, 0)),
                  pl.BlockSpec((1, d), lambda i: (0, 0)),
                  pl.BlockSpec((d, n), lambda i: (0, 0), pipeline_mode=pl.Buffered(1))],
        out_specs=pl.BlockSpec((rows, n), lambda i: (i, 0)),
        out_shape=jax.ShapeDtypeStruct((t, n), BF16),
        compiler_params=_params(1),
        name="mem_kv",
    )(mem2d, g, w)


def _mix_cross_kernel(mo_ref, ro_ref, x_ref, kv_ref, wo_ref, wq_ref, wc_ref,
                      g_mix_ref, g_pre_ref, g_post_ref, o_ref):
    d = x_ref.shape[1]
    half = mo_ref.shape[1]
    dc = d // CROSS_HEADS
    rows = x_ref.shape[0] // 2
    sl = [slice(r * rows, (r + 1) * rows) for r in range(2)]
    mix = [_wdot(mo_ref[s, :], wo_ref[0:half, :]) + _wdot(ro_ref[s, :], wo_ref[half:, :]) for s in sl]
    x1 = [x_ref[s, :] + _rms(m, g_mix_ref[...]) for s, m in zip(sl, mix)]
    h = [_rms(x, g_pre_ref[...]).astype(BF16) for x in x1]
    cq = [(_wdot(hh, wq_ref[...]) * (dc ** -0.5)).astype(BF16) for hh in h]
    att = []
    for r in range(2):
        heads = []
        for hc in range(CROSS_HEADS):
            c0, c1 = hc * dc, (hc + 1) * dc
            s = _dot_nt(cq[r][:, c0:c1], kv_ref[:, c0:c1])
            e = jnp.exp(s - jnp.max(s, axis=1, keepdims=True))
            o = _dot(e.astype(BF16), kv_ref[:, d + c0:d + c1])
            heads.append((o / jnp.sum(e, axis=1, keepdims=True)).astype(BF16))
        att.append(jnp.concatenate(heads, axis=1))
    c = [_wdot(a, wc_ref[...]) for a in att]
    for r in range(2):
        o_ref[sl[r], :] = x1[r] + _rms(c[r], g_post_ref[...])


def _mix_cross(mo, ro, x2d, kv, w_out, w_cq, w_co, g_mix, g_pre, g_post, seq, n_mem):
    t, d = x2d.shape
    tm = MIX_ROWS
    tiles_per_seq = seq // tm
    half = mo.shape[1]
    full = lambda i: (0, 0)
    weight = pl.BlockSpec((d, d), full, pipeline_mode=pl.Buffered(1))
    return pl.pallas_call(
        _mix_cross_kernel,
        grid=(t // tm,),
        in_specs=[
            pl.BlockSpec((tm, half), lambda i: (i, 0)),
            pl.BlockSpec((tm, half), lambda i: (i, 0)),
            pl.BlockSpec((tm, d), lambda i: (i, 0)),
            pl.BlockSpec((n_mem, 2 * d), lambda i: (i // tiles_per_seq, 0)),
            weight, weight, weight,
            pl.BlockSpec((1, d), full), pl.BlockSpec((1, d), full), pl.BlockSpec((1, d), full),
        ],
        out_specs=pl.BlockSpec((tm, d), lambda i: (i, 0)),
        out_shape=jax.ShapeDtypeStruct((t, d), F32),
        compiler_params=_params(1),
        name="mix_cross",
    )(mo, ro, x2d, kv, w_out, w_cq, w_co, g_mix, g_pre, g_post)


def _ffn_kernel(x_ref, wgu_ref, wd_ref, g_pre_ref, g_post_ref, o_ref, *, d_ff, bounds):
    rows = x_ref.shape[0] // 2
    xs = [x_ref[r * rows:(r + 1) * rows, :] for r in range(2)]
    hs = [_rms(x, g_pre_ref[...]).astype(BF16) for x in xs]
    fs = [None, None]
    for c0, c1 in bounds:
        for r in range(2):
            gate = _wdot(hs[r], wgu_ref[:, c0:c1])
            up = _wdot(hs[r], wgu_ref[:, d_ff + c0:d_ff + c1])
            act = (gate * jax.nn.sigmoid(gate) * up).astype(BF16)
            down = _wdot(act, wd_ref[c0:c1, :])
            fs[r] = down if fs[r] is None else fs[r] + down
    for r in range(2):
        o_ref[r * rows:(r + 1) * rows, :] = xs[r] + _rms(fs[r], g_post_ref[...])


def _ffn(x2d, w_gate_up, w_down, g_pre, g_post):
    t, d = x2d.shape
    d_ff = w_down.shape[0]
    tm = FFN_ROWS
    assert d_ff % MXU_DIM == 0
    mid = (d_ff // MXU_DIM // 2) * MXU_DIM
    bounds = ((0, mid), (mid, d_ff))
    full = lambda i: (0, 0)
    return pl.pallas_call(
        functools.partial(_ffn_kernel, d_ff=d_ff, bounds=bounds),
        grid=(t // tm,),
        in_specs=[
            pl.BlockSpec((tm, d), lambda i: (i, 0)),
            pl.BlockSpec((d, 2 * d_ff), full, pipeline_mode=pl.Buffered(1)),
            pl.BlockSpec((d_ff, d), full, pipeline_mode=pl.Buffered(1)),
            pl.BlockSpec((1, d), full), pl.BlockSpec((1, d), full),
        ],
        out_specs=pl.BlockSpec((tm, d), lambda i: (i, 0)),
        out_shape=jax.ShapeDtypeStruct((t, d), F32),
        compiler_params=_params(1),
        name="ffn",
    )(x2d, w_gate_up, w_down, g_pre, g_post)


def kernel(x, mem, g_pre_mix, w_in, w_out, g_post_mix, g_pre_cross, g_mem, w_cq, w_ckv, w_co,
           g_post_cross, g_pre_ffn, w_gate_up, w_down, g_post_ffn):
    batch, seq, d = x.shape
    n_mem = mem.shape[1]
    depth = w_in.shape[0]
    moba_w = MOBA_HEADS * MOBA_HEAD_DIM
    ret_qk_w = RET_HEADS * RET_QK_DIM
    ret_v_w = RET_HEADS * RET_V_DIM
    c_mq, c_mk, c_mv = 0, moba_w, 2 * moba_w
    c_rq = 3 * moba_w
    c_rk = c_rq + ret_qk_w
    c_rv = c_rk + ret_qk_w
    c_rg = c_rv + ret_v_w
    n_proj = c_rg + ret_v_w
    assert w_in.shape[2] == n_proj
    sections = (("mq", c_mq, c_mk), ("mk", c_mk, c_mv), ("mv", c_mv, c_rq), ("rq", c_rq, c_rk),
                ("rk", c_rk, c_rv), ("rv", c_rv, c_rg), ("rg", c_rg, n_proj))

    moba_inv = np.power(ROPE_THETA, -np.arange(ROPE_DIM // 2, dtype=np.float64) * 2.0 / ROPE_DIM)
    ret_inv = 1.0 / np.power(RET_THETA, np.linspace(0.0, 1.0, RET_QK_DIM // 2))
    tabs = (
        _rotary_tables(seq, moba_inv, ROPE_DIM, MOBA_HEAD_DIM, MOBA_HEAD_DIM ** -0.5 * math.log2(math.e)),
        _rotary_tables(seq, moba_inv, ROPE_DIM, MOBA_HEAD_DIM, 1.0),
        _rotary_tables(seq, ret_inv, RET_QK_DIM, RET_QK_DIM, 1.0),
        _rotary_tables(seq, ret_inv, RET_QK_DIM, RET_QK_DIM, RET_QK_DIM ** -0.5),
    )

    xf = x.reshape(batch * seq, d)
    mem2d = mem.reshape(batch * n_mem, d)
    row = lambda g: g.reshape(1, d)
    for l in range(depth):
        proj = _in_proj(xf, row(g_pre_mix[l]), w_in[l], tabs, seq, sections)
        mo = _moba(proj, batch, seq, c_mq, c_mk, c_mv)
        ro = _retention(proj, batch, seq, c_rq, c_rk, c_rv, c_rg)
        kv = _mem_kv(mem2d, row(g_mem[l]), w_ckv[l], MEM_KV_ROWS)
        x2 = _mix_cross(mo, ro, xf, kv, w_out[l], w_cq[l], w_co[l], row(g_post_mix[l]),
                        row(g_pre_cross[l]), row(g_post_cross[l]), seq, n_mem)
        xf = _ffn(x2, w_gate_up[l], w_down[l], row(g_pre_ffn[l]), row(g_post_ffn[l]))
    return xf.reshape(batch, seq, d)
```

```python
import functools
import math

import jax
import jax.numpy as jnp
import numpy as np
from jax import lax
from jax.experimental import pallas as pl
from jax.experimental.pallas import tpu as pltpu

F32 = jnp.float32
BF16 = jnp.bfloat16

NORM_EPS = 1e-6
NEG_INF = -1e30

LANES = 128
MXU_DIM = 256
VMEM_LIMIT_BYTES = 56 * 1024 * 1024

MOBA_HEAD_DIM = 64
MOBA_HEADS = 8
MOBA_BLOCK = 256
MOBA_TOPK = 3
MOBA_ONES_ROWS = 16
MOBA_PREFETCH = 2
ROPE_THETA = 500000.0
ROPE_DIM = MOBA_HEAD_DIM // 4

RET_HEADS = 4
RET_QK_DIM = 64
RET_V_DIM = 128
RET_THETA = 10000.0
RET_CHUNK = 256

CROSS_HEADS = 4

IN_PROJ_ROWS = 1024
MEM_KV_ROWS = 1024
MIX_ROWS = 1024
MIX_PARTS = 2
FFN_ROWS = 512


def _dot(a, b):
    return jnp.dot(a, b, preferred_element_type=F32)


def _wdot(a, w):
    return jnp.dot(a, w.astype(BF16), preferred_element_type=F32)


def _dot_nt(a, b):
    return lax.dot_general(a, b, (((1,), (1,)), ((), ())), preferred_element_type=F32)


def _dot_tn(a, b):
    return lax.dot_general(a, b, (((0,), (0,)), ((), ())), preferred_element_type=F32)


def _rms(x, g):
    return x * lax.rsqrt(jnp.mean(x * x, axis=-1, keepdims=True) + NORM_EPS) * g


def _params(n_grid_dims):
    return pltpu.CompilerParams(
        dimension_semantics=("arbitrary",) * n_grid_dims,
        vmem_limit_bytes=VMEM_LIMIT_BYTES,
    )


def _rotary_tables(seq, inv_freq, rot_dim, head_dim, scale):
    half = rot_dim // 2
    ang = np.arange(seq, dtype=np.float64)[:, None] * inv_freq[None, :]
    cos, sin = np.cos(ang), np.sin(ang)
    pad = head_dim - rot_dim
    a = np.concatenate([cos, cos, np.ones((seq, pad))], axis=1)
    bm = np.concatenate([-sin, np.zeros((seq, half + pad))], axis=1)
    bp = np.concatenate([np.zeros((seq, half)), sin, np.zeros((seq, pad))], axis=1)
    reps = LANES // head_dim
    tabs = np.stack([np.tile(t, (1, reps)) for t in (a, bm, bp)], axis=0)
    return jnp.asarray((tabs * scale).astype(np.float32))


def _rotate(acc, tab_ref, rows, half):
    a, bm, bp = tab_ref[0, rows, :], tab_ref[1, rows, :], tab_ref[2, rows, :]
    outs = []
    for c in range(acc.shape[1] // LANES):
        xs = acc[:, c * LANES:(c + 1) * LANES]
        outs.append(xs * a + pltpu.roll(xs, LANES - half, 1) * bm + pltpu.roll(xs, half, 1) * bp)
    return jnp.concatenate(outs, axis=1)


def _in_proj_kernel(x_ref, g_ref, w_ref, tmq_ref, tmk_ref, trq_ref, trk_ref, o_ref, *, sections):
    rows = x_ref.shape[0] // 2
    sl = [slice(r * rows, (r + 1) * rows) for r in range(2)]
    h = [_rms(x_ref[s, :], g_ref[...]).astype(BF16) for s in sl]
    tabs = {"mq": (tmq_ref, ROPE_DIM // 2), "mk": (tmk_ref, ROPE_DIM // 2),
            "rq": (trq_ref, RET_QK_DIM // 2), "rk": (trk_ref, RET_QK_DIM // 2)}
    for kind, c0, c1 in sections:
        for r in range(2):
            acc = _wdot(h[r], w_ref[:, c0:c1])
            if kind in tabs:
                tab_ref, half = tabs[kind]
                acc = _rotate(acc, tab_ref, sl[r], half)
            o_ref[sl[r], c0:c1] = acc.astype(BF16)


def _in_proj(x2d, g, w, tabs, seq, sections):
    t, d = x2d.shape
    n = w.shape[1]
    tm = IN_PROJ_ROWS
    tiles_per_seq = seq // tm
    tab_spec = pl.BlockSpec((3, tm, LANES), lambda i: (0, i % tiles_per_seq, 0))
    return pl.pallas_call(
        functools.partial(_in_proj_kernel, sections=sections),
        grid=(t // tm,),
        in_specs=[
            pl.BlockSpec((tm, d), lambda i: (i, 0)),
            pl.BlockSpec((1, d), lambda i: (0, 0)),
            pl.BlockSpec((d, n), lambda i: (0, 0), pipeline_mode=pl.Buffered(1)),
            tab_spec, tab_spec, tab_spec, tab_spec,
        ],
        out_specs=pl.BlockSpec((tm, n), lambda i: (i, 0)),
        out_shape=jax.ShapeDtypeStruct((t, n), BF16),
        compiler_params=_params(1),
        name="in_proj",
    )(x2d, g, w, *tabs)


def _moba_kernel(q_ref, k_ref, v_ref, o_ref, qp_ref, vt_ref, masked_ref):
    seq = q_ref.shape[0]
    blk = MOBA_BLOCK
    nb = seq // blk
    hd = MOBA_HEAD_DIM
    q2 = q_ref[...]
    k2 = k_ref[...]
    lane = lax.broadcasted_iota(jnp.int32, (seq, LANES), 1)
    kmean2 = jnp.mean(k2.astype(F32).reshape(nb, blk, LANES), axis=1)

    lane8 = lax.broadcasted_iota(jnp.int32, (nb, LANES), 1)
    km = jnp.concatenate([jnp.where(lane8 < hd, kmean2, 0.0), jnp.where(lane8 >= hd, kmean2, 0.0)], axis=0)
    km_hi = km.astype(BF16)
    km_lo = (km - km_hi.astype(F32)).astype(BF16)
    gate2 = _dot_nt(jnp.concatenate([km_hi, km_lo], axis=0), q2)
    jidx = lax.broadcasted_iota(jnp.int32, (nb, seq), 0)
    qblk = lax.broadcasted_iota(jnp.int32, (nb, seq), 1) // blk
    for hh in range(2):
        gate = gate2[hh * nb:(hh + 1) * nb] + gate2[(2 + hh) * nb:(3 + hh) * nb]
        rank = jnp.zeros((nb, seq), jnp.int32)
        for i in range(nb):
            gi = gate[i:i + 1, :]
            beats = ((gi > gate) | ((gi == gate) & (i < jidx))) & (i < qblk)
            rank = rank + beats.astype(jnp.int32)
        masked_ref[hh] = ((jidx >= qblk) | (rank >= MOBA_TOPK)).astype(F32)
        mine = (lane >= hh * hd) & (lane < (hh + 1) * hd)
        qp_ref[hh] = jnp.where(mine, q2, jnp.zeros_like(q2))
    v2t = v_ref[...].astype(F32).T
    ones = jnp.ones((MOBA_ONES_ROWS, seq), F32)
    for hh in range(2):
        vt_ref[hh] = jnp.concatenate([v2t[hh * hd:(hh + 1) * hd], ones], axis=0).astype(BF16)

    kr = lax.broadcasted_iota(jnp.int32, (blk, blk), 0)
    qcol = lax.broadcasted_iota(jnp.int32, (blk, blk), 1)

    def scores(t):
        c, j = tiles[t]
        out = []
        for hh in range(2):
            s = _dot_nt(k_ref[j * blk:(j + 1) * blk, :], qp_ref[hh, c * blk:(c + 1) * blk, :])
            out.append(jnp.where(kr <= qcol, s, NEG_INF) if j == c else s)
        return out

    tiles = [(c, j) for c in range(nb) for j in range(c + 1)]
    pending = {t: scores(t) for t in range(min(MOBA_PREFETCH, len(tiles)))}
    m_run = [None, None]
    acc = [None, None]
    for t, (c, j) in enumerate(tiles):
        r0, r1 = c * blk, (c + 1) * blk
        if t + MOBA_PREFETCH < len(tiles):
            pending[t + MOBA_PREFETCH] = scores(t + MOBA_PREFETCH)
        s_cur = pending.pop(t)
        for hh in range(2):
            m_t = jnp.max(s_cur[hh], axis=0, keepdims=True)
            if j < c:
                off = masked_ref[hh, j:j + 1, r0:r1] > 0.5
                m_t = jnp.where(off, NEG_INF, m_t)
            m_new = m_t if j == 0 else jnp.maximum(m_run[hh], m_t)
            shift = jnp.where(off, -NEG_INF, m_new) if j < c else m_new
            pv = _dot(vt_ref[hh, :, j * blk:(j + 1) * blk], jnp.exp2(s_cur[hh] - shift).astype(BF16))
            acc[hh] = pv if j == 0 else acc[hh] * jnp.exp2(m_run[hh] - m_new) + pv
            m_run[hh] = m_new
        if j == c:
            res = [acc[hh][0:hd, :] / acc[hh][hd:hd + 1, :] for hh in range(2)]
            o_ref[r0:r1, :] = jnp.concatenate(res, axis=0).T.astype(BF16)


def _moba(proj, batch, seq, q_col, k_col, v_col):
    t = proj.shape[0]
    pairs = MOBA_HEADS // 2

    def spec(col0):
        cb = col0 // LANES
        return pl.BlockSpec((seq, LANES), lambda b, p: (b, cb + p))

    return pl.pallas_call(
        _moba_kernel,
        grid=(batch, pairs),
        in_specs=[spec(q_col), spec(k_col), spec(v_col)],
        out_specs=pl.BlockSpec((seq, LANES), lambda b, p: (b, p)),
        out_shape=jax.ShapeDtypeStruct((t, pairs * LANES), BF16),
        scratch_shapes=[pltpu.VMEM((2, seq, LANES), BF16),
                        pltpu.VMEM((2, MOBA_HEAD_DIM + MOBA_ONES_ROWS, seq), BF16),
                        pltpu.VMEM((2, seq // MOBA_BLOCK, seq), F32)],
        compiler_params=_params(2),
        name="moba",
    )(proj, proj, proj)


_RET_LOG_G = [math.log(1.0 - 2.0 ** (-5.0 - h)) for h in range(RET_HEADS)]


def _retention_kernel(q_ref, k_ref, v_ref, g_ref, o_ref):
    seq = q_ref.shape[0]
    ch = RET_CHUNK
    dv = RET_V_DIM
    pair = pl.program_id(1)
    lane = lax.broadcasted_iota(jnp.int32, (ch, LANES), 1)
    rowf = lax.broadcasted_iota(jnp.int32, (ch, LANES), 0).astype(F32)
    ri = lax.broadcasted_iota(jnp.int32, (ch, ch), 0)
    ci = lax.broadcasted_iota(jnp.int32, (ch, ch), 1)
    diff = (ri - ci).astype(F32)
    for hh in range(2):
        log_g = jnp.where(pair == 0, _RET_LOG_G[hh], _RET_LOG_G[2 + hh]).astype(F32)
        inner_decay = jnp.where(diff >= 0, jnp.exp(log_g * jnp.maximum(diff, 0.0)), 0.0)
        q_decay = jnp.exp(log_g * (rowf + 1.0))
        k_decay = jnp.exp(log_g * (ch - 1.0 - rowf))
        chunk_decay = jnp.exp(jnp.full((LANES, dv), ch, F32) * log_g)
        mine = (lane >= hh * RET_QK_DIM) & (lane < (hh + 1) * RET_QK_DIM)
        state = jnp.zeros((LANES, dv), F32)
        for n in range(seq // ch):
            r0, r1 = n * ch, (n + 1) * ch
            qc = jnp.where(mine, q_ref[r0:r1, :], jnp.zeros((), BF16))
            kc = k_ref[r0:r1, :]
            vc = v_ref[r0:r1, hh * dv:(hh + 1) * dv]
            attn = _dot_nt(qc, kc) * inner_decay
            out = _dot(attn.astype(BF16), vc)
            out = out + _dot((qc.astype(F32) * q_decay).astype(BF16), state.astype(BF16))
            state = state * chunk_decay + _dot_tn((kc.astype(F32) * k_decay).astype(BF16), vc)
            out = out * lax.rsqrt(jnp.mean(out * out, axis=-1, keepdims=True) + NORM_EPS)
            gate = g_ref[r0:r1, hh * dv:(hh + 1) * dv].astype(F32)
            o_ref[r0:r1, hh * dv:(hh + 1) * dv] = (gate * jax.nn.sigmoid(gate) * out).astype(BF16)


def _retention(proj, batch, seq, q_col, k_col, v_col, g_col):
    t = proj.shape[0]
    pairs = RET_HEADS // 2
    wv = 2 * RET_V_DIM

    def spec(col0, width):
        cb = col0 // width
        return pl.BlockSpec((seq, width), lambda b, p: (b, cb + p))

    return pl.pallas_call(
        _retention_kernel,
        grid=(batch, pairs),
        in_specs=[spec(q_col, LANES), spec(k_col, LANES), spec(v_col, wv), spec(g_col, wv)],
        out_specs=pl.BlockSpec((seq, wv), lambda b, p: (b, p)),
        out_shape=jax.ShapeDtypeStruct((t, pairs * wv), BF16),
        compiler_params=_params(2),
        name="retention",
    )(proj, proj, proj, proj)


def _mem_kv_kernel(m_ref, g_ref, w_ref, o_ref):
    o_ref[...] = _wdot(_rms(m_ref[...], g_ref[...]).astype(BF16), w_ref[...]).astype(BF16)


def _mem_kv(mem2d, g, w, rows):
    t, d = mem2d.shape
    n = w.shape[1]
    return pl.pallas_call(
        _mem_kv_kernel,
        grid=(t // rows,),
        in_specs=[pl.BlockSpec((rows, d), lambda i: (i, 0)),
                  pl.BlockSpec((1, d), lambda i: (0, 0)),
                  pl.BlockSpec((d, n), lambda i: (0, 0), pipeline_mode=pl.Buffered(1))],
        out_specs=pl.BlockSpec((rows, n), lambda i: (i, 0)),
        out_shape=jax.ShapeDtypeStruct((t, n), BF16),
        compiler_params=_params(1),
        name="mem_kv",
    )(mem2d, g, w)


def _mix_cross_kernel(mo_ref, ro_ref, x_ref, kv_ref, wo_ref, wq_ref, wc_ref,
                      g_mix_ref, g_pre_ref, g_post_ref, o_ref):
    d = x_ref.shape[1]
    half = mo_ref.shape[1]
    dc = d // CROSS_HEADS
    rows = x_ref.shape[0] // MIX_PARTS
    sl = [slice(r * rows, (r + 1) * rows) for r in range(MIX_PARTS)]
    mix = [_wdot(mo_ref[s, :], wo_ref[0:half, :]) + _wdot(ro_ref[s, :], wo_ref[half:, :]) for s in sl]
    x1 = [x_ref[s, :] + _rms(m, g_mix_ref[...]) for s, m in zip(sl, mix)]
    h = [_rms(x, g_pre_ref[...]).astype(BF16) for x in x1]
    cq = [(_wdot(hh, wq_ref[...]) * (dc ** -0.5)).astype(BF16) for hh in h]
    att = []
    for r in range(MIX_PARTS):
        heads = []
        for hc in range(CROSS_HEADS):
            c0, c1 = hc * dc, (hc + 1) * dc
            s = _dot_nt(cq[r][:, c0:c1], kv_ref[:, c0:c1])
            e = jnp.exp(s - jnp.max(s, axis=1, keepdims=True))
            o = _dot(e.astype(BF16), kv_ref[:, d + c0:d + c1])
            heads.append((o / jnp.sum(e, axis=1, keepdims=True)).astype(BF16))
        att.append(jnp.concatenate(heads, axis=1))
    c = [_wdot(a, wc_ref[...]) for a in att]
    for r in range(MIX_PARTS):
        o_ref[sl[r], :] = x1[r] + _rms(c[r], g_post_ref[...])


def _mix_cross(mo, ro, x2d, kv, w_out, w_cq, w_co, g_mix, g_pre, g_post, seq, n_mem):
    t, d = x2d.shape
    tm = MIX_ROWS
    tiles_per_seq = seq // tm
    half = mo.shape[1]
    full = lambda i: (0, 0)
    weight = pl.BlockSpec((d, d), full, pipeline_mode=pl.Buffered(1))
    return pl.pallas_call(
        _mix_cross_kernel,
        grid=(t // tm,),
        in_specs=[
            pl.BlockSpec((tm, half), lambda i: (i, 0)),
            pl.BlockSpec((tm, half), lambda i: (i, 0)),
            pl.BlockSpec((tm, d), lambda i: (i, 0)),
            pl.BlockSpec((n_mem, 2 * d), lambda i: (i // tiles_per_seq, 0)),
            weight, weight, weight,
            pl.BlockSpec((1, d), full), pl.BlockSpec((1, d), full), pl.BlockSpec((1, d), full),
        ],
        out_specs=pl.BlockSpec((tm, d), lambda i: (i, 0)),
        out_shape=jax.ShapeDtypeStruct((t, d), F32),
        compiler_params=_params(1),
        name="mix_cross",
    )(mo, ro, x2d, kv, w_out, w_cq, w_co, g_mix, g_pre, g_post)


def _ffn_kernel(x_ref, wgu_ref, wd_ref, g_pre_ref, g_post_ref, o_ref, *, d_ff, bounds):
    rows = x_ref.shape[0] // 2
    xs = [x_ref[r * rows:(r + 1) * rows, :] for r in range(2)]
    hs = [_rms(x, g_pre_ref[...]).astype(BF16) for x in xs]
    fs = [None, None]
    for c0, c1 in bounds:
        for r in range(2):
            gate = _wdot(hs[r], wgu_ref[:, c0:c1])
            up = _wdot(hs[r], wgu_ref[:, d_ff + c0:d_ff + c1])
            act = (gate * jax.nn.sigmoid(gate) * up).astype(BF16)
            down = _wdot(act, wd_ref[c0:c1, :])
            fs[r] = down if fs[r] is None else fs[r] + down
    for r in range(2):
        o_ref[r * rows:(r + 1) * rows, :] = xs[r] + _rms(fs[r], g_post_ref[...])


def _ffn(x2d, w_gate_up, w_down, g_pre, g_post):
    t, d = x2d.shape
    d_ff = w_down.shape[0]
    tm = FFN_ROWS
    assert d_ff % MXU_DIM == 0
    mid = (d_ff // MXU_DIM // 2) * MXU_DIM
    bounds = ((0, mid), (mid, d_ff))
    full = lambda i: (0, 0)
    return pl.pallas_call(
        functools.partial(_ffn_kernel, d_ff=d_ff, bounds=bounds),
        grid=(t // tm,),
        in_specs=[
            pl.BlockSpec((tm, d), lambda i: (i, 0)),
            pl.BlockSpec((d, 2 * d_ff), full, pipeline_mode=pl.Buffered(1)),
            pl.BlockSpec((d_ff, d), full, pipeline_mode=pl.Buffered(1)),
            pl.BlockSpec((1, d), full), pl.BlockSpec((1, d), full),
        ],
        out_specs=pl.BlockSpec((tm, d), lambda i: (i, 0)),
        out_shape=jax.ShapeDtypeStruct((t, d), F32),
        compiler_params=_params(1),
        name="ffn",
    )(x2d, w_gate_up, w_down, g_pre, g_post)


def kernel(x, mem, g_pre_mix, w_in, w_out, g_post_mix, g_pre_cross, g_mem, w_cq, w_ckv, w_co,
           g_post_cross, g_pre_ffn, w_gate_up, w_down, g_post_ffn):
    batch, seq, d = x.shape
    n_mem = mem.shape[1]
    depth = w_in.shape[0]
    moba_w = MOBA_HEADS * MOBA_HEAD_DIM
    ret_qk_w = RET_HEADS * RET_QK_DIM
    ret_v_w = RET_HEADS * RET_V_DIM
    c_mq, c_mk, c_mv = 0, moba_w, 2 * moba_w
    c_rq = 3 * moba_w
    c_rk = c_rq + ret_qk_w
    c_rv = c_rk + ret_qk_w
    c_rg = c_rv + ret_v_w
    n_proj = c_rg + ret_v_w
    assert w_in.shape[2] == n_proj
    sections = (("mq", c_mq, c_mk), ("mk", c_mk, c_mv), ("mv", c_mv, c_rq), ("rq", c_rq, c_rk),
                ("rk", c_rk, c_rv), ("rv", c_rv, c_rg), ("rg", c_rg, n_proj))

    moba_inv = np.power(ROPE_THETA, -np.arange(ROPE_DIM // 2, dtype=np.float64) * 2.0 / ROPE_DIM)
    ret_inv = 1.0 / np.power(RET_THETA, np.linspace(0.0, 1.0, RET_QK_DIM // 2))
    tabs = (
        _rotary_tables(seq, moba_inv, ROPE_DIM, MOBA_HEAD_DIM, MOBA_HEAD_DIM ** -0.5 * math.log2(math.e)),
        _rotary_tables(seq, moba_inv, ROPE_DIM, MOBA_HEAD_DIM, 1.0),
        _rotary_tables(seq, ret_inv, RET_QK_DIM, RET_QK_DIM, 1.0),
        _rotary_tables(seq, ret_inv, RET_QK_DIM, RET_QK_DIM, RET_QK_DIM ** -0.5),
    )

    xf = x.reshape(batch * seq, d)
    mem2d = mem.reshape(batch * n_mem, d)
    row = lambda g: g.reshape(1, d)
    for l in range(depth):
        proj = _in_proj(xf, row(g_pre_mix[l]), w_in[l], tabs, seq, sections)
        mo = _moba(proj, batch, seq, c_mq, c_mk, c_mv)
        ro = _retention(proj, batch, seq, c_rq, c_rk, c_rv, c_rg)
        kv = _mem_kv(mem2d, row(g_mem[l]), w_ckv[l], MEM_KV_ROWS)
        x2 = _mix_cross(mo, ro, xf, kv, w_out[l], w_cq[l], w_co[l], row(g_post_mix[l]),
                        row(g_pre_cross[l]), row(g_post_cross[l]), seq, n_mem)
        xf = _ffn(x2, w_gate_up[l], w_down[l], row(g_pre_ffn[l]), row(g_post_ffn[l]))
    return xf.reshape(batch, seq, d)
```

```python
import functools
import math

import jax
import jax.numpy as jnp
import numpy as np
from jax import lax
from jax.experimental import pallas as pl
from jax.experimental.pallas import tpu as pltpu

F32 = jnp.float32
BF16 = jnp.bfloat16

NORM_EPS = 1e-6
NEG_INF = -1e30

LANES = 128
BF16_SUBLANES = 16
MXU_DIM = 256
VMEM_LIMIT_BYTES = 56 * 1024 * 1024

MOBA_HEAD_DIM = 64
MOBA_HEADS = 8
MOBA_BLOCK = 256
MOBA_TOPK = 3
MOBA_ONES_ROWS = 16
MOBA_PREFETCH = 2
ROPE_THETA = 500000.0
ROPE_DIM = MOBA_HEAD_DIM // 4

RET_HEADS = 4
RET_QK_DIM = 64
RET_V_DIM = 128
RET_THETA = 10000.0
RET_CHUNK = 256

CROSS_HEADS = 4

IN_PROJ_ROWS = 1024
MEM_KV_ROWS = 1024
MIX_ROWS = 1024
MIX_PARTS = 2
FFN_ROWS = 1024


def _dot(a, b):
    return jnp.dot(a, b, preferred_element_type=F32)


def _wdot(a, w):
    return jnp.dot(a, w.astype(BF16), preferred_element_type=F32)


def _dot_nt(a, b):
    return lax.dot_general(a, b, (((1,), (1,)), ((), ())), preferred_element_type=F32)


def _dot_tn(a, b):
    return lax.dot_general(a, b, (((0,), (0,)), ((), ())), preferred_element_type=F32)


def _rms(x, g, scale=None):
    inv = lax.rsqrt(jnp.mean(x * x, axis=-1, keepdims=True) + NORM_EPS)
    if scale is not None:
        inv = inv * scale
    return x * inv * g


def _params(n_grid_dims):
    return pltpu.CompilerParams(
        dimension_semantics=("arbitrary",) * n_grid_dims,
        vmem_limit_bytes=VMEM_LIMIT_BYTES,
    )


def _rotary_tables(seq, inv_freq, rot_dim, head_dim, scale):
    half = rot_dim // 2
    ang = np.arange(seq, dtype=np.float64)[:, None] * inv_freq[None, :]
    cos, sin = np.cos(ang), np.sin(ang)
    pad = head_dim - rot_dim
    a = np.concatenate([cos, cos, np.ones((seq, pad))], axis=1)
    bm = np.concatenate([-sin, np.zeros((seq, half + pad))], axis=1)
    bp = np.concatenate([np.zeros((seq, half)), sin, np.zeros((seq, pad))], axis=1)
    reps = LANES // head_dim
    tabs = np.stack([np.tile(t, (1, reps)) for t in (a, bm, bp)], axis=0)
    return jnp.asarray((tabs * scale).astype(np.float32))


def _rotate(acc, tab_ref, rows, half):
    a, bm, bp = tab_ref[0, rows, :], tab_ref[1, rows, :], tab_ref[2, rows, :]
    outs = []
    for c in range(acc.shape[1] // LANES):
        xs = acc[:, c * LANES:(c + 1) * LANES]
        outs.append(xs * a + pltpu.roll(xs, LANES - half, 1) * bm + pltpu.roll(xs, half, 1) * bp)
    return jnp.concatenate(outs, axis=1)


def _in_proj_kernel(x_ref, g_ref, w_ref, tmq_ref, tmk_ref, trq_ref, trk_ref, o_ref, *, sections):
    rows = x_ref.shape[0] // 2
    sl = [slice(r * rows, (r + 1) * rows) for r in range(2)]
    h = [_rms(x_ref[s, :], g_ref[...]).astype(BF16) for s in sl]
    tabs = {"mq": (tmq_ref, ROPE_DIM // 2), "mk": (tmk_ref, ROPE_DIM // 2),
            "rq": (trq_ref, RET_QK_DIM // 2), "rk": (trk_ref, RET_QK_DIM // 2)}
    for kind, c0, c1 in sections:
        for r in range(2):
            acc = _wdot(h[r], w_ref[:, c0:c1])
            if kind in tabs:
                tab_ref, half = tabs[kind]
                acc = _rotate(acc, tab_ref, sl[r], half)
            o_ref[sl[r], c0:c1] = acc.astype(BF16)


def _in_proj(x2d, g, w, tabs, seq, sections):
    t, d = x2d.shape
    n = w.shape[1]
    tm = IN_PROJ_ROWS
    tiles_per_seq = seq // tm
    tab_spec = pl.BlockSpec((3, tm, LANES), lambda i: (0, i % tiles_per_seq, 0))
    return pl.pallas_call(
        functools.partial(_in_proj_kernel, sections=sections),
        grid=(t // tm,),
        in_specs=[
            pl.BlockSpec((tm, d), lambda i: (i, 0)),
            pl.BlockSpec((1, d), lambda i: (0, 0)),
            pl.BlockSpec((d, n), lambda i: (0, 0), pipeline_mode=pl.Buffered(1)),
            tab_spec, tab_spec, tab_spec, tab_spec,
        ],
        out_specs=pl.BlockSpec((tm, n), lambda i: (i, 0)),
        out_shape=jax.ShapeDtypeStruct((t, n), BF16),
        compiler_params=_params(1),
        name="in_proj",
    )(x2d, g, w, *tabs)


def _moba_kernel(q_ref, k_ref, v_ref, *refs, n_weights):
    w_refs, o_ref, wb_refs = refs[:n_weights], refs[n_weights], refs[n_weights + 1:2 * n_weights + 1]
    qp_ref, vt_ref, masked_ref = refs[2 * n_weights + 1:]
    for w_ref, wb_ref in zip(w_refs, wb_refs):
        wb_ref[...] = w_ref[...].astype(BF16)
    seq = q_ref.shape[0]
    blk = MOBA_BLOCK
    nb = seq // blk
    hd = MOBA_HEAD_DIM
    q2 = q_ref[...]
    k2 = k_ref[...]
    lane = lax.broadcasted_iota(jnp.int32, (seq, LANES), 1)
    kmean2 = jnp.mean(k2.astype(F32).reshape(nb, blk, LANES), axis=1)

    lane8 = lax.broadcasted_iota(jnp.int32, (nb, LANES), 1)
    km = jnp.concatenate([jnp.where(lane8 < hd, kmean2, 0.0), jnp.where(lane8 >= hd, kmean2, 0.0)], axis=0)
    km_hi = km.astype(BF16)
    km_lo = (km - km_hi.astype(F32)).astype(BF16)
    gate2 = _dot_nt(jnp.concatenate([km_hi, km_lo], axis=0), q2)
    jidx = lax.broadcasted_iota(jnp.int32, (nb, seq), 0)
    qblk = lax.broadcasted_iota(jnp.int32, (nb, seq), 1) // blk
    for hh in range(2):
        gate = gate2[hh * nb:(hh + 1) * nb] + gate2[(2 + hh) * nb:(3 + hh) * nb]
        rank = jnp.zeros((nb, seq), jnp.int32)
        for i in range(nb):
            gi = gate[i:i + 1, :]
            beats = ((gi > gate) | ((gi == gate) & (i < jidx))) & (i < qblk)
            rank = rank + beats.astype(jnp.int32)
        masked_ref[hh] = ((jidx >= qblk) | (rank >= MOBA_TOPK)).astype(F32)
        mine = (lane >= hh * hd) & (lane < (hh + 1) * hd)
        qp_ref[hh] = jnp.where(mine, q2, jnp.zeros_like(q2))
    v2t = v_ref[...].astype(F32).T
    ones = jnp.ones((MOBA_ONES_ROWS, seq), F32)
    for hh in range(2):
        vt_ref[hh] = jnp.concatenate([v2t[hh * hd:(hh + 1) * hd], ones], axis=0).astype(BF16)

    kr = lax.broadcasted_iota(jnp.int32, (blk, blk), 0)
    qcol = lax.broadcasted_iota(jnp.int32, (blk, blk), 1)

    def scores(t):
        c, j = tiles[t]
        out = []
        for hh in range(2):
            s = _dot_nt(k_ref[j * blk:(j + 1) * blk, :], qp_ref[hh, c * blk:(c + 1) * blk, :])
            out.append(jnp.where(kr <= qcol, s, NEG_INF) if j == c else s)
        return out

    tiles = [(c, j) for c in range(nb) for j in range(c + 1)]
    pending = {t: scores(t) for t in range(min(MOBA_PREFETCH, len(tiles)))}
    m_run = [None, None]
    acc = [None, None]
    for t, (c, j) in enumerate(tiles):
        r0, r1 = c * blk, (c + 1) * blk
        if t + MOBA_PREFETCH < len(tiles):
            pending[t + MOBA_PREFETCH] = scores(t + MOBA_PREFETCH)
        s_cur = pending.pop(t)
        for hh in range(2):
            m_t = jnp.max(s_cur[hh], axis=0, keepdims=True)
            if j < c:
                off = masked_ref[hh, j:j + 1, r0:r1] > 0.5
                m_t = jnp.where(off, NEG_INF, m_t)
            m_new = m_t if j == 0 else jnp.maximum(m_run[hh], m_t)
            shift = jnp.where(off, -NEG_INF, m_new) if j < c else m_new
            pv = _dot(vt_ref[hh, :, j * blk:(j + 1) * blk], jnp.exp2(s_cur[hh] - shift).astype(BF16))
            acc[hh] = pv if j == 0 else acc[hh] * jnp.exp2(m_run[hh] - m_new) + pv
            m_run[hh] = m_new
        if j == c:
            res = [acc[hh][0:hd, :] * (1.0 / acc[hh][hd:hd + 1, :]) for hh in range(2)]
            o_ref[r0:r1, :] = jnp.concatenate(res, axis=0).T.astype(BF16)


def _moba(proj, batch, seq, q_col, k_col, v_col, weights):
    t = proj.shape[0]
    pairs = MOBA_HEADS // 2
    steps = batch * pairs

    def spec(col0):
        cb = col0 // LANES
        return pl.BlockSpec((seq, LANES), lambda b, p: (b, cb + p))

    def weight_spec(w):
        share = 1
        while (w.shape[0] * share) % (steps * BF16_SUBLANES):
            share *= 2
        rows = w.shape[0] * share // steps
        return pl.BlockSpec((rows, w.shape[1]), lambda b, p, share=share: ((b * pairs + p) // share, 0))

    w_specs = [weight_spec(w) for w in weights]
    outs = pl.pallas_call(
        functools.partial(_moba_kernel, n_weights=len(weights)),
        grid=(batch, pairs),
        in_specs=[spec(q_col), spec(k_col), spec(v_col)] + w_specs,
        out_specs=[pl.BlockSpec((seq, LANES), lambda b, p: (b, p))] + w_specs,
        out_shape=[jax.ShapeDtypeStruct((t, pairs * LANES), BF16)]
        + [jax.ShapeDtypeStruct(w.shape, BF16) for w in weights],
        scratch_shapes=[pltpu.VMEM((2, seq, LANES), BF16),
                        pltpu.VMEM((2, MOBA_HEAD_DIM + MOBA_ONES_ROWS, seq), BF16),
                        pltpu.VMEM((2, seq // MOBA_BLOCK, seq), F32)],
        compiler_params=_params(2),
        name="moba",
    )(proj, proj, proj, *weights)
    return outs[0], outs[1:]


_RET_LOG_G = [math.log(1.0 - 2.0 ** (-5.0 - h)) for h in range(RET_HEADS)]


def _retention_kernel(q_ref, k_ref, v_ref, g_ref, o_ref):
    seq = q_ref.shape[0]
    ch = RET_CHUNK
    dv = RET_V_DIM
    pair = pl.program_id(1)
    lane = lax.broadcasted_iota(jnp.int32, (ch, LANES), 1)
    rowf = lax.broadcasted_iota(jnp.int32, (ch, LANES), 0).astype(F32)
    ri = lax.broadcasted_iota(jnp.int32, (ch, ch), 0)
    ci = lax.broadcasted_iota(jnp.int32, (ch, ch), 1)
    diff = (ri - ci).astype(F32)
    for hh in range(2):
        log_g = jnp.where(pair == 0, _RET_LOG_G[hh], _RET_LOG_G[2 + hh]).astype(F32)
        inner_decay = jnp.where(diff >= 0, jnp.exp(log_g * jnp.maximum(diff, 0.0)), 0.0)
        q_decay = jnp.exp(log_g * (rowf + 1.0))
        k_decay = jnp.exp(log_g * (ch - 1.0 - rowf))
        chunk_decay = jnp.exp(jnp.full((LANES, dv), ch, F32) * log_g)
        mine = (lane >= hh * RET_QK_DIM) & (lane < (hh + 1) * RET_QK_DIM)
        state = jnp.zeros((LANES, dv), F32)
        for n in range(seq // ch):
            r0, r1 = n * ch, (n + 1) * ch
            qc = jnp.where(mine, q_ref[r0:r1, :], jnp.zeros((), BF16))
            kc = k_ref[r0:r1, :]
            vc = v_ref[r0:r1, hh * dv:(hh + 1) * dv]
            attn = _dot_nt(qc, kc) * inner_decay
            out = _dot(attn.astype(BF16), vc)
            out = out + _dot((qc.astype(F32) * q_decay).astype(BF16), state.astype(BF16))
            state = state * chunk_decay + _dot_tn((kc.astype(F32) * k_decay).astype(BF16), vc)
            out = out * lax.rsqrt(jnp.mean(out * out, axis=-1, keepdims=True) + NORM_EPS)
            gate = g_ref[r0:r1, hh * dv:(hh + 1) * dv].astype(F32)
            o_ref[r0:r1, hh * dv:(hh + 1) * dv] = (gate * jax.nn.sigmoid(gate) * out).astype(BF16)


def _retention(proj, batch, seq, q_col, k_col, v_col, g_col):
    t = proj.shape[0]
    pairs = RET_HEADS // 2
    wv = 2 * RET_V_DIM

    def spec(col0, width):
        cb = col0 // width
        return pl.BlockSpec((seq, width), lambda b, p: (b, cb + p))

    return pl.pallas_call(
        _retention_kernel,
        grid=(batch, pairs),
        in_specs=[spec(q_col, LANES), spec(k_col, LANES), spec(v_col, wv), spec(g_col, wv)],
        out_specs=pl.BlockSpec((seq, wv), lambda b, p: (b, p)),
        out_shape=jax.ShapeDtypeStruct((t, pairs * wv), BF16),
        compiler_params=_params(2),
        name="retention",
    )(proj, proj, proj, proj)


def _mem_kv_kernel(m_ref, g_ref, w_ref, o_ref):
    o_ref[...] = _wdot(_rms(m_ref[...], g_ref[...]).astype(BF16), w_ref[...]).astype(BF16)


def _mem_kv(mem2d, g, w, rows):
    t, d = mem2d.shape
    n = w.shape[1]
    return pl.pallas_call(
        _mem_kv_kernel,
        grid=(t // rows,),
        in_specs=[pl.BlockSpec((rows, d), lambda i: (i, 0)),
                  pl.BlockSpec((1, d), lambda i: (0, 0)),
                  pl.BlockSpec((d, n), lambda i: (0, 0), pipeline_mode=pl.Buffered(1))],
        out_specs=pl.BlockSpec((rows, n), lambda i: (i, 0)),
        out_shape=jax.ShapeDtypeStruct((t, n), BF16),
        compiler_params=_params(1),
        name="mem_kv",
    )(mem2d, g, w)


def _mix_cross_kernel(mo_ref, ro_ref, x_ref, kv_ref, wo_ref, wq_ref, wc_ref,
                      g_mix_ref, g_pre_ref, g_post_ref, o_ref):
    d = x_ref.shape[1]
    half = mo_ref.shape[1]
    dc = d // CROSS_HEADS
    rows = x_ref.shape[0] // MIX_PARTS
    sl = [slice(r * rows, (r + 1) * rows) for r in range(MIX_PARTS)]
    mix = [_wdot(mo_ref[s, :], wo_ref[0:half, :]) + _wdot(ro_ref[s, :], wo_ref[half:, :]) for s in sl]
    x1 = [x_ref[s, :] + _rms(m, g_mix_ref[...]) for s, m in zip(sl, mix)]
    h = [_rms(x, g_pre_ref[...], scale=dc ** -0.5).astype(BF16) for x in x1]
    cq = [_wdot(hh, wq_ref[...]).astype(BF16) for hh in h]
    att = []
    for r in range(MIX_PARTS):
        heads = []
        for hc in range(CROSS_HEADS):
            c0, c1 = hc * dc, (hc + 1) * dc
            s = _dot_nt(cq[r][:, c0:c1], kv_ref[:, c0:c1])
            e = jnp.exp(s - jnp.max(s, axis=1, keepdims=True))
            o = _dot(e.astype(BF16), kv_ref[:, d + c0:d + c1])
            heads.append((o * (1.0 / jnp.sum(e, axis=1, keepdims=True))).astype(BF16))
        att.append(jnp.concatenate(heads, axis=1))
    c = [_wdot(a, wc_ref[...]) for a in att]
    for r in range(MIX_PARTS):
        o_ref[sl[r], :] = x1[r] + _rms(c[r], g_post_ref[...])


def _mix_cross(mo, ro, x2d, kv, w_out, w_cq, w_co, g_mix, g_pre, g_post, seq, n_mem):
    t, d = x2d.shape
    tm = MIX_ROWS
    tiles_per_seq = seq // tm
    half = mo.shape[1]
    full = lambda i: (0, 0)
    weight = pl.BlockSpec((d, d), full, pipeline_mode=pl.Buffered(1))
    return pl.pallas_call(
        _mix_cross_kernel,
        grid=(t // tm,),
        in_specs=[
            pl.BlockSpec((tm, half), lambda i: (i, 0)),
            pl.BlockSpec((tm, half), lambda i: (i, 0)),
            pl.BlockSpec((tm, d), lambda i: (i, 0)),
            pl.BlockSpec((n_mem, 2 * d), lambda i: (i // tiles_per_seq, 0)),
            weight, weight, weight,
            pl.BlockSpec((1, d), full), pl.BlockSpec((1, d), full), pl.BlockSpec((1, d), full),
        ],
        out_specs=pl.BlockSpec((tm, d), lambda i: (i, 0)),
        out_shape=jax.ShapeDtypeStruct((t, d), F32),
        compiler_params=_params(1),
        name="mix_cross",
    )(mo, ro, x2d, kv, w_out, w_cq, w_co, g_mix, g_pre, g_post)


def _ffn_kernel(x_ref, wgu_ref, wd_ref, g_pre_ref, g_post_ref, o_ref, *, d_ff, bounds):
    rows = x_ref.shape[0] // 2
    xs = [x_ref[r * rows:(r + 1) * rows, :] for r in range(2)]
    hs = [_rms(x, g_pre_ref[...]).astype(BF16) for x in xs]
    fs = [None, None]
    for c0, c1 in bounds:
        for r in range(2):
            gate = _wdot(hs[r], wgu_ref[:, c0:c1])
            up = _wdot(hs[r], wgu_ref[:, d_ff + c0:d_ff + c1])
            act = (gate * jax.nn.sigmoid(gate) * up).astype(BF16)
            down = _wdot(act, wd_ref[c0:c1, :])
            fs[r] = down if fs[r] is None else fs[r] + down
    for r in range(2):
        o_ref[r * rows:(r + 1) * rows, :] = xs[r] + _rms(fs[r], g_post_ref[...])


def _ffn(x2d, w_gate_up, w_down, g_pre, g_post):
    t, d = x2d.shape
    d_ff = w_down.shape[0]
    tm = FFN_ROWS
    assert d_ff % MXU_DIM == 0
    mid = (d_ff // MXU_DIM // 2) * MXU_DIM
    bounds = ((0, mid), (mid, d_ff))
    full = lambda i: (0, 0)
    return pl.pallas_call(
        functools.partial(_ffn_kernel, d_ff=d_ff, bounds=bounds),
        grid=(t // tm,),
        in_specs=[
            pl.BlockSpec((tm, d), lambda i: (i, 0)),
            pl.BlockSpec((d, 2 * d_ff), full, pipeline_mode=pl.Buffered(1)),
            pl.BlockSpec((d_ff, d), full, pipeline_mode=pl.Buffered(1)),
            pl.BlockSpec((1, d), full), pl.BlockSpec((1, d), full),
        ],
        out_specs=pl.BlockSpec((tm, d), lambda i: (i, 0)),
        out_shape=jax.ShapeDtypeStruct((t, d), F32),
        compiler_params=_params(1),
        name="ffn",
    )(x2d, w_gate_up, w_down, g_pre, g_post)


def kernel(x, mem, g_pre_mix, w_in, w_out, g_post_mix, g_pre_cross, g_mem, w_cq, w_ckv, w_co,
           g_post_cross, g_pre_ffn, w_gate_up, w_down, g_post_ffn):
    batch, seq, d = x.shape
    n_mem = mem.shape[1]
    depth = w_in.shape[0]
    moba_w = MOBA_HEADS * MOBA_HEAD_DIM
    ret_qk_w = RET_HEADS * RET_QK_DIM
    ret_v_w = RET_HEADS * RET_V_DIM
    c_mq, c_mk, c_mv = 0, moba_w, 2 * moba_w
    c_rq = 3 * moba_w
    c_rk = c_rq + ret_qk_w
    c_rv = c_rk + ret_qk_w
    c_rg = c_rv + ret_v_w
    n_proj = c_rg + ret_v_w
    assert w_in.shape[2] == n_proj
    sections = (("mq", c_mq, c_mk), ("mk", c_mk, c_mv), ("mv", c_mv, c_rq), ("rq", c_rq, c_rk),
                ("rk", c_rk, c_rv), ("rv", c_rv, c_rg), ("rg", c_rg, n_proj))

    moba_inv = np.power(ROPE_THETA, -np.arange(ROPE_DIM // 2, dtype=np.float64) * 2.0 / ROPE_DIM)
    ret_inv = 1.0 / np.power(RET_THETA, np.linspace(0.0, 1.0, RET_QK_DIM // 2))
    tabs = (
        _rotary_tables(seq, moba_inv, ROPE_DIM, MOBA_HEAD_DIM, MOBA_HEAD_DIM ** -0.5 * math.log2(math.e)),
        _rotary_tables(seq, moba_inv, ROPE_DIM, MOBA_HEAD_DIM, 1.0),
        _rotary_tables(seq, ret_inv, RET_QK_DIM, RET_QK_DIM, 1.0),
        _rotary_tables(seq, ret_inv, RET_QK_DIM, RET_QK_DIM, RET_QK_DIM ** -0.5),
    )

    xf = x.reshape(batch * seq, d)
    mem2d = mem.reshape(batch * n_mem, d)
    row = lambda g: g.reshape(1, d)
    for l in range(depth):
        proj = _in_proj(xf, row(g_pre_mix[l]), w_in[l], tabs, seq, sections)
        mo, (b_ckv, b_out, b_cq, b_co, b_gate_up, b_down) = _moba(
            proj, batch, seq, c_mq, c_mk, c_mv,
            (w_ckv[l], w_out[l], w_cq[l], w_co[l], w_gate_up[l], w_down[l]))
        ro = _retention(proj, batch, seq, c_rq, c_rk, c_rv, c_rg)
        kv = _mem_kv(mem2d, row(g_mem[l]), b_ckv, MEM_KV_ROWS)
        x2 = _mix_cross(mo, ro, xf, kv, b_out, b_cq, b_co, row(g_post_mix[l]),
                        row(g_pre_cross[l]), row(g_post_cross[l]), seq, n_mem)
        xf = _ffn(x2, b_gate_up, b_down, row(g_pre_ffn[l]), row(g_post_ffn[l]))
    return xf.reshape(batch, seq, d)
```

```python
import functools
import math

import jax
import jax.numpy as jnp
import numpy as np
from jax import lax
from jax.experimental import pallas as pl
from jax.experimental.pallas import tpu as pltpu

F32 = jnp.float32
BF16 = jnp.bfloat16

NORM_EPS = 1e-6
NEG_INF = -1e30

LANES = 128
BF16_SUBLANES = 16
MXU_DIM = 256
VMEM_LIMIT_BYTES = 56 * 1024 * 1024

MOBA_HEAD_DIM = 64
MOBA_HEADS = 8
MOBA_BLOCK = 256
MOBA_TOPK = 3
MOBA_ONES_ROWS = 16
MOBA_PREFETCH = 2
MOBA_PAIRS_PER_STEP = 2
ROPE_THETA = 500000.0
ROPE_DIM = MOBA_HEAD_DIM // 4

RET_HEADS = 4
RET_QK_DIM = 64
RET_V_DIM = 128
RET_THETA = 10000.0
RET_CHUNK = 256

CROSS_HEADS = 4

IN_PROJ_ROWS = 1024
MEM_KV_ROWS = 1024
MIX_ROWS = 1024
MIX_PARTS = 2
FFN_ROWS = 1024


def _dot(a, b):
    return jnp.dot(a, b, preferred_element_type=F32)


def _wdot(a, w):
    return jnp.dot(a, w.astype(BF16), preferred_element_type=F32)


def _dot_nt(a, b):
    return lax.dot_general(a, b, (((1,), (1,)), ((), ())), preferred_element_type=F32)


def _dot_tn(a, b):
    return lax.dot_general(a, b, (((0,), (0,)), ((), ())), preferred_element_type=F32)


def _rms(x, g, scale=None):
    inv = lax.rsqrt(jnp.mean(x * x, axis=-1, keepdims=True) + NORM_EPS)
    if scale is not None:
        inv = inv * scale
    return x * inv * g


def _params(n_grid_dims):
    return pltpu.CompilerParams(
        dimension_semantics=("arbitrary",) * n_grid_dims,
        vmem_limit_bytes=VMEM_LIMIT_BYTES,
    )


def _rotary_tables(seq, inv_freq, rot_dim, head_dim, scale):
    half = rot_dim // 2
    ang = np.arange(seq, dtype=np.float64)[:, None] * inv_freq[None, :]
    cos, sin = np.cos(ang), np.sin(ang)
    pad = head_dim - rot_dim
    a = np.concatenate([cos, cos, np.ones((seq, pad))], axis=1)
    bm = np.concatenate([-sin, np.zeros((seq, half + pad))], axis=1)
    bp = np.concatenate([np.zeros((seq, half)), sin, np.zeros((seq, pad))], axis=1)
    reps = LANES // head_dim
    tabs = np.stack([np.tile(t, (1, reps)) for t in (a, bm, bp)], axis=0)
    return jnp.asarray((tabs * scale).astype(np.float32))


def _rotate(acc, tab_ref, rows, half):
    a, bm, bp = tab_ref[0, rows, :], tab_ref[1, rows, :], tab_ref[2, rows, :]
    outs = []
    for c in range(acc.shape[1] // LANES):
        xs = acc[:, c * LANES:(c + 1) * LANES]
        outs.append(xs * a + pltpu.roll(xs, LANES - half, 1) * bm + pltpu.roll(xs, half, 1) * bp)
    return jnp.concatenate(outs, axis=1)


def _in_proj_kernel(x_ref, g_ref, w_ref, tmq_ref, tmk_ref, trq_ref, trk_ref, o_ref, *, sections):
    rows = x_ref.shape[0] // 2
    sl = [slice(r * rows, (r + 1) * rows) for r in range(2)]
    h = [_rms(x_ref[s, :], g_ref[...]).astype(BF16) for s in sl]
    tabs = {"mq": (tmq_ref, ROPE_DIM // 2), "mk": (tmk_ref, ROPE_DIM // 2),
            "rq": (trq_ref, RET_QK_DIM // 2), "rk": (trk_ref, RET_QK_DIM // 2)}
    for kind, c0, c1 in sections:
        for r in range(2):
            acc = _wdot(h[r], w_ref[:, c0:c1])
            if kind in tabs:
                tab_ref, half = tabs[kind]
                acc = _rotate(acc, tab_ref, sl[r], half)
            o_ref[sl[r], c0:c1] = acc.astype(BF16)


def _in_proj(x2d, g, w, tabs, seq, sections):
    t, d = x2d.shape
    n = w.shape[1]
    tm = IN_PROJ_ROWS
    tiles_per_seq = seq // tm
    tab_spec = pl.BlockSpec((3, tm, LANES), lambda i: (0, i % tiles_per_seq, 0))
    return pl.pallas_call(
        functools.partial(_in_proj_kernel, sections=sections),
        grid=(t // tm,),
        in_specs=[
            pl.BlockSpec((tm, d), lambda i: (i, 0)),
            pl.BlockSpec((1, d), lambda i: (0, 0)),
            pl.BlockSpec((d, n), lambda i: (0, 0), pipeline_mode=pl.Buffered(1)),
            tab_spec, tab_spec, tab_spec, tab_spec,
        ],
        out_specs=pl.BlockSpec((tm, n), lambda i: (i, 0)),
        out_shape=jax.ShapeDtypeStruct((t, n), BF16),
        compiler_params=_params(1),
        name="in_proj",
    )(x2d, g, w, *tabs)


def _moba_kernel(q_ref, k_ref, v_ref, *refs, n_weights):
    w_refs, o_ref, wb_refs = refs[:n_weights], refs[n_weights], refs[n_weights + 1:2 * n_weights + 1]
    qp_ref, vt_ref, masked_ref = refs[2 * n_weights + 1:]
    for w_ref, wb_ref in zip(w_refs, wb_refs):
        wb_ref[...] = w_ref[...].astype(BF16)
    seq = q_ref.shape[0]
    blk = MOBA_BLOCK
    nb = seq // blk
    hd = MOBA_HEAD_DIM
    lane = lax.broadcasted_iota(jnp.int32, (seq, LANES), 1)
    lane8 = lax.broadcasted_iota(jnp.int32, (nb, LANES), 1)
    jidx = lax.broadcasted_iota(jnp.int32, (nb, seq), 0)
    qblk = lax.broadcasted_iota(jnp.int32, (nb, seq), 1) // blk
    ones = jnp.ones((MOBA_ONES_ROWS, seq), F32)
    for pp in range(MOBA_PAIRS_PER_STEP):
        ln = slice(pp * LANES, (pp + 1) * LANES)
        q2 = q_ref[:, ln]
        kmean2 = jnp.mean(k_ref[:, ln].astype(F32).reshape(nb, blk, LANES), axis=1)
        km = jnp.concatenate([jnp.where(lane8 < hd, kmean2, 0.0), jnp.where(lane8 >= hd, kmean2, 0.0)], axis=0)
        km_hi = km.astype(BF16)
        km_lo = (km - km_hi.astype(F32)).astype(BF16)
        gate2 = _dot_nt(jnp.concatenate([km_hi, km_lo], axis=0), q2)
        v2t = v_ref[:, ln].astype(F32).T
        for hh in range(2):
            h = 2 * pp + hh
            gate = gate2[hh * nb:(hh + 1) * nb] + gate2[(2 + hh) * nb:(3 + hh) * nb]
            rank = jnp.zeros((nb, seq), jnp.int32)
            for i in range(nb):
                gi = gate[i:i + 1, :]
                beats = ((gi > gate) | ((gi == gate) & (i < jidx))) & (i < qblk)
                rank = rank + beats.astype(jnp.int32)
            masked_ref[h] = ((jidx >= qblk) | (rank >= MOBA_TOPK)).astype(F32)
            mine = (lane >= hh * hd) & (lane < (hh + 1) * hd)
            qp_ref[h] = jnp.where(mine, q2, jnp.zeros_like(q2))
            vt_ref[h] = jnp.concatenate([v2t[hh * hd:(hh + 1) * hd], ones], axis=0).astype(BF16)

    kr = lax.broadcasted_iota(jnp.int32, (blk, blk), 0)
    qcol = lax.broadcasted_iota(jnp.int32, (blk, blk), 1)

    def scores(t):
        pp, c, j = tiles[t]
        out = []
        for hh in range(2):
            s = _dot_nt(k_ref[j * blk:(j + 1) * blk, pp * LANES:(pp + 1) * LANES],
                        qp_ref[2 * pp + hh, c * blk:(c + 1) * blk, :])
            out.append(jnp.where(kr <= qcol, s, NEG_INF) if j == c else s)
        return out

    tiles = [(pp, c, j) for pp in range(MOBA_PAIRS_PER_STEP) for c in range(nb) for j in range(c + 1)]
    pending = {t: scores(t) for t in range(min(MOBA_PREFETCH, len(tiles)))}
    m_run = [None, None]
    acc = [None, None]
    for t, (pp, c, j) in enumerate(tiles):
        r0, r1 = c * blk, (c + 1) * blk
        if t + MOBA_PREFETCH < len(tiles):
            pending[t + MOBA_PREFETCH] = scores(t + MOBA_PREFETCH)
        s_cur = pending.pop(t)
        for hh in range(2):
            h = 2 * pp + hh
            m_t = jnp.max(s_cur[hh], axis=0, keepdims=True)
            if j < c:
                off = masked_ref[h, j:j + 1, r0:r1] > 0.5
                m_t = jnp.where(off, NEG_INF, m_t)
            m_new = m_t if j == 0 else jnp.maximum(m_run[hh], m_t)
            shift = jnp.where(off, -NEG_INF, m_new) if j < c else m_new
            pv = _dot(vt_ref[h, :, j * blk:(j + 1) * blk], jnp.exp2(s_cur[hh] - shift).astype(BF16))
            acc[hh] = pv if j == 0 else acc[hh] * jnp.exp2(m_run[hh] - m_new) + pv
            m_run[hh] = m_new
        if j == c:
            res = [acc[hh][0:hd, :] * (1.0 / acc[hh][hd:hd + 1, :]) for hh in range(2)]
            o_ref[r0:r1, pp * LANES:(pp + 1) * LANES] = jnp.concatenate(res, axis=0).T.astype(BF16)


def _moba(proj, batch, seq, q_col, k_col, v_col, weights):
    t = proj.shape[0]
    width = MOBA_PAIRS_PER_STEP * LANES
    pairs = MOBA_HEADS * MOBA_HEAD_DIM // width
    n_heads = 2 * MOBA_PAIRS_PER_STEP
    steps = batch * pairs

    def spec(col0):
        assert col0 % width == 0
        return pl.BlockSpec((seq, width), lambda b, p: (b, col0 // width + p))

    def weight_spec(w):
        share = 1
        while (w.shape[0] * share) % (steps * BF16_SUBLANES):
            share *= 2
        rows = w.shape[0] * share // steps
        return pl.BlockSpec((rows, w.shape[1]), lambda b, p, share=share: ((b * pairs + p) // share, 0))

    w_specs = [weight_spec(w) for w in weights]
    outs = pl.pallas_call(
        functools.partial(_moba_kernel, n_weights=len(weights)),
        grid=(batch, pairs),
        in_specs=[spec(q_col), spec(k_col), spec(v_col)] + w_specs,
        out_specs=[pl.BlockSpec((seq, width), lambda b, p: (b, p))] + w_specs,
        out_shape=[jax.ShapeDtypeStruct((t, pairs * width), BF16)]
        + [jax.ShapeDtypeStruct(w.shape, BF16) for w in weights],
        scratch_shapes=[pltpu.VMEM((n_heads, seq, LANES), BF16),
                        pltpu.VMEM((n_heads, MOBA_HEAD_DIM + MOBA_ONES_ROWS, seq), BF16),
                        pltpu.VMEM((n_heads, seq // MOBA_BLOCK, seq), F32)],
        compiler_params=_params(2),
        name="moba",
    )(proj, proj, proj, *weights)
    return outs[0], outs[1:]


_RET_LOG_G = [math.log(1.0 - 2.0 ** (-5.0 - h)) for h in range(RET_HEADS)]


def _retention_kernel(q_ref, k_ref, v_ref, g_ref, o_ref):
    seq = q_ref.shape[0]
    ch = RET_CHUNK
    dv = RET_V_DIM
    lane = lax.broadcasted_iota(jnp.int32, (ch, LANES), 1)
    rowf = lax.broadcasted_iota(jnp.int32, (ch, LANES), 0).astype(F32)
    ri = lax.broadcasted_iota(jnp.int32, (ch, ch), 0)
    ci = lax.broadcasted_iota(jnp.int32, (ch, ch), 1)
    diff = (ri - ci).astype(F32)
    heads = []
    for h in range(RET_HEADS):
        log_g = _RET_LOG_G[h]
        hh = h % 2
        heads.append(dict(
            inner_decay=jnp.where(diff >= 0, jnp.exp(log_g * jnp.maximum(diff, 0.0)), 0.0),
            q_decay=jnp.exp(log_g * (rowf + 1.0)),
            k_decay=jnp.exp(log_g * (ch - 1.0 - rowf)),
            chunk_decay=math.exp(log_g * ch),
            mine=(lane >= hh * RET_QK_DIM) & (lane < (hh + 1) * RET_QK_DIM),
            qk=slice((h // 2) * LANES, (h // 2 + 1) * LANES),
            v=slice(h * dv, (h + 1) * dv),
            state=jnp.zeros((LANES, dv), F32),
        ))
    for n in range(seq // ch):
        r0, r1 = n * ch, (n + 1) * ch
        for hd in heads:
            qc = jnp.where(hd["mine"], q_ref[r0:r1, hd["qk"]], jnp.zeros((), BF16))
            kc = k_ref[r0:r1, hd["qk"]]
            vc = v_ref[r0:r1, hd["v"]]
            attn = _dot_nt(qc, kc) * hd["inner_decay"]
            out = _dot(attn.astype(BF16), vc)
            out = out + _dot((qc.astype(F32) * hd["q_decay"]).astype(BF16), hd["state"].astype(BF16))
            hd["state"] = (hd["state"] * hd["chunk_decay"]
                           + _dot_tn((kc.astype(F32) * hd["k_decay"]).astype(BF16), vc))
            out = out * lax.rsqrt(jnp.mean(out * out, axis=-1, keepdims=True) + NORM_EPS)
            gate = g_ref[r0:r1, hd["v"]].astype(F32)
            o_ref[r0:r1, hd["v"]] = (gate * jax.nn.sigmoid(gate) * out).astype(BF16)


def _retention(proj, batch, seq, q_col, k_col, v_col, g_col):
    t = proj.shape[0]
    wqk = RET_HEADS * RET_QK_DIM
    wv = RET_HEADS * RET_V_DIM

    def spec(col0, width):
        assert col0 % width == 0
        return pl.BlockSpec((seq, width), lambda b: (b, col0 // width))

    return pl.pallas_call(
        _retention_kernel,
        grid=(batch,),
        in_specs=[spec(q_col, wqk), spec(k_col, wqk), spec(v_col, wv), spec(g_col, wv)],
        out_specs=pl.BlockSpec((seq, wv), lambda b: (b, 0)),
        out_shape=jax.ShapeDtypeStruct((t, wv), BF16),
        compiler_params=_params(1),
        name="retention",
    )(proj, proj, proj, proj)


def _mem_kv_kernel(m_ref, g_ref, w_ref, o_ref):
    o_ref[...] = _wdot(_rms(m_ref[...], g_ref[...]).astype(BF16), w_ref[...]).astype(BF16)


def _mem_kv(mem2d, g, w, rows):
    t, d = mem2d.shape
    n = w.shape[1]
    return pl.pallas_call(
        _mem_kv_kernel,
        grid=(t // rows,),
        in_specs=[pl.BlockSpec((rows, d), lambda i: (i, 0)),
                  pl.BlockSpec((1, d), lambda i: (0, 0)),
                  pl.BlockSpec((d, n), lambda i: (0, 0), pipeline_mode=pl.Buffered(1))],
        out_specs=pl.BlockSpec((rows, n), lambda i: (i, 0)),
        out_shape=jax.ShapeDtypeStruct((t, n), BF16),
        compiler_params=_params(1),
        name="mem_kv",
    )(mem2d, g, w)


def _mix_cross_kernel(mo_ref, ro_ref, x_ref, kv_ref, wo_ref, wq_ref, wc_ref,
                      g_mix_ref, g_pre_ref, g_post_ref, o_ref):
    d = x_ref.shape[1]
    half = mo_ref.shape[1]
    dc = d // CROSS_HEADS
    rows = x_ref.shape[0] // MIX_PARTS
    sl = [slice(r * rows, (r + 1) * rows) for r in range(MIX_PARTS)]
    mix = [_wdot(mo_ref[s, :], wo_ref[0:half, :]) + _wdot(ro_ref[s, :], wo_ref[half:, :]) for s in sl]
    x1 = [x_ref[s, :] + _rms(m, g_mix_ref[...]) for s, m in zip(sl, mix)]
    h = [_rms(x, g_pre_ref[...], scale=dc ** -0.5).astype(BF16) for x in x1]
    cq = [_wdot(hh, wq_ref[...]).astype(BF16) for hh in h]
    att = []
    for r in range(MIX_PARTS):
        heads = []
        for hc in range(CROSS_HEADS):
            c0, c1 = hc * dc, (hc + 1) * dc
            s = _dot_nt(cq[r][:, c0:c1], kv_ref[:, c0:c1])
            e = jnp.exp(s - jnp.max(s, axis=1, keepdims=True))
            o = _dot(e.astype(BF16), kv_ref[:, d + c0:d + c1])
            heads.append((o * (1.0 / jnp.sum(e, axis=1, keepdims=True))).astype(BF16))
        att.append(jnp.concatenate(heads, axis=1))
    c = [_wdot(a, wc_ref[...]) for a in att]
    for r in range(MIX_PARTS):
        o_ref[sl[r], :] = x1[r] + _rms(c[r], g_post_ref[...])


def _mix_cross(mo, ro, x2d, kv, w_out, w_cq, w_co, g_mix, g_pre, g_post, seq, n_mem):
    t, d = x2d.shape
    tm = MIX_ROWS
    tiles_per_seq = seq // tm
    half = mo.shape[1]
    full = lambda i: (0, 0)
    weight = pl.BlockSpec((d, d), full, pipeline_mode=pl.Buffered(1))
    return pl.pallas_call(
        _mix_cross_kernel,
        grid=(t // tm,),
        in_specs=[
            pl.BlockSpec((tm, half), lambda i: (i, 0)),
            pl.BlockSpec((tm, half), lambda i: (i, 0)),
            pl.BlockSpec((tm, d), lambda i: (i, 0)),
            pl.BlockSpec((n_mem, 2 * d), lambda i: (i // tiles_per_seq, 0)),
            weight, weight, weight,
            pl.BlockSpec((1, d), full), pl.BlockSpec((1, d), full), pl.BlockSpec((1, d), full),
        ],
        out_specs=pl.BlockSpec((tm, d), lambda i: (i, 0)),
        out_shape=jax.ShapeDtypeStruct((t, d), F32),
        compiler_params=_params(1),
        name="mix_cross",
    )(mo, ro, x2d, kv, w_out, w_cq, w_co, g_mix, g_pre, g_post)


def _ffn_kernel(x_ref, wgu_ref, wd_ref, g_pre_ref, g_post_ref, o_ref, *, d_ff, bounds):
    rows = x_ref.shape[0] // 2
    xs = [x_ref[r * rows:(r + 1) * rows, :] for r in range(2)]
    hs = [_rms(x, g_pre_ref[...]).astype(BF16) for x in xs]
    fs = [None, None]
    for c0, c1 in bounds:
        for r in range(2):
            gate = _wdot(hs[r], wgu_ref[:, c0:c1])
            up = _wdot(hs[r], wgu_ref[:, d_ff + c0:d_ff + c1])
            act = (gate * jax.nn.sigmoid(gate) * up).astype(BF16)
            down = _wdot(act, wd_ref[c0:c1, :])
            fs[r] = down if fs[r] is None else fs[r] + down
    for r in range(2):
        o_ref[r * rows:(r + 1) * rows, :] = xs[r] + _rms(fs[r], g_post_ref[...])


def _ffn(x2d, w_gate_up, w_down, g_pre, g_post):
    t, d = x2d.shape
    d_ff = w_down.shape[0]
    tm = FFN_ROWS
    assert d_ff % MXU_DIM == 0
    mid = (d_ff // MXU_DIM // 2) * MXU_DIM
    bounds = ((0, mid), (mid, d_ff))
    full = lambda i: (0, 0)
    return pl.pallas_call(
        functools.partial(_ffn_kernel, d_ff=d_ff, bounds=bounds),
        grid=(t // tm,),
        in_specs=[
            pl.BlockSpec((tm, d), lambda i: (i, 0)),
            pl.BlockSpec((d, 2 * d_ff), full, pipeline_mode=pl.Buffered(1)),
            pl.BlockSpec((d_ff, d), full, pipeline_mode=pl.Buffered(1)),
            pl.BlockSpec((1, d), full), pl.BlockSpec((1, d), full),
        ],
        out_specs=pl.BlockSpec((tm, d), lambda i: (i, 0)),
        out_shape=jax.ShapeDtypeStruct((t, d), F32),
        compiler_params=_params(1),
        name="ffn",
    )(x2d, w_gate_up, w_down, g_pre, g_post)


def kernel(x, mem, g_pre_mix, w_in, w_out, g_post_mix, g_pre_cross, g_mem, w_cq, w_ckv, w_co,
           g_post_cross, g_pre_ffn, w_gate_up, w_down, g_post_ffn):
    batch, seq, d = x.shape
    n_mem = mem.shape[1]
    depth = w_in.shape[0]
    moba_w = MOBA_HEADS * MOBA_HEAD_DIM
    ret_qk_w = RET_HEADS * RET_QK_DIM
    ret_v_w = RET_HEADS * RET_V_DIM
    c_mq, c_mk, c_mv = 0, moba_w, 2 * moba_w
    c_rq = 3 * moba_w
    c_rk = c_rq + ret_qk_w
    c_rv = c_rk + ret_qk_w
    c_rg = c_rv + ret_v_w
    n_proj = c_rg + ret_v_w
    assert w_in.shape[2] == n_proj
    sections = (("mq", c_mq, c_mk), ("mk", c_mk, c_mv), ("mv", c_mv, c_rq), ("rq", c_rq, c_rk),
                ("rk", c_rk, c_rv), ("rv", c_rv, c_rg), ("rg", c_rg, n_proj))

    moba_inv = np.power(ROPE_THETA, -np.arange(ROPE_DIM // 2, dtype=np.float64) * 2.0 / ROPE_DIM)
    ret_inv = 1.0 / np.power(RET_THETA, np.linspace(0.0, 1.0, RET_QK_DIM // 2))
    tabs = (
        _rotary_tables(seq, moba_inv, ROPE_DIM, MOBA_HEAD_DIM, MOBA_HEAD_DIM ** -0.5 * math.log2(math.e)),
        _rotary_tables(seq, moba_inv, ROPE_DIM, MOBA_HEAD_DIM, 1.0),
        _rotary_tables(seq, ret_inv, RET_QK_DIM, RET_QK_DIM, 1.0),
        _rotary_tables(seq, ret_inv, RET_QK_DIM, RET_QK_DIM, RET_QK_DIM ** -0.5),
    )

    xf = x.reshape(batch * seq, d)
    mem2d = mem.reshape(batch * n_mem, d)
    row = lambda g: g.reshape(1, d)
    for l in range(depth):
        proj = _in_proj(xf, row(g_pre_mix[l]), w_in[l], tabs, seq, sections)
        mo, (b_ckv, b_out, b_cq, b_co, b_gate_up, b_down) = _moba(
            proj, batch, seq, c_mq, c_mk, c_mv,
            (w_ckv[l], w_out[l], w_cq[l], w_co[l], w_gate_up[l], w_down[l]))
        ro = _retention(proj, batch, seq, c_rq, c_rk, c_rv, c_rg)
        kv = _mem_kv(mem2d, row(g_mem[l]), b_ckv, MEM_KV_ROWS)
        x2 = _mix_cross(mo, ro, xf, kv, b_out, b_cq, b_co, row(g_post_mix[l]),
                        row(g_pre_cross[l]), row(g_post_cross[l]), seq, n_mem)
        xf = _ffn(x2, b_gate_up, b_down, row(g_pre_ffn[l]), row(g_post_ffn[l]))
    return xf.reshape(batch, seq, d)
```

```python
import functools
import math

import jax
import jax.numpy as jnp
import numpy as np
from jax import lax
from jax.experimental import pallas as pl
from jax.experimental.pallas import tpu as pltpu

F32 = jnp.float32
BF16 = jnp.bfloat16

NORM_EPS = 1e-6
NEG_INF = -1e30

LANES = 128
BF16_SUBLANES = 16
MXU_DIM = 256
VMEM_LIMIT_BYTES = 56 * 1024 * 1024

MOBA_HEAD_DIM = 64
MOBA_HEADS = 8
MOBA_BLOCK = 256
MOBA_TOPK = 3
MOBA_ONES_ROWS = 16
MOBA_PREFETCH = 2
MOBA_PAIRS_PER_STEP = 2
ROPE_THETA = 500000.0
ROPE_DIM = MOBA_HEAD_DIM // 4

RET_HEADS = 4
RET_QK_DIM = 64
RET_V_DIM = 128
RET_THETA = 10000.0
RET_CHUNK = 256

CROSS_HEADS = 4

IN_PROJ_ROWS = 1024
IN_PROJ_PARTS = 2
MEM_KV_ROWS = 1024
MIX_ROWS = 1024
MIX_PARTS = 2
FFN_ROWS = 1024
FFN_PARTS = 4
FFN_CHUNK = 1536


def _dot(a, b):
    return jnp.dot(a, b, preferred_element_type=F32)


def _wdot(a, w):
    return jnp.dot(a, w.astype(BF16), preferred_element_type=F32)


def _dot_nt(a, b):
    return lax.dot_general(a, b, (((1,), (1,)), ((), ())), preferred_element_type=F32)


def _dot_tn(a, b):
    return lax.dot_general(a, b, (((0,), (0,)), ((), ())), preferred_element_type=F32)


def _rms(x, g, scale=None):
    inv = lax.rsqrt(jnp.mean(x * x, axis=-1, keepdims=True) + NORM_EPS)
    if scale is not None:
        inv = inv * scale
    return x * inv * g


def _params(n_grid_dims):
    return pltpu.CompilerParams(
        dimension_semantics=("arbitrary",) * n_grid_dims,
        vmem_limit_bytes=VMEM_LIMIT_BYTES,
    )


def _rotary_tables(seq, inv_freq, rot_dim, head_dim, scale):
    half = rot_dim // 2
    ang = np.arange(seq, dtype=np.float64)[:, None] * inv_freq[None, :]
    cos, sin = np.cos(ang), np.sin(ang)
    pad = head_dim - rot_dim
    a = np.concatenate([cos, cos, np.ones((seq, pad))], axis=1)
    bm = np.concatenate([-sin, np.zeros((seq, half + pad))], axis=1)
    bp = np.concatenate([np.zeros((seq, half)), sin, np.zeros((seq, pad))], axis=1)
    reps = LANES // head_dim
    tabs = np.stack([np.tile(t, (1, reps)) for t in (a, bm, bp)], axis=0)
    return jnp.asarray((tabs * scale).astype(np.float32))


def _rotate(acc, tab_ref, rows, half):
    a, bm, bp = tab_ref[0, rows, :], tab_ref[1, rows, :], tab_ref[2, rows, :]
    outs = []
    for c in range(acc.shape[1] // LANES):
        xs = acc[:, c * LANES:(c + 1) * LANES]
        outs.append(xs * a + pltpu.roll(xs, LANES - half, 1) * bm + pltpu.roll(xs, half, 1) * bp)
    return jnp.concatenate(outs, axis=1)


def _in_proj_kernel(x_ref, g_ref, w_ref, tmq_ref, tmk_ref, trq_ref, trk_ref, o_ref, *, sections):
    rows = x_ref.shape[0] // IN_PROJ_PARTS
    sl = [slice(r * rows, (r + 1) * rows) for r in range(IN_PROJ_PARTS)]
    h = [_rms(x_ref[s, :], g_ref[...]).astype(BF16) for s in sl]
    tabs = {"mq": (tmq_ref, ROPE_DIM // 2), "mk": (tmk_ref, ROPE_DIM // 2),
            "rq": (trq_ref, RET_QK_DIM // 2), "rk": (trk_ref, RET_QK_DIM // 2)}
    for kind, c0, c1 in sections:
        for r in range(IN_PROJ_PARTS):
            acc = _wdot(h[r], w_ref[:, c0:c1])
            if kind in tabs:
                tab_ref, half = tabs[kind]
                acc = _rotate(acc, tab_ref, sl[r], half)
            o_ref[sl[r], c0:c1] = acc.astype(BF16)


def _in_proj(x2d, g, w, tabs, seq, sections):
    t, d = x2d.shape
    n = w.shape[1]
    tm = IN_PROJ_ROWS
    tiles_per_seq = seq // tm
    tab_spec = pl.BlockSpec((3, tm, LANES), lambda i: (0, i % tiles_per_seq, 0))
    return pl.pallas_call(
        functools.partial(_in_proj_kernel, sections=sections),
        grid=(t // tm,),
        in_specs=[
            pl.BlockSpec((tm, d), lambda i: (i, 0)),
            pl.BlockSpec((1, d), lambda i: (0, 0)),
            pl.BlockSpec((d, n), lambda i: (0, 0), pipeline_mode=pl.Buffered(1)),
            tab_spec, tab_spec, tab_spec, tab_spec,
        ],
        out_specs=pl.BlockSpec((tm, n), lambda i: (i, 0)),
        out_shape=jax.ShapeDtypeStruct((t, n), BF16),
        compiler_params=_params(1),
        name="in_proj",
    )(x2d, g, w, *tabs)


def _moba_kernel(q_ref, k_ref, v_ref, *refs, n_weights):
    w_refs, o_ref, wb_refs = refs[:n_weights], refs[n_weights], refs[n_weights + 1:2 * n_weights + 1]
    qp_ref, vt_ref, masked_ref = refs[2 * n_weights + 1:]
    for w_ref, wb_ref in zip(w_refs, wb_refs):
        wb_ref[...] = w_ref[...].astype(BF16)
    seq = q_ref.shape[0]
    blk = MOBA_BLOCK
    nb = seq // blk
    hd = MOBA_HEAD_DIM
    lane = lax.broadcasted_iota(jnp.int32, (seq, LANES), 1)
    lane8 = lax.broadcasted_iota(jnp.int32, (nb, LANES), 1)
    jidx = lax.broadcasted_iota(jnp.int32, (nb, seq), 0)
    qblk = lax.broadcasted_iota(jnp.int32, (nb, seq), 1) // blk
    ones = jnp.ones((MOBA_ONES_ROWS, seq), F32)
    for pp in range(MOBA_PAIRS_PER_STEP):
        ln = slice(pp * LANES, (pp + 1) * LANES)
        q2 = q_ref[:, ln]
        kmean2 = jnp.mean(k_ref[:, ln].astype(F32).reshape(nb, blk, LANES), axis=1)
        km = jnp.concatenate([jnp.where(lane8 < hd, kmean2, 0.0), jnp.where(lane8 >= hd, kmean2, 0.0)], axis=0)
        km_hi = km.astype(BF16)
        km_lo = (km - km_hi.astype(F32)).astype(BF16)
        gate2 = _dot_nt(jnp.concatenate([km_hi, km_lo], axis=0), q2)
        v2t = v_ref[:, ln].astype(F32).T
        for hh in range(2):
            h = 2 * pp + hh
            gate = gate2[hh * nb:(hh + 1) * nb] + gate2[(2 + hh) * nb:(3 + hh) * nb]
            rank = jnp.zeros((nb, seq), jnp.int32)
            for i in range(nb):
                gi = gate[i:i + 1, :]
                beats = ((gi > gate) | ((gi == gate) & (i < jidx))) & (i < qblk)
                rank = rank + beats.astype(jnp.int32)
            masked_ref[h] = ((jidx >= qblk) | (rank >= MOBA_TOPK)).astype(F32)
            mine = (lane >= hh * hd) & (lane < (hh + 1) * hd)
            qp_ref[h] = jnp.where(mine, q2, jnp.zeros_like(q2))
            vt_ref[h] = jnp.concatenate([v2t[hh * hd:(hh + 1) * hd], ones], axis=0).astype(BF16)

    kr = lax.broadcasted_iota(jnp.int32, (blk, blk), 0)
    qcol = lax.broadcasted_iota(jnp.int32, (blk, blk), 1)

    def scores(t):
        pp, c, j = tiles[t]
        out = []
        for hh in range(2):
            s = _dot_nt(k_ref[j * blk:(j + 1) * blk, pp * LANES:(pp + 1) * LANES],
                        qp_ref[2 * pp + hh, c * blk:(c + 1) * blk, :])
            out.append(jnp.where(kr <= qcol, s, NEG_INF) if j == c else s)
        return out

    tiles = [(pp, c, j) for pp in range(MOBA_PAIRS_PER_STEP) for c in range(nb) for j in range(c + 1)]
    pending = {t: scores(t) for t in range(min(MOBA_PREFETCH, len(tiles)))}
    m_run = [None, None]
    acc = [None, None]
    for t, (pp, c, j) in enumerate(tiles):
        r0, r1 = c * blk, (c + 1) * blk
        if t + MOBA_PREFETCH < len(tiles):
            pending[t + MOBA_PREFETCH] = scores(t + MOBA_PREFETCH)
        s_cur = pending.pop(t)
        for hh in range(2):
            h = 2 * pp + hh
            m_t = jnp.max(s_cur[hh], axis=0, keepdims=True)
            if j < c:
                off = masked_ref[h, j:j + 1, r0:r1] > 0.5
                m_t = jnp.where(off, NEG_INF, m_t)
            m_new = m_t if j == 0 else jnp.maximum(m_run[hh], m_t)
            shift = jnp.where(off, -NEG_INF, m_new) if j < c else m_new
            pv = _dot(vt_ref[h, :, j * blk:(j + 1) * blk], jnp.exp2(s_cur[hh] - shift).astype(BF16))
            acc[hh] = pv if j == 0 else acc[hh] * jnp.exp2(m_run[hh] - m_new) + pv
            m_run[hh] = m_new
        if j == c:
            res = [acc[hh][0:hd, :] * (1.0 / acc[hh][hd:hd + 1, :]) for hh in range(2)]
            o_ref[r0:r1, pp * LANES:(pp + 1) * LANES] = jnp.concatenate(res, axis=0).T.astype(BF16)


def _moba(proj, batch, seq, q_col, k_col, v_col, weights):
    t = proj.shape[0]
    width = MOBA_PAIRS_PER_STEP * LANES
    pairs = MOBA_HEADS * MOBA_HEAD_DIM // width
    n_heads = 2 * MOBA_PAIRS_PER_STEP
    steps = batch * pairs

    def spec(col0):
        assert col0 % width == 0
        return pl.BlockSpec((seq, width), lambda b, p: (b, col0 // width + p))

    def weight_spec(w):
        share = 1
        while (w.shape[0] * share) % (steps * BF16_SUBLANES):
            share *= 2
        rows = w.shape[0] * share // steps
        return pl.BlockSpec((rows, w.shape[1]), lambda b, p, share=share: ((b * pairs + p) // share, 0))

    w_specs = [weight_spec(w) for w in weights]
    outs = pl.pallas_call(
        functools.partial(_moba_kernel, n_weights=len(weights)),
        grid=(batch, pairs),
        in_specs=[spec(q_col), spec(k_col), spec(v_col)] + w_specs,
        out_specs=[pl.BlockSpec((seq, width), lambda b, p: (b, p))] + w_specs,
        out_shape=[jax.ShapeDtypeStruct((t, pairs * width), BF16)]
        + [jax.ShapeDtypeStruct(w.shape, BF16) for w in weights],
        scratch_shapes=[pltpu.VMEM((n_heads, seq, LANES), BF16),
                        pltpu.VMEM((n_heads, MOBA_HEAD_DIM + MOBA_ONES_ROWS, seq), BF16),
                        pltpu.VMEM((n_heads, seq // MOBA_BLOCK, seq), F32)],
        compiler_params=_params(2),
        name="moba",
    )(proj, proj, proj, *weights)
    return outs[0], outs[1:]


_RET_LOG_G = [math.log(1.0 - 2.0 ** (-5.0 - h)) for h in range(RET_HEADS)]


def _retention_kernel(q_ref, k_ref, v_ref, g_ref, o_ref):
    seq = q_ref.shape[0]
    ch = RET_CHUNK
    dv = RET_V_DIM
    lane = lax.broadcasted_iota(jnp.int32, (ch, LANES), 1)
    rowf = lax.broadcasted_iota(jnp.int32, (ch, LANES), 0).astype(F32)
    ri = lax.broadcasted_iota(jnp.int32, (ch, ch), 0)
    ci = lax.broadcasted_iota(jnp.int32, (ch, ch), 1)
    diff = (ri - ci).astype(F32)
    heads = []
    for h in range(RET_HEADS):
        log_g = _RET_LOG_G[h]
        hh = h % 2
        heads.append(dict(
            inner_decay=jnp.where(diff >= 0, jnp.exp(log_g * jnp.maximum(diff, 0.0)), 0.0),
            q_decay=jnp.exp(log_g * (rowf + 1.0)),
            k_decay=jnp.exp(log_g * (ch - 1.0 - rowf)),
            chunk_decay=math.exp(log_g * ch),
            mine=(lane >= hh * RET_QK_DIM) & (lane < (hh + 1) * RET_QK_DIM),
            qk=slice((h // 2) * LANES, (h // 2 + 1) * LANES),
            v=slice(h * dv, (h + 1) * dv),
            state=jnp.zeros((LANES, dv), F32),
        ))
    for n in range(seq // ch):
        r0, r1 = n * ch, (n + 1) * ch
        for hd in heads:
            qc = jnp.where(hd["mine"], q_ref[r0:r1, hd["qk"]], jnp.zeros((), BF16))
            kc = k_ref[r0:r1, hd["qk"]]
            vc = v_ref[r0:r1, hd["v"]]
            attn = _dot_nt(qc, kc) * hd["inner_decay"]
            out = _dot(attn.astype(BF16), vc)
            out = out + _dot((qc.astype(F32) * hd["q_decay"]).astype(BF16), hd["state"].astype(BF16))
            hd["state"] = (hd["state"] * hd["chunk_decay"]
                           + _dot_tn((kc.astype(F32) * hd["k_decay"]).astype(BF16), vc))
            out = out * lax.rsqrt(jnp.mean(out * out, axis=-1, keepdims=True) + NORM_EPS)
            gate = g_ref[r0:r1, hd["v"]].astype(F32)
            o_ref[r0:r1, hd["v"]] = (gate * jax.nn.sigmoid(gate) * out).astype(BF16)


def _retention(proj, batch, seq, q_col, k_col, v_col, g_col):
    t = proj.shape[0]
    wqk = RET_HEADS * RET_QK_DIM
    wv = RET_HEADS * RET_V_DIM

    def spec(col0, width):
        assert col0 % width == 0
        return pl.BlockSpec((seq, width), lambda b: (b, col0 // width))

    return pl.pallas_call(
        _retention_kernel,
        grid=(batch,),
        in_specs=[spec(q_col, wqk), spec(k_col, wqk), spec(v_col, wv), spec(g_col, wv)],
        out_specs=pl.BlockSpec((seq, wv), lambda b: (b, 0)),
        out_shape=jax.ShapeDtypeStruct((t, wv), BF16),
        compiler_params=_params(1),
        name="retention",
    )(proj, proj, proj, proj)


def _mem_kv_kernel(m_ref, g_ref, w_ref, o_ref):
    o_ref[...] = _wdot(_rms(m_ref[...], g_ref[...]).astype(BF16), w_ref[...]).astype(BF16)


def _mem_kv(mem2d, g, w, rows):
    t, d = mem2d.shape
    n = w.shape[1]
    return pl.pallas_call(
        _mem_kv_kernel,
        grid=(t // rows,),
        in_specs=[pl.BlockSpec((rows, d), lambda i: (i, 0)),
                  pl.BlockSpec((1, d), lambda i: (0, 0)),
                  pl.BlockSpec((d, n), lambda i: (0, 0), pipeline_mode=pl.Buffered(1))],
        out_specs=pl.BlockSpec((rows, n), lambda i: (i, 0)),
        out_shape=jax.ShapeDtypeStruct((t, n), BF16),
        compiler_params=_params(1),
        name="mem_kv",
    )(mem2d, g, w)


def _mix_cross_kernel(mo_ref, ro_ref, x_ref, kv_ref, wo_ref, wq_ref, wc_ref,
                      g_mix_ref, g_pre_ref, g_post_ref, o_ref):
    d = x_ref.shape[1]
    half = mo_ref.shape[1]
    dc = d // CROSS_HEADS
    rows = x_ref.shape[0] // MIX_PARTS
    sl = [slice(r * rows, (r + 1) * rows) for r in range(MIX_PARTS)]
    mix = [_wdot(mo_ref[s, :], wo_ref[0:half, :]) + _wdot(ro_ref[s, :], wo_ref[half:, :]) for s in sl]
    x1 = [x_ref[s, :] + _rms(m, g_mix_ref[...]) for s, m in zip(sl, mix)]
    h = [_rms(x, g_pre_ref[...], scale=dc ** -0.5).astype(BF16) for x in x1]
    cq = [_wdot(hh, wq_ref[...]).astype(BF16) for hh in h]
    att = []
    for r in range(MIX_PARTS):
        heads = []
        for hc in range(CROSS_HEADS):
            c0, c1 = hc * dc, (hc + 1) * dc
            s = _dot_nt(cq[r][:, c0:c1], kv_ref[:, c0:c1])
            e = jnp.exp(s - jnp.max(s, axis=1, keepdims=True))
            o = _dot(e.astype(BF16), kv_ref[:, d + c0:d + c1])
            heads.append((o * (1.0 / jnp.sum(e, axis=1, keepdims=True))).astype(BF16))
        att.append(jnp.concatenate(heads, axis=1))
    c = [_wdot(a, wc_ref[...]) for a in att]
    for r in range(MIX_PARTS):
        o_ref[sl[r], :] = x1[r] + _rms(c[r], g_post_ref[...])


def _mix_cross(mo, ro, x2d, kv, w_out, w_cq, w_co, g_mix, g_pre, g_post, seq, n_mem):
    t, d = x2d.shape
    tm = MIX_ROWS
    tiles_per_seq = seq // tm
    half = mo.shape[1]
    full = lambda i: (0, 0)
    weight = pl.BlockSpec((d, d), full, pipeline_mode=pl.Buffered(1))
    return pl.pallas_call(
        _mix_cross_kernel,
        grid=(t // tm,),
        in_specs=[
            pl.BlockSpec((tm, half), lambda i: (i, 0)),
            pl.BlockSpec((tm, half), lambda i: (i, 0)),
            pl.BlockSpec((tm, d), lambda i: (i, 0)),
            pl.BlockSpec((n_mem, 2 * d), lambda i: (i // tiles_per_seq, 0)),
            weight, weight, weight,
            pl.BlockSpec((1, d), full), pl.BlockSpec((1, d), full), pl.BlockSpec((1, d), full),
        ],
        out_specs=pl.BlockSpec((tm, d), lambda i: (i, 0)),
        out_shape=jax.ShapeDtypeStruct((t, d), F32),
        compiler_params=_params(1),
        name="mix_cross",
    )(mo, ro, x2d, kv, w_out, w_cq, w_co, g_mix, g_pre, g_post)


def _ffn_kernel(x_ref, wgu_ref, wd_ref, g_pre_ref, g_post_ref, o_ref, *, d_ff, bounds):
    rows = x_ref.shape[0] // FFN_PARTS
    xs = [x_ref[r * rows:(r + 1) * rows, :] for r in range(FFN_PARTS)]
    hs = [_rms(x, g_pre_ref[...]).astype(BF16) for x in xs]
    fs = [None] * FFN_PARTS
    for c0, c1 in bounds:
        for r in range(FFN_PARTS):
            gate = _wdot(hs[r], wgu_ref[:, c0:c1])
            up = _wdot(hs[r], wgu_ref[:, d_ff + c0:d_ff + c1])
            act = (gate * jax.nn.sigmoid(gate) * up).astype(BF16)
            down = _wdot(act, wd_ref[c0:c1, :])
            fs[r] = down if fs[r] is None else fs[r] + down
    for r in range(FFN_PARTS):
        o_ref[r * rows:(r + 1) * rows, :] = xs[r] + _rms(fs[r], g_post_ref[...])


def _ffn(x2d, w_gate_up, w_down, g_pre, g_post):
    t, d = x2d.shape
    d_ff = w_down.shape[0]
    tm = FFN_ROWS
    assert d_ff % MXU_DIM == 0 and FFN_CHUNK % MXU_DIM == 0
    edges = list(range(0, d_ff, FFN_CHUNK)) + [d_ff]
    bounds = tuple(zip(edges[:-1], edges[1:]))
    full = lambda i: (0, 0)
    return pl.pallas_call(
        functools.partial(_ffn_kernel, d_ff=d_ff, bounds=bounds),
        grid=(t // tm,),
        in_specs=[
            pl.BlockSpec((tm, d), lambda i: (i, 0)),
            pl.BlockSpec((d, 2 * d_ff), full, pipeline_mode=pl.Buffered(1)),
            pl.BlockSpec((d_ff, d), full, pipeline_mode=pl.Buffered(1)),
            pl.BlockSpec((1, d), full), pl.BlockSpec((1, d), full),
        ],
        out_specs=pl.BlockSpec((tm, d), lambda i: (i, 0)),
        out_shape=jax.ShapeDtypeStruct((t, d), F32),
        compiler_params=_params(1),
        name="ffn",
    )(x2d, w_gate_up, w_down, g_pre, g_post)


def kernel(x, mem, g_pre_mix, w_in, w_out, g_post_mix, g_pre_cross, g_mem, w_cq, w_ckv, w_co,
           g_post_cross, g_pre_ffn, w_gate_up, w_down, g_post_ffn):
    batch, seq, d = x.shape
    n_mem = mem.shape[1]
    depth = w_in.shape[0]
    moba_w = MOBA_HEADS * MOBA_HEAD_DIM
    ret_qk_w = RET_HEADS * RET_QK_DIM
    ret_v_w = RET_HEADS * RET_V_DIM
    c_mq, c_mk, c_mv = 0, moba_w, 2 * moba_w
    c_rq = 3 * moba_w
    c_rk = c_rq + ret_qk_w
    c_rv = c_rk + ret_qk_w
    c_rg = c_rv + ret_v_w
    n_proj = c_rg + ret_v_w
    assert w_in.shape[2] == n_proj
    sections = (("mq", c_mq, c_mk), ("mk", c_mk, c_mv), ("mv", c_mv, c_rq), ("rq", c_rq, c_rk),
                ("rk", c_rk, c_rv), ("rv", c_rv, c_rg), ("rg", c_rg, n_proj))

    moba_inv = np.power(ROPE_THETA, -np.arange(ROPE_DIM // 2, dtype=np.float64) * 2.0 / ROPE_DIM)
    ret_inv = 1.0 / np.power(RET_THETA, np.linspace(0.0, 1.0, RET_QK_DIM // 2))
    tabs = (
        _rotary_tables(seq, moba_inv, ROPE_DIM, MOBA_HEAD_DIM, MOBA_HEAD_DIM ** -0.5 * math.log2(math.e)),
        _rotary_tables(seq, moba_inv, ROPE_DIM, MOBA_HEAD_DIM, 1.0),
        _rotary_tables(seq, ret_inv, RET_QK_DIM, RET_QK_DIM, 1.0),
        _rotary_tables(seq, ret_inv, RET_QK_DIM, RET_QK_DIM, RET_QK_DIM ** -0.5),
    )

    xf = x.reshape(batch * seq, d)
    mem2d = mem.reshape(batch * n_mem, d)
    row = lambda g: g.reshape(1, d)
    for l in range(depth):
        proj = _in_proj(xf, row(g_pre_mix[l]), w_in[l], tabs, seq, sections)
        mo, (b_ckv, b_out, b_cq, b_co, b_gate_up, b_down) = _moba(
            proj, batch, seq, c_mq, c_mk, c_mv,
            (w_ckv[l], w_out[l], w_cq[l], w_co[l], w_gate_up[l], w_down[l]))
        ro = _retention(proj, batch, seq, c_rq, c_rk, c_rv, c_rg)
        kv = _mem_kv(mem2d, row(g_mem[l]), b_ckv, MEM_KV_ROWS)
        x2 = _mix_cross(mo, ro, xf, kv, b_out, b_cq, b_co, row(g_post_mix[l]),
                        row(g_pre_cross[l]), row(g_post_cross[l]), seq, n_mem)
        xf = _ffn(x2, b_gate_up, b_down, row(g_pre_ffn[l]), row(g_post_ffn[l]))
    return xf.reshape(batch, seq, d)
```

```python
import functools
import math

import jax
import jax.numpy as jnp
import numpy as np
from jax import lax
from jax.experimental import pallas as pl
from jax.experimental.pallas import tpu as pltpu

F32 = jnp.float32
BF16 = jnp.bfloat16

NORM_EPS = 1e-6
NEG_INF = -1e30

LANES = 128
BF16_SUBLANES = 16
MXU_DIM = 256
VMEM_LIMIT_BYTES = 56 * 1024 * 1024

MOBA_HEAD_DIM = 64
MOBA_HEADS = 8
MOBA_BLOCK = 256
MOBA_TOPK = 3
MOBA_ONES_ROWS = 16
MOBA_PREFETCH = 2
MOBA_PAIRS_PER_STEP = 2
ROPE_THETA = 500000.0
ROPE_DIM = MOBA_HEAD_DIM // 4

RET_HEADS = 4
RET_QK_DIM = 64
RET_V_DIM = 128
RET_THETA = 10000.0
RET_CHUNK = 256

CROSS_HEADS = 4

IN_PROJ_ROWS = 1024
IN_PROJ_PARTS = 2
MIX_ROWS = 1024
MIX_PARTS = 2
FFN_ROWS = 1024
FFN_PARTS = 4
FFN_CHUNK = 1536


def _dot(a, b):
    return jnp.dot(a, b, preferred_element_type=F32)


def _wdot(a, w):
    return jnp.dot(a, w.astype(BF16), preferred_element_type=F32)


def _dot_nt(a, b):
    return lax.dot_general(a, b, (((1,), (1,)), ((), ())), preferred_element_type=F32)


def _dot_tn(a, b):
    return lax.dot_general(a, b, (((0,), (0,)), ((), ())), preferred_element_type=F32)


def _rms(x, g, scale=None):
    inv = lax.rsqrt(jnp.mean(x * x, axis=-1, keepdims=True) + NORM_EPS)
    if scale is not None:
        inv = inv * scale
    return x * inv * g


def _params(n_grid_dims):
    return pltpu.CompilerParams(
        dimension_semantics=("arbitrary",) * n_grid_dims,
        vmem_limit_bytes=VMEM_LIMIT_BYTES,
    )


def _rotary_tables(seq, inv_freq, rot_dim, head_dim, scale):
    half = rot_dim // 2
    ang = np.arange(seq, dtype=np.float64)[:, None] * inv_freq[None, :]
    cos, sin = np.cos(ang), np.sin(ang)
    pad = head_dim - rot_dim
    a = np.concatenate([cos, cos, np.ones((seq, pad))], axis=1)
    bm = np.concatenate([-sin, np.zeros((seq, half + pad))], axis=1)
    bp = np.concatenate([np.zeros((seq, half)), sin, np.zeros((seq, pad))], axis=1)
    reps = LANES // head_dim
    tabs = np.stack([np.tile(t, (1, reps)) for t in (a, bm, bp)], axis=0)
    return jnp.asarray((tabs * scale).astype(np.float32))


def _rotate(acc, tab_ref, rows, half):
    a, bm, bp = tab_ref[0, rows, :], tab_ref[1, rows, :], tab_ref[2, rows, :]
    outs = []
    for c in range(acc.shape[1] // LANES):
        xs = acc[:, c * LANES:(c + 1) * LANES]
        outs.append(xs * a + pltpu.roll(xs, LANES - half, 1) * bm + pltpu.roll(xs, half, 1) * bp)
    return jnp.concatenate(outs, axis=1)


def _in_proj_kernel(x_ref, g_ref, w_ref, tmq_ref, tmk_ref, trq_ref, trk_ref, o_ref, *, sections):
    rows = x_ref.shape[0] // IN_PROJ_PARTS
    sl = [slice(r * rows, (r + 1) * rows) for r in range(IN_PROJ_PARTS)]
    h = [_rms(x_ref[s, :], g_ref[...]).astype(BF16) for s in sl]
    tabs = {"mq": (tmq_ref, ROPE_DIM // 2), "mk": (tmk_ref, ROPE_DIM // 2),
            "rq": (trq_ref, RET_QK_DIM // 2), "rk": (trk_ref, RET_QK_DIM // 2)}
    for kind, c0, c1 in sections:
        for r in range(IN_PROJ_PARTS):
            acc = _wdot(h[r], w_ref[:, c0:c1])
            if kind in tabs:
                tab_ref, half = tabs[kind]
                acc = _rotate(acc, tab_ref, sl[r], half)
            o_ref[sl[r], c0:c1] = acc.astype(BF16)


def _in_proj(x2d, g, w, tabs, seq, sections):
    t, d = x2d.shape
    n = w.shape[1]
    tm = IN_PROJ_ROWS
    tiles_per_seq = seq // tm
    tab_spec = pl.BlockSpec((3, tm, LANES), lambda i: (0, i % tiles_per_seq, 0))
    return pl.pallas_call(
        functools.partial(_in_proj_kernel, sections=sections),
        grid=(t // tm,),
        in_specs=[
            pl.BlockSpec((tm, d), lambda i: (i, 0)),
            pl.BlockSpec((1, d), lambda i: (0, 0)),
            pl.BlockSpec((d, n), lambda i: (0, 0), pipeline_mode=pl.Buffered(1)),
            tab_spec, tab_spec, tab_spec, tab_spec,
        ],
        out_specs=pl.BlockSpec((tm, n), lambda i: (i, 0)),
        out_shape=jax.ShapeDtypeStruct((t, n), BF16),
        compiler_params=_params(1),
        name="in_proj",
    )(x2d, g, w, *tabs)


def _moba_kernel(q_ref, k_ref, v_ref, *refs, n_weights):
    w_refs, o_ref, wb_refs = refs[:n_weights], refs[n_weights], refs[n_weights + 1:2 * n_weights + 1]
    qp_ref, vt_ref, masked_ref = refs[2 * n_weights + 1:]
    for w_ref, wb_ref in zip(w_refs, wb_refs):
        wb_ref[...] = w_ref[...].astype(BF16)
    seq = q_ref.shape[0]
    blk = MOBA_BLOCK
    nb = seq // blk
    hd = MOBA_HEAD_DIM
    lane = lax.broadcasted_iota(jnp.int32, (seq, LANES), 1)
    lane8 = lax.broadcasted_iota(jnp.int32, (nb, LANES), 1)
    jidx = lax.broadcasted_iota(jnp.int32, (nb, seq), 0)
    qblk = lax.broadcasted_iota(jnp.int32, (nb, seq), 1) // blk
    ones = jnp.ones((MOBA_ONES_ROWS, seq), F32)
    for pp in range(MOBA_PAIRS_PER_STEP):
        ln = slice(pp * LANES, (pp + 1) * LANES)
        q2 = q_ref[:, ln]
        kmean2 = jnp.mean(k_ref[:, ln].astype(F32).reshape(nb, blk, LANES), axis=1)
        km = jnp.concatenate([jnp.where(lane8 < hd, kmean2, 0.0), jnp.where(lane8 >= hd, kmean2, 0.0)], axis=0)
        km_hi = km.astype(BF16)
        km_lo = (km - km_hi.astype(F32)).astype(BF16)
        gate2 = _dot_nt(jnp.concatenate([km_hi, km_lo], axis=0), q2)
        v2t = v_ref[:, ln].astype(F32).T
        for hh in range(2):
            h = 2 * pp + hh
            gate = gate2[hh * nb:(hh + 1) * nb] + gate2[(2 + hh) * nb:(3 + hh) * nb]
            rank = jnp.zeros((nb, seq), jnp.int32)
            for i in range(nb):
                gi = gate[i:i + 1, :]
                beats = ((gi > gate) | ((gi == gate) & (i < jidx))) & (i < qblk)
                rank = rank + beats.astype(jnp.int32)
            masked_ref[h] = ((jidx >= qblk) | (rank >= MOBA_TOPK)).astype(F32)
            mine = (lane >= hh * hd) & (lane < (hh + 1) * hd)
            qp_ref[h] = jnp.where(mine, q2, jnp.zeros_like(q2))
            vt_ref[h] = jnp.concatenate([v2t[hh * hd:(hh + 1) * hd], ones], axis=0).astype(BF16)

    kr = lax.broadcasted_iota(jnp.int32, (blk, blk), 0)
    qcol = lax.broadcasted_iota(jnp.int32, (blk, blk), 1)

    def scores(t):
        pp, c, j = tiles[t]
        out = []
        for hh in range(2):
            s = _dot_nt(k_ref[j * blk:(j + 1) * blk, pp * LANES:(pp + 1) * LANES],
                        qp_ref[2 * pp + hh, c * blk:(c + 1) * blk, :])
            out.append(jnp.where(kr <= qcol, s, NEG_INF) if j == c else s)
        return out

    tiles = [(pp, c, j) for pp in range(MOBA_PAIRS_PER_STEP) for c in range(nb) for j in range(c + 1)]
    pending = {t: scores(t) for t in range(min(MOBA_PREFETCH, len(tiles)))}
    m_run = [None, None]
    acc = [None, None]
    for t, (pp, c, j) in enumerate(tiles):
        r0, r1 = c * blk, (c + 1) * blk
        if t + MOBA_PREFETCH < len(tiles):
            pending[t + MOBA_PREFETCH] = scores(t + MOBA_PREFETCH)
        s_cur = pending.pop(t)
        for hh in range(2):
            h = 2 * pp + hh
            m_t = jnp.max(s_cur[hh], axis=0, keepdims=True)
            if j < c:
                off = masked_ref[h, j:j + 1, r0:r1] > 0.5
                m_t = jnp.where(off, NEG_INF, m_t)
            m_new = m_t if j == 0 else jnp.maximum(m_run[hh], m_t)
            shift = jnp.where(off, -NEG_INF, m_new) if j < c else m_new
            pv = _dot(vt_ref[h, :, j * blk:(j + 1) * blk], jnp.exp2(s_cur[hh] - shift).astype(BF16))
            acc[hh] = pv if j == 0 else acc[hh] * jnp.exp2(m_run[hh] - m_new) + pv
            m_run[hh] = m_new
        if j == c:
            res = [acc[hh][0:hd, :] * (1.0 / acc[hh][hd:hd + 1, :]) for hh in range(2)]
            o_ref[r0:r1, pp * LANES:(pp + 1) * LANES] = jnp.concatenate(res, axis=0).T.astype(BF16)


def _moba(proj, batch, seq, q_col, k_col, v_col, weights):
    t = proj.shape[0]
    width = MOBA_PAIRS_PER_STEP * LANES
    pairs = MOBA_HEADS * MOBA_HEAD_DIM // width
    n_heads = 2 * MOBA_PAIRS_PER_STEP
    steps = batch * pairs

    def spec(col0):
        assert col0 % width == 0
        return pl.BlockSpec((seq, width), lambda b, p: (b, col0 // width + p))

    def weight_spec(w):
        share = 1
        while (w.shape[0] * share) % (steps * BF16_SUBLANES):
            share *= 2
        rows = w.shape[0] * share // steps
        return pl.BlockSpec((rows, w.shape[1]), lambda b, p, share=share: ((b * pairs + p) // share, 0))

    w_specs = [weight_spec(w) for w in weights]
    outs = pl.pallas_call(
        functools.partial(_moba_kernel, n_weights=len(weights)),
        grid=(batch, pairs),
        in_specs=[spec(q_col), spec(k_col), spec(v_col)] + w_specs,
        out_specs=[pl.BlockSpec((seq, width), lambda b, p: (b, p))] + w_specs,
        out_shape=[jax.ShapeDtypeStruct((t, pairs * width), BF16)]
        + [jax.ShapeDtypeStruct(w.shape, BF16) for w in weights],
        scratch_shapes=[pltpu.VMEM((n_heads, seq, LANES), BF16),
                        pltpu.VMEM((n_heads, MOBA_HEAD_DIM + MOBA_ONES_ROWS, seq), BF16),
                        pltpu.VMEM((n_heads, seq // MOBA_BLOCK, seq), F32)],
        compiler_params=_params(2),
        name="moba",
    )(proj, proj, proj, *weights)
    return outs[0], outs[1:]


_RET_LOG_G = [math.log(1.0 - 2.0 ** (-5.0 - h)) for h in range(RET_HEADS)]


def _retention_kernel(q_ref, k_ref, v_ref, g_ref, mem_ref, g_mem_ref, w_ckv_ref, o_ref, kv_ref):
    kv_ref[...] = _wdot(_rms(mem_ref[...], g_mem_ref[...]).astype(BF16), w_ckv_ref[...]).astype(BF16)
    seq = q_ref.shape[0]
    ch = RET_CHUNK
    dv = RET_V_DIM
    lane = lax.broadcasted_iota(jnp.int32, (ch, LANES), 1)
    rowf = lax.broadcasted_iota(jnp.int32, (ch, LANES), 0).astype(F32)
    ri = lax.broadcasted_iota(jnp.int32, (ch, ch), 0)
    ci = lax.broadcasted_iota(jnp.int32, (ch, ch), 1)
    diff = (ri - ci).astype(F32)
    heads = []
    for h in range(RET_HEADS):
        log_g = _RET_LOG_G[h]
        hh = h % 2
        heads.append(dict(
            inner_decay=jnp.where(diff >= 0, jnp.exp(log_g * jnp.maximum(diff, 0.0)), 0.0),
            q_decay=jnp.exp(log_g * (rowf + 1.0)),
            k_decay=jnp.exp(log_g * (ch - 1.0 - rowf)),
            chunk_decay=math.exp(log_g * ch),
            mine=(lane >= hh * RET_QK_DIM) & (lane < (hh + 1) * RET_QK_DIM),
            qk=slice((h // 2) * LANES, (h // 2 + 1) * LANES),
            v=slice(h * dv, (h + 1) * dv),
            state=jnp.zeros((LANES, dv), F32),
        ))
    for n in range(seq // ch):
        r0, r1 = n * ch, (n + 1) * ch
        for hd in heads:
            qc = jnp.where(hd["mine"], q_ref[r0:r1, hd["qk"]], jnp.zeros((), BF16))
            kc = k_ref[r0:r1, hd["qk"]]
            vc = v_ref[r0:r1, hd["v"]]
            attn = _dot_nt(qc, kc) * hd["inner_decay"]
            out = _dot(attn.astype(BF16), vc)
            out = out + _dot((qc.astype(F32) * hd["q_decay"]).astype(BF16), hd["state"].astype(BF16))
            hd["state"] = (hd["state"] * hd["chunk_decay"]
                           + _dot_tn((kc.astype(F32) * hd["k_decay"]).astype(BF16), vc))
            out = out * lax.rsqrt(jnp.mean(out * out, axis=-1, keepdims=True) + NORM_EPS)
            gate = g_ref[r0:r1, hd["v"]].astype(F32)
            o_ref[r0:r1, hd["v"]] = (gate * jax.nn.sigmoid(gate) * out).astype(BF16)


def _retention(proj, batch, seq, q_col, k_col, v_col, g_col, mem2d, g_mem, w_ckv):
    t = proj.shape[0]
    wqk = RET_HEADS * RET_QK_DIM
    wv = RET_HEADS * RET_V_DIM
    n_mem = mem2d.shape[0] // batch
    d, n_kv = w_ckv.shape

    def spec(col0, width):
        assert col0 % width == 0
        return pl.BlockSpec((seq, width), lambda b: (b, col0 // width))

    return pl.pallas_call(
        _retention_kernel,
        grid=(batch,),
        in_specs=[spec(q_col, wqk), spec(k_col, wqk), spec(v_col, wv), spec(g_col, wv),
                  pl.BlockSpec((n_mem, d), lambda b: (b, 0)),
                  pl.BlockSpec((1, d), lambda b: (0, 0)),
                  pl.BlockSpec((d, n_kv), lambda b: (0, 0), pipeline_mode=pl.Buffered(1))],
        out_specs=[pl.BlockSpec((seq, wv), lambda b: (b, 0)),
                   pl.BlockSpec((n_mem, n_kv), lambda b: (b, 0))],
        out_shape=[jax.ShapeDtypeStruct((t, wv), BF16),
                   jax.ShapeDtypeStruct((batch * n_mem, n_kv), BF16)],
        compiler_params=_params(1),
        name="retention",
    )(proj, proj, proj, proj, mem2d, g_mem, w_ckv)


def _mix_cross_kernel(mo_ref, ro_ref, x_ref, kv_ref, wo_ref, wq_ref, wc_ref,
                      g_mix_ref, g_pre_ref, g_post_ref, o_ref):
    d = x_ref.shape[1]
    half = mo_ref.shape[1]
    dc = d // CROSS_HEADS
    rows = x_ref.shape[0] // MIX_PARTS
    sl = [slice(r * rows, (r + 1) * rows) for r in range(MIX_PARTS)]
    mix = [_wdot(mo_ref[s, :], wo_ref[0:half, :]) + _wdot(ro_ref[s, :], wo_ref[half:, :]) for s in sl]
    x1 = [x_ref[s, :] + _rms(m, g_mix_ref[...]) for s, m in zip(sl, mix)]
    h = [_rms(x, g_pre_ref[...], scale=dc ** -0.5).astype(BF16) for x in x1]
    cq = [_wdot(hh, wq_ref[...]).astype(BF16) for hh in h]
    att = []
    for r in range(MIX_PARTS):
        heads = []
        for hc in range(CROSS_HEADS):
            c0, c1 = hc * dc, (hc + 1) * dc
            s = _dot_nt(cq[r][:, c0:c1], kv_ref[:, c0:c1])
            e = jnp.exp(s - jnp.max(s, axis=1, keepdims=True))
            o = _dot(e.astype(BF16), kv_ref[:, d + c0:d + c1])
            heads.append((o * (1.0 / jnp.sum(e, axis=1, keepdims=True))).astype(BF16))
        att.append(jnp.concatenate(heads, axis=1))
    c = [_wdot(a, wc_ref[...]) for a in att]
    for r in range(MIX_PARTS):
        o_ref[sl[r], :] = x1[r] + _rms(c[r], g_post_ref[...])


def _mix_cross(mo, ro, x2d, kv, w_out, w_cq, w_co, g_mix, g_pre, g_post, seq, n_mem):
    t, d = x2d.shape
    tm = MIX_ROWS
    tiles_per_seq = seq // tm
    half = mo.shape[1]
    full = lambda i: (0, 0)
    weight = pl.BlockSpec((d, d), full, pipeline_mode=pl.Buffered(1))
    return pl.pallas_call(
        _mix_cross_kernel,
        grid=(t // tm,),
        in_specs=[
            pl.BlockSpec((tm, half), lambda i: (i, 0)),
            pl.BlockSpec((tm, half), lambda i: (i, 0)),
            pl.BlockSpec((tm, d), lambda i: (i, 0)),
            pl.BlockSpec((n_mem, 2 * d), lambda i: (i // tiles_per_seq, 0)),
            weight, weight, weight,
            pl.BlockSpec((1, d), full), pl.BlockSpec((1, d), full), pl.BlockSpec((1, d), full),
        ],
        out_specs=pl.BlockSpec((tm, d), lambda i: (i, 0)),
        out_shape=jax.ShapeDtypeStruct((t, d), F32),
        compiler_params=_params(1),
        name="mix_cross",
    )(mo, ro, x2d, kv, w_out, w_cq, w_co, g_mix, g_pre, g_post)


def _ffn_kernel(x_ref, wgu_ref, wd_ref, g_pre_ref, g_post_ref, o_ref, *, d_ff, bounds):
    rows = x_ref.shape[0] // FFN_PARTS
    xs = [x_ref[r * rows:(r + 1) * rows, :] for r in range(FFN_PARTS)]
    hs = [_rms(x, g_pre_ref[...]).astype(BF16) for x in xs]
    fs = [None] * FFN_PARTS
    for c0, c1 in bounds:
        for r in range(FFN_PARTS):
            gate = _wdot(hs[r], wgu_ref[:, c0:c1])
            up = _wdot(hs[r], wgu_ref[:, d_ff + c0:d_ff + c1])
            act = (gate * jax.nn.sigmoid(gate) * up).astype(BF16)
            down = _wdot(act, wd_ref[c0:c1, :])
            fs[r] = down if fs[r] is None else fs[r] + down
    for r in range(FFN_PARTS):
        o_ref[r * rows:(r + 1) * rows, :] = xs[r] + _rms(fs[r], g_post_ref[...])


def _ffn(x2d, w_gate_up, w_down, g_pre, g_post):
    t, d = x2d.shape
    d_ff = w_down.shape[0]
    tm = FFN_ROWS
    assert d_ff % MXU_DIM == 0 and FFN_CHUNK % MXU_DIM == 0
    edges = list(range(0, d_ff, FFN_CHUNK)) + [d_ff]
    bounds = tuple(zip(edges[:-1], edges[1:]))
    full = lambda i: (0, 0)
    return pl.pallas_call(
        functools.partial(_ffn_kernel, d_ff=d_ff, bounds=bounds),
        grid=(t // tm,),
        in_specs=[
            pl.BlockSpec((tm, d), lambda i: (i, 0)),
            pl.BlockSpec((d, 2 * d_ff), full, pipeline_mode=pl.Buffered(1)),
            pl.BlockSpec((d_ff, d), full, pipeline_mode=pl.Buffered(1)),
            pl.BlockSpec((1, d), full), pl.BlockSpec((1, d), full),
        ],
        out_specs=pl.BlockSpec((tm, d), lambda i: (i, 0)),
        out_shape=jax.ShapeDtypeStruct((t, d), F32),
        compiler_params=_params(1),
        name="ffn",
    )(x2d, w_gate_up, w_down, g_pre, g_post)


def kernel(x, mem, g_pre_mix, w_in, w_out, g_post_mix, g_pre_cross, g_mem, w_cq, w_ckv, w_co,
           g_post_cross, g_pre_ffn, w_gate_up, w_down, g_post_ffn):
    batch, seq, d = x.shape
    n_mem = mem.shape[1]
    depth = w_in.shape[0]
    moba_w = MOBA_HEADS * MOBA_HEAD_DIM
    ret_qk_w = RET_HEADS * RET_QK_DIM
    ret_v_w = RET_HEADS * RET_V_DIM
    c_mq, c_mk, c_mv = 0, moba_w, 2 * moba_w
    c_rq = 3 * moba_w
    c_rk = c_rq + ret_qk_w
    c_rv = c_rk + ret_qk_w
    c_rg = c_rv + ret_v_w
    n_proj = c_rg + ret_v_w
    assert w_in.shape[2] == n_proj
    sections = (("mq", c_mq, c_mk), ("mk", c_mk, c_mv), ("mv", c_mv, c_rq), ("rq", c_rq, c_rk),
                ("rk", c_rk, c_rv), ("rv", c_rv, c_rg), ("rg", c_rg, n_proj))

    moba_inv = np.power(ROPE_THETA, -np.arange(ROPE_DIM // 2, dtype=np.float64) * 2.0 / ROPE_DIM)
    ret_inv = 1.0 / np.power(RET_THETA, np.linspace(0.0, 1.0, RET_QK_DIM // 2))
    tabs = (
        _rotary_tables(seq, moba_inv, ROPE_DIM, MOBA_HEAD_DIM, MOBA_HEAD_DIM ** -0.5 * math.log2(math.e)),
        _rotary_tables(seq, moba_inv, ROPE_DIM, MOBA_HEAD_DIM, 1.0),
        _rotary_tables(seq, ret_inv, RET_QK_DIM, RET_QK_DIM, 1.0),
        _rotary_tables(seq, ret_inv, RET_QK_DIM, RET_QK_DIM, RET_QK_DIM ** -0.5),
    )

    xf = x.reshape(batch * seq, d)
    mem2d = mem.reshape(batch * n_mem, d)
    row = lambda g: g.reshape(1, d)
    for l in range(depth):
        proj = _in_proj(xf, row(g_pre_mix[l]), w_in[l], tabs, seq, sections)
        mo, (b_ckv, b_out, b_cq, b_co, b_gate_up, b_down) = _moba(
            proj, batch, seq, c_mq, c_mk, c_mv,
            (w_ckv[l], w_out[l], w_cq[l], w_co[l], w_gate_up[l], w_down[l]))
        ro, kv = _retention(proj, batch, seq, c_rq, c_rk, c_rv, c_rg, mem2d, row(g_mem[l]), b_ckv)
        x2 = _mix_cross(mo, ro, xf, kv, b_out, b_cq, b_co, row(g_post_mix[l]),
                        row(g_pre_cross[l]), row(g_post_cross[l]), seq, n_mem)
        xf = _ffn(x2, b_gate_up, b_down, row(g_pre_ffn[l]), row(g_post_ffn[l]))
    return xf.reshape(batch, seq, d)
```

```python
import functools
import math

import jax
import jax.numpy as jnp
import numpy as np
from jax import lax
from jax.experimental import pallas as pl
from jax.experimental.pallas import tpu as pltpu

F32 = jnp.float32
BF16 = jnp.bfloat16

NORM_EPS = 1e-6
NEG_INF = -1e30

LANES = 128
BF16_SUBLANES = 16
MXU_DIM = 256
VMEM_LIMIT_BYTES = 56 * 1024 * 1024

MOBA_HEAD_DIM = 64
MOBA_HEADS = 8
MOBA_BLOCK = 256
MOBA_TOPK = 3
MOBA_ONES_ROWS = 16
MOBA_PREFETCH = 4
MOBA_PAIRS_PER_STEP = 2
ROPE_THETA = 500000.0
ROPE_DIM = MOBA_HEAD_DIM // 4

RET_HEADS = 4
RET_QK_DIM = 64
RET_V_DIM = 128
RET_THETA = 10000.0
RET_CHUNK = 256

CROSS_HEADS = 4

IN_PROJ_ROWS = 1024
IN_PROJ_PARTS = 2
MIX_ROWS = 1024
MIX_PARTS = 2
FFN_ROWS = 1024
FFN_PARTS = 4
FFN_CHUNK = 1536


def _dot(a, b):
    return jnp.dot(a, b, preferred_element_type=F32)


def _wdot(a, w):
    return jnp.dot(a, w.astype(BF16), preferred_element_type=F32)


def _dot_nt(a, b):
    return lax.dot_general(a, b, (((1,), (1,)), ((), ())), preferred_element_type=F32)


def _dot_tn(a, b):
    return lax.dot_general(a, b, (((0,), (0,)), ((), ())), preferred_element_type=F32)


def _rms(x, g, scale=None):
    inv = lax.rsqrt(jnp.mean(x * x, axis=-1, keepdims=True) + NORM_EPS)
    if scale is not None:
        inv = inv * scale
    return x * inv * g


def _params(n_grid_dims):
    return pltpu.CompilerParams(
        dimension_semantics=("arbitrary",) * n_grid_dims,
        vmem_limit_bytes=VMEM_LIMIT_BYTES,
    )


def _rotary_tables(seq, inv_freq, rot_dim, head_dim, scale):
    half = rot_dim // 2
    ang = np.arange(seq, dtype=np.float64)[:, None] * inv_freq[None, :]
    cos, sin = np.cos(ang), np.sin(ang)
    pad = head_dim - rot_dim
    a = np.concatenate([cos, cos, np.ones((seq, pad))], axis=1)
    bm = np.concatenate([-sin, np.zeros((seq, half + pad))], axis=1)
    bp = np.concatenate([np.zeros((seq, half)), sin, np.zeros((seq, pad))], axis=1)
    reps = LANES // head_dim
    tabs = np.stack([np.tile(t, (1, reps)) for t in (a, bm, bp)], axis=0)
    return jnp.asarray((tabs * scale).astype(np.float32))


def _rotate(acc, tab_ref, rows, half):
    a, bm, bp = tab_ref[0, rows, :], tab_ref[1, rows, :], tab_ref[2, rows, :]
    outs = []
    for c in range(acc.shape[1] // LANES):
        xs = acc[:, c * LANES:(c + 1) * LANES]
        outs.append(xs * a + pltpu.roll(xs, LANES - half, 1) * bm + pltpu.roll(xs, half, 1) * bp)
    return jnp.concatenate(outs, axis=1)


def _in_proj_kernel(x_ref, g_ref, w_ref, tmq_ref, tmk_ref, trq_ref, trk_ref, o_ref, *, sections):
    rows = x_ref.shape[0] // IN_PROJ_PARTS
    sl = [slice(r * rows, (r + 1) * rows) for r in range(IN_PROJ_PARTS)]
    h = [_rms(x_ref[s, :], g_ref[...]).astype(BF16) for s in sl]
    tabs = {"mq": (tmq_ref, ROPE_DIM // 2), "mk": (tmk_ref, ROPE_DIM // 2),
            "rq": (trq_ref, RET_QK_DIM // 2), "rk": (trk_ref, RET_QK_DIM // 2)}
    for kind, c0, c1 in sections:
        for r in range(IN_PROJ_PARTS):
            acc = _wdot(h[r], w_ref[:, c0:c1])
            if kind in tabs:
                tab_ref, half = tabs[kind]
                acc = _rotate(acc, tab_ref, sl[r], half)
            o_ref[sl[r], c0:c1] = acc.astype(BF16)


def _in_proj(x2d, g, w, tabs, seq, sections):
    t, d = x2d.shape
    n = w.shape[1]
    tm = IN_PROJ_ROWS
    tiles_per_seq = seq // tm
    tab_spec = pl.BlockSpec((3, tm, LANES), lambda i: (0, i % tiles_per_seq, 0))
    return pl.pallas_call(
        functools.partial(_in_proj_kernel, sections=sections),
        grid=(t // tm,),
        in_specs=[
            pl.BlockSpec((tm, d), lambda i: (i, 0)),
            pl.BlockSpec((1, d), lambda i: (0, 0)),
            pl.BlockSpec((d, n), lambda i: (0, 0), pipeline_mode=pl.Buffered(1)),
            tab_spec, tab_spec, tab_spec, tab_spec,
        ],
        out_specs=pl.BlockSpec((tm, n), lambda i: (i, 0)),
        out_shape=jax.ShapeDtypeStruct((t, n), BF16),
        compiler_params=_params(1),
        name="in_proj",
    )(x2d, g, w, *tabs)


def _moba_kernel(q_ref, k_ref, v_ref, *refs, n_weights):
    w_refs, o_ref, wb_refs = refs[:n_weights], refs[n_weights], refs[n_weights + 1:2 * n_weights + 1]
    qp_ref, vt_ref, masked_ref = refs[2 * n_weights + 1:]
    for w_ref, wb_ref in zip(w_refs, wb_refs):
        wb_ref[...] = w_ref[...].astype(BF16)
    seq = q_ref.shape[0]
    blk = MOBA_BLOCK
    nb = seq // blk
    hd = MOBA_HEAD_DIM
    lane = lax.broadcasted_iota(jnp.int32, (seq, LANES), 1)
    lane8 = lax.broadcasted_iota(jnp.int32, (nb, LANES), 1)
    jidx = lax.broadcasted_iota(jnp.int32, (nb, seq), 0)
    qblk = lax.broadcasted_iota(jnp.int32, (nb, seq), 1) // blk
    ones = jnp.ones((MOBA_ONES_ROWS, seq), F32)

    def prepare(pp):
        ln = slice(pp * LANES, (pp + 1) * LANES)
        q2 = q_ref[:, ln]
        kmean2 = jnp.mean(k_ref[:, ln].astype(F32).reshape(nb, blk, LANES), axis=1)
        km = jnp.concatenate([jnp.where(lane8 < hd, kmean2, 0.0), jnp.where(lane8 >= hd, kmean2, 0.0)], axis=0)
        km_hi = km.astype(BF16)
        km_lo = (km - km_hi.astype(F32)).astype(BF16)
        gate2 = _dot_nt(jnp.concatenate([km_hi, km_lo], axis=0), q2)
        v2t = v_ref[:, ln].astype(F32).T
        for hh in range(2):
            h = 2 * pp + hh
            gate = gate2[hh * nb:(hh + 1) * nb] + gate2[(2 + hh) * nb:(3 + hh) * nb]
            rank = jnp.zeros((nb, seq), jnp.int32)
            for i in range(nb):
                gi = gate[i:i + 1, :]
                beats = ((gi > gate) | ((gi == gate) & (i < jidx))) & (i < qblk)
                rank = rank + beats.astype(jnp.int32)
            masked_ref[h] = ((jidx >= qblk) | (rank >= MOBA_TOPK)).astype(F32)
            mine = (lane >= hh * hd) & (lane < (hh + 1) * hd)
            qp_ref[h] = jnp.where(mine, q2, jnp.zeros_like(q2))
            vt_ref[h] = jnp.concatenate([v2t[hh * hd:(hh + 1) * hd], ones], axis=0).astype(BF16)

    kr = lax.broadcasted_iota(jnp.int32, (blk, blk), 0)
    qcol = lax.broadcasted_iota(jnp.int32, (blk, blk), 1)

    def scores(t):
        pp, c, hh, j = tiles[t]
        s = _dot_nt(k_ref[j * blk:(j + 1) * blk, pp * LANES:(pp + 1) * LANES],
                    qp_ref[2 * pp + hh, c * blk:(c + 1) * blk, :])
        return jnp.where(kr <= qcol, s, NEG_INF) if j == c else s

    tiles = [(pp, c, hh, j) for pp in range(MOBA_PAIRS_PER_STEP) for c in range(nb)
             for hh in range(2) for j in range(c + 1)]
    for pp in range(MOBA_PAIRS_PER_STEP):
        prepare(pp)
    pending = {t: scores(t) for t in range(min(MOBA_PREFETCH, len(tiles)))}
    res = [None, None]
    for t, (pp, c, hh, j) in enumerate(tiles):
        r0, r1 = c * blk, (c + 1) * blk
        if t + MOBA_PREFETCH < len(tiles):
            pending[t + MOBA_PREFETCH] = scores(t + MOBA_PREFETCH)
        s_cur = pending.pop(t)
        h = 2 * pp + hh
        m_t = jnp.max(s_cur, axis=0, keepdims=True)
        if j < c:
            off = masked_ref[h, j:j + 1, r0:r1] > 0.5
            m_t = jnp.where(off, NEG_INF, m_t)
        m_new = m_t if j == 0 else jnp.maximum(m_run, m_t)
        shift = jnp.where(off, -NEG_INF, m_new) if j < c else m_new
        pv = _dot(vt_ref[h, :, j * blk:(j + 1) * blk], jnp.exp2(s_cur - shift).astype(BF16))
        acc = pv if j == 0 else acc * jnp.exp2(m_run - m_new) + pv
        m_run = m_new
        if j == c:
            res[hh] = acc[0:hd, :] * (1.0 / acc[hd:hd + 1, :])
            if hh == 1:
                o_ref[r0:r1, pp * LANES:(pp + 1) * LANES] = jnp.concatenate(res, axis=0).T.astype(BF16)


def _moba(proj, batch, seq, q_col, k_col, v_col, weights):
    t = proj.shape[0]
    width = MOBA_PAIRS_PER_STEP * LANES
    pairs = MOBA_HEADS * MOBA_HEAD_DIM // width
    n_heads = 2 * MOBA_PAIRS_PER_STEP
    steps = batch * pairs

    def spec(col0):
        assert col0 % width == 0
        return pl.BlockSpec((seq, width), lambda b, p: (b, col0 // width + p))

    def weight_spec(w):
        share = 1
        while (w.shape[0] * share) % (steps * BF16_SUBLANES):
            share *= 2
        rows = w.shape[0] * share // steps
        return pl.BlockSpec((rows, w.shape[1]), lambda b, p, share=share: ((b * pairs + p) // share, 0))

    w_specs = [weight_spec(w) for w in weights]
    outs = pl.pallas_call(
        functools.partial(_moba_kernel, n_weights=len(weights)),
        grid=(batch, pairs),
        in_specs=[spec(q_col), spec(k_col), spec(v_col)] + w_specs,
        out_specs=[pl.BlockSpec((seq, width), lambda b, p: (b, p))] + w_specs,
        out_shape=[jax.ShapeDtypeStruct((t, pairs * width), BF16)]
        + [jax.ShapeDtypeStruct(w.shape, BF16) for w in weights],
        scratch_shapes=[pltpu.VMEM((n_heads, seq, LANES), BF16),
                        pltpu.VMEM((n_heads, MOBA_HEAD_DIM + MOBA_ONES_ROWS, seq), BF16),
                        pltpu.VMEM((n_heads, seq // MOBA_BLOCK, seq), F32)],
        compiler_params=_params(2),
        name="moba",
    )(proj, proj, proj, *weights)
    return outs[0], outs[1:]


_RET_LOG_G = [math.log(1.0 - 2.0 ** (-5.0 - h)) for h in range(RET_HEADS)]


def _retention_kernel(q_ref, k_ref, v_ref, g_ref, mem_ref, g_mem_ref, w_ckv_ref, o_ref, kv_ref):
    kv_ref[...] = _wdot(_rms(mem_ref[...], g_mem_ref[...]).astype(BF16), w_ckv_ref[...]).astype(BF16)
    seq = q_ref.shape[0]
    ch = RET_CHUNK
    dv = RET_V_DIM
    lane = lax.broadcasted_iota(jnp.int32, (ch, LANES), 1)
    rowf = lax.broadcasted_iota(jnp.int32, (ch, LANES), 0).astype(F32)
    ri = lax.broadcasted_iota(jnp.int32, (ch, ch), 0)
    ci = lax.broadcasted_iota(jnp.int32, (ch, ch), 1)
    diff = (ri - ci).astype(F32)
    heads = []
    for h in range(RET_HEADS):
        log_g = _RET_LOG_G[h]
        hh = h % 2
        heads.append(dict(
            inner_decay=jnp.where(diff >= 0, jnp.exp(log_g * jnp.maximum(diff, 0.0)), 0.0),
            q_decay=jnp.exp(log_g * (rowf + 1.0)),
            k_decay=jnp.exp(log_g * (ch - 1.0 - rowf)),
            chunk_decay=math.exp(log_g * ch),
            mine=(lane >= hh * RET_QK_DIM) & (lane < (hh + 1) * RET_QK_DIM),
            qk=slice((h // 2) * LANES, (h // 2 + 1) * LANES),
            v=slice(h * dv, (h + 1) * dv),
            state=jnp.zeros((LANES, dv), F32),
        ))
    for n in range(seq // ch):
        r0, r1 = n * ch, (n + 1) * ch
        for hd in heads:
            qc = jnp.where(hd["mine"], q_ref[r0:r1, hd["qk"]], jnp.zeros((), BF16))
            kc = k_ref[r0:r1, hd["qk"]]
            vc = v_ref[r0:r1, hd["v"]]
            attn = _dot_nt(qc, kc) * hd["inner_decay"]
            out = _dot(attn.astype(BF16), vc)
            out = out + _dot((qc.astype(F32) * hd["q_decay"]).astype(BF16), hd["state"].astype(BF16))
            hd["state"] = (hd["state"] * hd["chunk_decay"]
                           + _dot_tn((kc.astype(F32) * hd["k_decay"]).astype(BF16), vc))
            out = out * lax.rsqrt(jnp.mean(out * out, axis=-1, keepdims=True) + NORM_EPS)
            gate = g_ref[r0:r1, hd["v"]].astype(F32)
            o_ref[r0:r1, hd["v"]] = (gate * jax.nn.sigmoid(gate) * out).astype(BF16)


def _retention(proj, batch, seq, q_col, k_col, v_col, g_col, mem2d, g_mem, w_ckv):
    t = proj.shape[0]
    wqk = RET_HEADS * RET_QK_DIM
    wv = RET_HEADS * RET_V_DIM
    n_mem = mem2d.shape[0] // batch
    d, n_kv = w_ckv.shape

    def spec(col0, width):
        assert col0 % width == 0
        return pl.BlockSpec((seq, width), lambda b: (b, col0 // width))

    return pl.pallas_call(
        _retention_kernel,
        grid=(batch,),
        in_specs=[spec(q_col, wqk), spec(k_col, wqk), spec(v_col, wv), spec(g_col, wv),
                  pl.BlockSpec((n_mem, d), lambda b: (b, 0)),
                  pl.BlockSpec((1, d), lambda b: (0, 0)),
                  pl.BlockSpec((d, n_kv), lambda b: (0, 0), pipeline_mode=pl.Buffered(1))],
        out_specs=[pl.BlockSpec((seq, wv), lambda b: (b, 0)),
                   pl.BlockSpec((n_mem, n_kv), lambda b: (b, 0))],
        out_shape=[jax.ShapeDtypeStruct((t, wv), BF16),
                   jax.ShapeDtypeStruct((batch * n_mem, n_kv), BF16)],
        compiler_params=_params(1),
        name="retention",
    )(proj, proj, proj, proj, mem2d, g_mem, w_ckv)


def _mix_cross_kernel(mo_ref, ro_ref, x_ref, kv_ref, wo_ref, wq_ref, wc_ref,
                      g_mix_ref, g_pre_ref, g_post_ref, o_ref):
    d = x_ref.shape[1]
    half = mo_ref.shape[1]
    dc = d // CROSS_HEADS
    rows = x_ref.shape[0] // MIX_PARTS
    sl = [slice(r * rows, (r + 1) * rows) for r in range(MIX_PARTS)]
    mix = [_wdot(mo_ref[s, :], wo_ref[0:half, :]) + _wdot(ro_ref[s, :], wo_ref[half:, :]) for s in sl]
    x1 = [x_ref[s, :] + _rms(m, g_mix_ref[...]) for s, m in zip(sl, mix)]
    h = [_rms(x, g_pre_ref[...], scale=dc ** -0.5).astype(BF16) for x in x1]
    cq = [_wdot(hh, wq_ref[...]).astype(BF16) for hh in h]
    att = []
    for r in range(MIX_PARTS):
        heads = []
        for hc in range(CROSS_HEADS):
            c0, c1 = hc * dc, (hc + 1) * dc
            s = _dot_nt(cq[r][:, c0:c1], kv_ref[:, c0:c1])
            e = jnp.exp(s - jnp.max(s, axis=1, keepdims=True))
            o = _dot(e.astype(BF16), kv_ref[:, d + c0:d + c1])
            heads.append((o * (1.0 / jnp.sum(e, axis=1, keepdims=True))).astype(BF16))
        att.append(jnp.concatenate(heads, axis=1))
    c = [_wdot(a, wc_ref[...]) for a in att]
    for r in range(MIX_PARTS):
        o_ref[sl[r], :] = x1[r] + _rms(c[r], g_post_ref[...])


def _mix_cross(mo, ro, x2d, kv, w_out, w_cq, w_co, g_mix, g_pre, g_post, seq, n_mem):
    t, d = x2d.shape
    tm = MIX_ROWS
    tiles_per_seq = seq // tm
    half = mo.shape[1]
    full = lambda i: (0, 0)
    weight = pl.BlockSpec((d, d), full, pipeline_mode=pl.Buffered(1))
    return pl.pallas_call(
        _mix_cross_kernel,
        grid=(t // tm,),
        in_specs=[
            pl.BlockSpec((tm, half), lambda i: (i, 0)),
            pl.BlockSpec((tm, half), lambda i: (i, 0)),
            pl.BlockSpec((tm, d), lambda i: (i, 0)),
            pl.BlockSpec((n_mem, 2 * d), lambda i: (i // tiles_per_seq, 0)),
            weight, weight, weight,
            pl.BlockSpec((1, d), full), pl.BlockSpec((1, d), full), pl.BlockSpec((1, d), full),
        ],
        out_specs=pl.BlockSpec((tm, d), lambda i: (i, 0)),
        out_shape=jax.ShapeDtypeStruct((t, d), F32),
        compiler_params=_params(1),
        name="mix_cross",
    )(mo, ro, x2d, kv, w_out, w_cq, w_co, g_mix, g_pre, g_post)


def _ffn_kernel(x_ref, wgu_ref, wd_ref, g_pre_ref, g_post_ref, o_ref, *, d_ff, bounds):
    rows = x_ref.shape[0] // FFN_PARTS
    xs = [x_ref[r * rows:(r + 1) * rows, :] for r in range(FFN_PARTS)]
    hs = [_rms(x, g_pre_ref[...]).astype(BF16) for x in xs]
    fs = [None] * FFN_PARTS
    for c0, c1 in bounds:
        for r in range(FFN_PARTS):
            gate = _wdot(hs[r], wgu_ref[:, c0:c1])
            up = _wdot(hs[r], wgu_ref[:, d_ff + c0:d_ff + c1])
            act = (gate * jax.nn.sigmoid(gate) * up).astype(BF16)
            down = _wdot(act, wd_ref[c0:c1, :])
            fs[r] = down if fs[r] is None else fs[r] + down
    for r in range(FFN_PARTS):
        o_ref[r * rows:(r + 1) * rows, :] = xs[r] + _rms(fs[r], g_post_ref[...])


def _ffn(x2d, w_gate_up, w_down, g_pre, g_post):
    t, d = x2d.shape
    d_ff = w_down.shape[0]
    tm = FFN_ROWS
    assert d_ff % MXU_DIM == 0 and FFN_CHUNK % MXU_DIM == 0
    edges = list(range(0, d_ff, FFN_CHUNK)) + [d_ff]
    bounds = tuple(zip(edges[:-1], edges[1:]))
    full = lambda i: (0, 0)
    return pl.pallas_call(
        functools.partial(_ffn_kernel, d_ff=d_ff, bounds=bounds),
        grid=(t // tm,),
        in_specs=[
            pl.BlockSpec((tm, d), lambda i: (i, 0)),
            pl.BlockSpec((d, 2 * d_ff), full, pipeline_mode=pl.Buffered(1)),
            pl.BlockSpec((d_ff, d), full, pipeline_mode=pl.Buffered(1)),
            pl.BlockSpec((1, d), full), pl.BlockSpec((1, d), full),
        ],
        out_specs=pl.BlockSpec((tm, d), lambda i: (i, 0)),
        out_shape=jax.ShapeDtypeStruct((t, d), F32),
        compiler_params=_params(1),
        name="ffn",
    )(x2d, w_gate_up, w_down, g_pre, g_post)


def kernel(x, mem, g_pre_mix, w_in, w_out, g_post_mix, g_pre_cross, g_mem, w_cq, w_ckv, w_co,
           g_post_cross, g_pre_ffn, w_gate_up, w_down, g_post_ffn):
    batch, seq, d = x.shape
    n_mem = mem.shape[1]
    depth = w_in.shape[0]
    moba_w = MOBA_HEADS * MOBA_HEAD_DIM
    ret_qk_w = RET_HEADS * RET_QK_DIM
    ret_v_w = RET_HEADS * RET_V_DIM
    c_mq, c_mk, c_mv = 0, moba_w, 2 * moba_w
    c_rq = 3 * moba_w
    c_rk = c_rq + ret_qk_w
    c_rv = c_rk + ret_qk_w
    c_rg = c_rv + ret_v_w
    n_proj = c_rg + ret_v_w
    assert w_in.shape[2] == n_proj
    sections = (("mq", c_mq, c_mk), ("mk", c_mk, c_mv), ("mv", c_mv, c_rq), ("rq", c_rq, c_rk),
                ("rk", c_rk, c_rv), ("rv", c_rv, c_rg), ("rg", c_rg, n_proj))

    moba_inv = np.power(ROPE_THETA, -np.arange(ROPE_DIM // 2, dtype=np.float64) * 2.0 / ROPE_DIM)
    ret_inv = 1.0 / np.power(RET_THETA, np.linspace(0.0, 1.0, RET_QK_DIM // 2))
    tabs = (
        _rotary_tables(seq, moba_inv, ROPE_DIM, MOBA_HEAD_DIM, MOBA_HEAD_DIM ** -0.5 * math.log2(math.e)),
        _rotary_tables(seq, moba_inv, ROPE_DIM, MOBA_HEAD_DIM, 1.0),
        _rotary_tables(seq, ret_inv, RET_QK_DIM, RET_QK_DIM, 1.0),
        _rotary_tables(seq, ret_inv, RET_QK_DIM, RET_QK_DIM, RET_QK_DIM ** -0.5),
    )

    xf = x.reshape(batch * seq, d)
    mem2d = mem.reshape(batch * n_mem, d)
    row = lambda g: g.reshape(1, d)
    for l in range(depth):
        proj = _in_proj(xf, row(g_pre_mix[l]), w_in[l], tabs, seq, sections)
        mo, (b_ckv, b_out, b_cq, b_co, b_gate_up, b_down) = _moba(
            proj, batch, seq, c_mq, c_mk, c_mv,
            (w_ckv[l], w_out[l], w_cq[l], w_co[l], w_gate_up[l], w_down[l]))
        ro, kv = _retention(proj, batch, seq, c_rq, c_rk, c_rv, c_rg, mem2d, row(g_mem[l]), b_ckv)
        x2 = _mix_cross(mo, ro, xf, kv, b_out, b_cq, b_co, row(g_post_mix[l]),
                        row(g_pre_cross[l]), row(g_post_cross[l]), seq, n_mem)
        xf = _ffn(x2, b_gate_up, b_down, row(g_pre_ffn[l]), row(g_post_ffn[l]))
    return xf.reshape(batch, seq, d)
```

```python
import functools
import math

import jax
import jax.numpy as jnp
import numpy as np
from jax import lax
from jax.experimental import pallas as pl
from jax.experimental.pallas import tpu as pltpu

F32 = jnp.float32
BF16 = jnp.bfloat16

NORM_EPS = 1e-6
NEG_INF = -1e30

LANES = 128
BF16_SUBLANES = 16
MXU_DIM = 256
VMEM_LIMIT_BYTES = 56 * 1024 * 1024

MOBA_HEAD_DIM = 64
MOBA_HEADS = 8
MOBA_BLOCK = 256
MOBA_TOPK = 3
MOBA_ONES_ROWS = 16
MOBA_PREFETCH = 4
MOBA_PAIRS_PER_STEP = 2
ROPE_THETA = 500000.0
ROPE_DIM = MOBA_HEAD_DIM // 4

RET_HEADS = 4
RET_QK_DIM = 64
RET_V_DIM = 128
RET_THETA = 10000.0
RET_CHUNK = 256

CROSS_HEADS = 4

IN_PROJ_ROWS = 1024
IN_PROJ_PARTS = 2
MIX_ROWS = 1024
MIX_PARTS = 2
FFN_ROWS = 1024
FFN_PARTS = 4
FFN_CHUNK = 1536


def _dot(a, b):
    return jnp.dot(a, b, preferred_element_type=F32)


def _wdot(a, w):
    return jnp.dot(a, w.astype(BF16), preferred_element_type=F32)


def _dot_nt(a, b):
    return lax.dot_general(a, b, (((1,), (1,)), ((), ())), preferred_element_type=F32)


def _dot_tn(a, b):
    return lax.dot_general(a, b, (((0,), (0,)), ((), ())), preferred_element_type=F32)


def _rms(x, g, scale=None):
    inv = lax.rsqrt(jnp.mean(x * x, axis=-1, keepdims=True) + NORM_EPS)
    if scale is not None:
        inv = inv * scale
    return x * inv * g


def _params(n_grid_dims):
    return pltpu.CompilerParams(
        dimension_semantics=("arbitrary",) * n_grid_dims,
        vmem_limit_bytes=VMEM_LIMIT_BYTES,
    )


def _rotary_tables(seq, inv_freq, rot_dim, head_dim, scale):
    half = rot_dim // 2
    ang = np.arange(seq, dtype=np.float64)[:, None] * inv_freq[None, :]
    cos, sin = np.cos(ang), np.sin(ang)
    pad = head_dim - rot_dim
    a = np.concatenate([cos, cos, np.ones((seq, pad))], axis=1)
    bm = np.concatenate([-sin, np.zeros((seq, half + pad))], axis=1)
    bp = np.concatenate([np.zeros((seq, half)), sin, np.zeros((seq, pad))], axis=1)
    reps = LANES // head_dim
    tabs = np.stack([np.tile(t, (1, reps)) for t in (a, bm, bp)], axis=0)
    return jnp.asarray((tabs * scale).astype(np.float32))


def _rotate(acc, tab_ref, rows, half):
    a, bm, bp = tab_ref[0, rows, :], tab_ref[1, rows, :], tab_ref[2, rows, :]
    outs = []
    for c in range(acc.shape[1] // LANES):
        xs = acc[:, c * LANES:(c + 1) * LANES]
        outs.append(xs * a + pltpu.roll(xs, LANES - half, 1) * bm + pltpu.roll(xs, half, 1) * bp)
    return jnp.concatenate(outs, axis=1)


def _in_proj_kernel(x_ref, g_ref, w_ref, tmq_ref, tmk_ref, trq_ref, trk_ref, o_ref, *, sections):
    rows = x_ref.shape[0] // IN_PROJ_PARTS
    sl = [slice(r * rows, (r + 1) * rows) for r in range(IN_PROJ_PARTS)]
    h = [_rms(x_ref[s, :], g_ref[...]).astype(BF16) for s in sl]
    tabs = {"mq": (tmq_ref, ROPE_DIM // 2), "mk": (tmk_ref, ROPE_DIM // 2),
            "rq": (trq_ref, RET_QK_DIM // 2), "rk": (trk_ref, RET_QK_DIM // 2)}
    for kind, c0, c1 in sections:
        for r in range(IN_PROJ_PARTS):
            acc = _wdot(h[r], w_ref[:, c0:c1])
            if kind in tabs:
                tab_ref, half = tabs[kind]
                acc = _rotate(acc, tab_ref, sl[r], half)
            o_ref[sl[r], c0:c1] = acc.astype(BF16)


def _in_proj(x2d, g, w, tabs, seq, sections):
    t, d = x2d.shape
    n = w.shape[1]
    tm = IN_PROJ_ROWS
    tiles_per_seq = seq // tm
    tab_spec = pl.BlockSpec((3, tm, LANES), lambda i: (0, i % tiles_per_seq, 0))
    return pl.pallas_call(
        functools.partial(_in_proj_kernel, sections=sections),
        grid=(t // tm,),
        in_specs=[
            pl.BlockSpec((tm, d), lambda i: (i, 0)),
            pl.BlockSpec((1, d), lambda i: (0, 0)),
            pl.BlockSpec((d, n), lambda i: (0, 0), pipeline_mode=pl.Buffered(1)),
            tab_spec, tab_spec, tab_spec, tab_spec,
        ],
        out_specs=pl.BlockSpec((tm, n), lambda i: (i, 0)),
        out_shape=jax.ShapeDtypeStruct((t, n), BF16),
        compiler_params=_params(1),
        name="in_proj",
    )(x2d, g, w, *tabs)


def _moba_kernel(q_ref, k_ref, v_ref, *refs, n_weights):
    w_refs, o_ref, wb_refs = refs[:n_weights], refs[n_weights], refs[n_weights + 1:2 * n_weights + 1]
    qp_ref, vt_ref, masked_ref = refs[2 * n_weights + 1:]
    for w_ref, wb_ref in zip(w_refs, wb_refs):
        wb_ref[...] = w_ref[...].astype(BF16)
    seq = q_ref.shape[0]
    blk = MOBA_BLOCK
    nb = seq // blk
    hd = MOBA_HEAD_DIM
    lane = lax.broadcasted_iota(jnp.int32, (seq, LANES), 1)
    lane8 = lax.broadcasted_iota(jnp.int32, (nb, LANES), 1)
    jidx = lax.broadcasted_iota(jnp.int32, (nb, seq), 0)
    qblk = lax.broadcasted_iota(jnp.int32, (nb, seq), 1) // blk
    ones = jnp.ones((MOBA_ONES_ROWS, seq), F32)

    def prepare(pp):
        ln = slice(pp * LANES, (pp + 1) * LANES)
        q2 = q_ref[:, ln]
        kmean2 = jnp.mean(k_ref[:, ln].astype(F32).reshape(nb, blk, LANES), axis=1)
        km = jnp.concatenate([jnp.where(lane8 < hd, kmean2, 0.0), jnp.where(lane8 >= hd, kmean2, 0.0)], axis=0)
        km_hi = km.astype(BF16)
        km_lo = (km - km_hi.astype(F32)).astype(BF16)
        gate2 = _dot_nt(jnp.concatenate([km_hi, km_lo], axis=0), q2)
        v2t = v_ref[:, ln].astype(F32).T
        for hh in range(2):
            h = 2 * pp + hh
            gate = gate2[hh * nb:(hh + 1) * nb] + gate2[(2 + hh) * nb:(3 + hh) * nb]
            rank = jnp.zeros((nb, seq), jnp.int32)
            for i in range(nb):
                gi = gate[i:i + 1, :]
                beats = ((gi > gate) | ((gi == gate) & (i < jidx))) & (i < qblk)
                rank = rank + beats.astype(jnp.int32)
            masked_ref[h] = ((jidx >= qblk) | (rank >= MOBA_TOPK)).astype(F32)
            mine = (lane >= hh * hd) & (lane < (hh + 1) * hd)
            qp_ref[h] = jnp.where(mine, q2, jnp.zeros_like(q2))
            vt_ref[h] = jnp.concatenate([v2t[hh * hd:(hh + 1) * hd], ones], axis=0).astype(BF16)

    kr = lax.broadcasted_iota(jnp.int32, (blk, blk), 0)
    qcol = lax.broadcasted_iota(jnp.int32, (blk, blk), 1)

    def scores(t):
        pp, c, hh, j = tiles[t]
        s = _dot_nt(k_ref[j * blk:(j + 1) * blk, pp * LANES:(pp + 1) * LANES],
                    qp_ref[2 * pp + hh, c * blk:(c + 1) * blk, :])
        return jnp.where(kr <= qcol, s, NEG_INF) if j == c else s

    tiles = [(pp, c, hh, j) for pp in range(MOBA_PAIRS_PER_STEP) for c in range(nb)
             for hh in range(2) for j in range(c + 1)]
    for pp in range(MOBA_PAIRS_PER_STEP):
        prepare(pp)
    pending = {t: scores(t) for t in range(min(MOBA_PREFETCH, len(tiles)))}
    res = [None, None]
    for t, (pp, c, hh, j) in enumerate(tiles):
        r0, r1 = c * blk, (c + 1) * blk
        if t + MOBA_PREFETCH < len(tiles):
            pending[t + MOBA_PREFETCH] = scores(t + MOBA_PREFETCH)
        s_cur = pending.pop(t)
        h = 2 * pp + hh
        m_t = jnp.max(s_cur, axis=0, keepdims=True)
        if j < c:
            off = masked_ref[h, j:j + 1, r0:r1] > 0.5
            m_t = jnp.where(off, NEG_INF, m_t)
        m_new = m_t if j == 0 else jnp.maximum(m_run, m_t)
        shift = jnp.where(off, -NEG_INF, m_new) if j < c else m_new
        pv = _dot(vt_ref[h, :, j * blk:(j + 1) * blk], jnp.exp2(s_cur - shift).astype(BF16))
        acc = pv if j == 0 else acc * jnp.exp2(m_run - m_new) + pv
        m_run = m_new
        if j == c:
            res[hh] = acc[0:hd, :] * (1.0 / acc[hd:hd + 1, :])
            if hh == 1:
                o_ref[r0:r1, pp * LANES:(pp + 1) * LANES] = jnp.concatenate(res, axis=0).T.astype(BF16)


def _moba(proj, batch, seq, q_col, k_col, v_col, weights):
    t = proj.shape[0]
    width = MOBA_PAIRS_PER_STEP * LANES
    pairs = MOBA_HEADS * MOBA_HEAD_DIM // width
    n_heads = 2 * MOBA_PAIRS_PER_STEP
    steps = batch * pairs

    def spec(col0):
        assert col0 % width == 0
        return pl.BlockSpec((seq, width), lambda b, p: (b, col0 // width + p))

    def weight_spec(w):
        share = 1
        while (w.shape[0] * share) % (steps * BF16_SUBLANES):
            share *= 2
        rows = w.shape[0] * share // steps
        return pl.BlockSpec((rows, w.shape[1]), lambda b, p, share=share: ((b * pairs + p) // share, 0))

    w_specs = [weight_spec(w) for w in weights]
    outs = pl.pallas_call(
        functools.partial(_moba_kernel, n_weights=len(weights)),
        grid=(batch, pairs),
        in_specs=[spec(q_col), spec(k_col), spec(v_col)] + w_specs,
        out_specs=[pl.BlockSpec((seq, width), lambda b, p: (b, p))] + w_specs,
        out_shape=[jax.ShapeDtypeStruct((t, pairs * width), BF16)]
        + [jax.ShapeDtypeStruct(w.shape, BF16) for w in weights],
        scratch_shapes=[pltpu.VMEM((n_heads, seq, LANES), BF16),
                        pltpu.VMEM((n_heads, MOBA_HEAD_DIM + MOBA_ONES_ROWS, seq), BF16),
                        pltpu.VMEM((n_heads, seq // MOBA_BLOCK, seq), F32)],
        compiler_params=_params(2),
        name="moba",
    )(proj, proj, proj, *weights)
    return outs[0], outs[1:]


_RET_LOG_G = [math.log(1.0 - 2.0 ** (-5.0 - h)) for h in range(RET_HEADS)]


def _retention_kernel(q_ref, k_ref, v_ref, g_ref, mem_ref, g_mem_ref, w_ckv_ref, o_ref, kv_ref):
    kv_ref[...] = _wdot(_rms(mem_ref[...], g_mem_ref[...]).astype(BF16), w_ckv_ref[...]).astype(BF16)
    seq = q_ref.shape[0]
    ch = RET_CHUNK
    dv = RET_V_DIM
    lane = lax.broadcasted_iota(jnp.int32, (ch, LANES), 1)
    rowf = lax.broadcasted_iota(jnp.int32, (ch, LANES), 0).astype(F32)
    ri = lax.broadcasted_iota(jnp.int32, (ch, ch), 0)
    ci = lax.broadcasted_iota(jnp.int32, (ch, ch), 1)
    diff = (ri - ci).astype(F32)
    heads = []
    for h in range(RET_HEADS):
        log_g = _RET_LOG_G[h]
        hh = h % 2
        heads.append(dict(
            inner_decay=jnp.where(diff >= 0, jnp.exp(log_g * jnp.maximum(diff, 0.0)), 0.0),
            q_decay=jnp.exp(log_g * (rowf + 1.0)),
            k_decay=jnp.exp(log_g * (ch - 1.0 - rowf)),
            chunk_decay=math.exp(log_g * ch),
            mine=(lane >= hh * RET_QK_DIM) & (lane < (hh + 1) * RET_QK_DIM),
            qk=slice((h // 2) * LANES, (h // 2 + 1) * LANES),
            v=slice(h * dv, (h + 1) * dv),
            state=jnp.zeros((LANES, dv), F32),
        ))
    for n in range(seq // ch):
        r0, r1 = n * ch, (n + 1) * ch
        for hd in heads:
            qc = jnp.where(hd["mine"], q_ref[r0:r1, hd["qk"]], jnp.zeros((), BF16))
            kc = k_ref[r0:r1, hd["qk"]]
            vc = v_ref[r0:r1, hd["v"]]
            attn = _dot_nt(qc, kc) * hd["inner_decay"]
            out = _dot(attn.astype(BF16), vc)
            out = out + _dot(qc, hd["state"].astype(BF16)) * hd["q_decay"]
            hd["state"] = (hd["state"] * hd["chunk_decay"]
                           + _dot_tn((kc.astype(F32) * hd["k_decay"]).astype(BF16), vc))
            out = out * lax.rsqrt(jnp.mean(out * out, axis=-1, keepdims=True) + NORM_EPS)
            gate = g_ref[r0:r1, hd["v"]].astype(F32)
            o_ref[r0:r1, hd["v"]] = (gate * jax.nn.sigmoid(gate) * out).astype(BF16)


def _retention(proj, batch, seq, q_col, k_col, v_col, g_col, mem2d, g_mem, w_ckv):
    t = proj.shape[0]
    wqk = RET_HEADS * RET_QK_DIM
    wv = RET_HEADS * RET_V_DIM
    n_mem = mem2d.shape[0] // batch
    d, n_kv = w_ckv.shape

    def spec(col0, width):
        assert col0 % width == 0
        return pl.BlockSpec((seq, width), lambda b: (b, col0 // width))

    return pl.pallas_call(
        _retention_kernel,
        grid=(batch,),
        in_specs=[spec(q_col, wqk), spec(k_col, wqk), spec(v_col, wv), spec(g_col, wv),
                  pl.BlockSpec((n_mem, d), lambda b: (b, 0)),
                  pl.BlockSpec((1, d), lambda b: (0, 0)),
                  pl.BlockSpec((d, n_kv), lambda b: (0, 0), pipeline_mode=pl.Buffered(1))],
        out_specs=[pl.BlockSpec((seq, wv), lambda b: (b, 0)),
                   pl.BlockSpec((n_mem, n_kv), lambda b: (b, 0))],
        out_shape=[jax.ShapeDtypeStruct((t, wv), BF16),
                   jax.ShapeDtypeStruct((batch * n_mem, n_kv), BF16)],
        compiler_params=_params(1),
        name="retention",
    )(proj, proj, proj, proj, mem2d, g_mem, w_ckv)


def _mix_cross_kernel(mo_ref, ro_ref, x_ref, kv_ref, wo_ref, wq_ref, wc_ref,
                      g_mix_ref, g_pre_ref, g_post_ref, o_ref):
    d = x_ref.shape[1]
    half = mo_ref.shape[1]
    dc = d // CROSS_HEADS
    rows = x_ref.shape[0] // MIX_PARTS
    sl = [slice(r * rows, (r + 1) * rows) for r in range(MIX_PARTS)]
    mix = [_wdot(mo_ref[s, :], wo_ref[0:half, :]) + _wdot(ro_ref[s, :], wo_ref[half:, :]) for s in sl]
    x1 = [x_ref[s, :] + _rms(m, g_mix_ref[...]) for s, m in zip(sl, mix)]
    h = [_rms(x, g_pre_ref[...], scale=dc ** -0.5).astype(BF16) for x in x1]
    cq = [_wdot(hh, wq_ref[...]).astype(BF16) for hh in h]
    att = []
    for r in range(MIX_PARTS):
        heads = []
        for hc in range(CROSS_HEADS):
            c0, c1 = hc * dc, (hc + 1) * dc
            s = _dot_nt(cq[r][:, c0:c1], kv_ref[:, c0:c1])
            e = jnp.exp(s - jnp.max(s, axis=1, keepdims=True))
            o = _dot(e.astype(BF16), kv_ref[:, d + c0:d + c1])
            heads.append((o * (1.0 / jnp.sum(e, axis=1, keepdims=True))).astype(BF16))
        att.append(jnp.concatenate(heads, axis=1))
    c = [_wdot(a, wc_ref[...]) for a in att]
    for r in range(MIX_PARTS):
        o_ref[sl[r], :] = x1[r] + _rms(c[r], g_post_ref[...])


def _mix_cross(mo, ro, x2d, kv, w_out, w_cq, w_co, g_mix, g_pre, g_post, seq, n_mem):
    t, d = x2d.shape
    tm = MIX_ROWS
    tiles_per_seq = seq // tm
    half = mo.shape[1]
    full = lambda i: (0, 0)
    weight = pl.BlockSpec((d, d), full, pipeline_mode=pl.Buffered(1))
    return pl.pallas_call(
        _mix_cross_kernel,
        grid=(t // tm,),
        in_specs=[
            pl.BlockSpec((tm, half), lambda i: (i, 0)),
            pl.BlockSpec((tm, half), lambda i: (i, 0)),
            pl.BlockSpec((tm, d), lambda i: (i, 0)),
            pl.BlockSpec((n_mem, 2 * d), lambda i: (i // tiles_per_seq, 0)),
            weight, weight, weight,
            pl.BlockSpec((1, d), full), pl.BlockSpec((1, d), full), pl.BlockSpec((1, d), full),
        ],
        out_specs=pl.BlockSpec((tm, d), lambda i: (i, 0)),
        out_shape=jax.ShapeDtypeStruct((t, d), F32),
        compiler_params=_params(1),
        name="mix_cross",
    )(mo, ro, x2d, kv, w_out, w_cq, w_co, g_mix, g_pre, g_post)


def _ffn_kernel(x_ref, wgu_ref, wd_ref, g_pre_ref, g_post_ref, o_ref, *, d_ff, bounds):
    rows = x_ref.shape[0] // FFN_PARTS
    xs = [x_ref[r * rows:(r + 1) * rows, :] for r in range(FFN_PARTS)]
    hs = [_rms(x, g_pre_ref[...]).astype(BF16) for x in xs]
    fs = [None] * FFN_PARTS
    for c0, c1 in bounds:
        for r in range(FFN_PARTS):
            gate = _wdot(hs[r], wgu_ref[:, c0:c1])
            up = _wdot(hs[r], wgu_ref[:, d_ff + c0:d_ff + c1])
            act = (gate * jax.nn.sigmoid(gate) * up).astype(BF16)
            down = _wdot(act, wd_ref[c0:c1, :])
            fs[r] = down if fs[r] is None else fs[r] + down
    for r in range(FFN_PARTS):
        o_ref[r * rows:(r + 1) * rows, :] = xs[r] + _rms(fs[r], g_post_ref[...])


def _ffn(x2d, w_gate_up, w_down, g_pre, g_post):
    t, d = x2d.shape
    d_ff = w_down.shape[0]
    tm = FFN_ROWS
    assert d_ff % MXU_DIM == 0 and FFN_CHUNK % MXU_DIM == 0
    edges = list(range(0, d_ff, FFN_CHUNK)) + [d_ff]
    bounds = tuple(zip(edges[:-1], edges[1:]))
    full = lambda i: (0, 0)
    return pl.pallas_call(
        functools.partial(_ffn_kernel, d_ff=d_ff, bounds=bounds),
        grid=(t // tm,),
        in_specs=[
            pl.BlockSpec((tm, d), lambda i: (i, 0)),
            pl.BlockSpec((d, 2 * d_ff), full, pipeline_mode=pl.Buffered(1)),
            pl.BlockSpec((d_ff, d), full, pipeline_mode=pl.Buffered(1)),
            pl.BlockSpec((1, d), full), pl.BlockSpec((1, d), full),
        ],
        out_specs=pl.BlockSpec((tm, d), lambda i: (i, 0)),
        out_shape=jax.ShapeDtypeStruct((t, d), F32),
        compiler_params=_params(1),
        name="ffn",
    )(x2d, w_gate_up, w_down, g_pre, g_post)


def kernel(x, mem, g_pre_mix, w_in, w_out, g_post_mix, g_pre_cross, g_mem, w_cq, w_ckv, w_co,
           g_post_cross, g_pre_ffn, w_gate_up, w_down, g_post_ffn):
    batch, seq, d = x.shape
    n_mem = mem.shape[1]
    depth = w_in.shape[0]
    moba_w = MOBA_HEADS * MOBA_HEAD_DIM
    ret_qk_w = RET_HEADS * RET_QK_DIM
    ret_v_w = RET_HEADS * RET_V_DIM
    c_mq, c_mk, c_mv = 0, moba_w, 2 * moba_w
    c_rq = 3 * moba_w
    c_rk = c_rq + ret_qk_w
    c_rv = c_rk + ret_qk_w
    c_rg = c_rv + ret_v_w
    n_proj = c_rg + ret_v_w
    assert w_in.shape[2] == n_proj
    sections = (("mq", c_mq, c_mk), ("mk", c_mk, c_mv), ("mv", c_mv, c_rq), ("rq", c_rq, c_rk),
                ("rk", c_rk, c_rv), ("rv", c_rv, c_rg), ("rg", c_rg, n_proj))

    moba_inv = np.power(ROPE_THETA, -np.arange(ROPE_DIM // 2, dtype=np.float64) * 2.0 / ROPE_DIM)
    ret_inv = 1.0 / np.power(RET_THETA, np.linspace(0.0, 1.0, RET_QK_DIM // 2))
    tabs = (
        _rotary_tables(seq, moba_inv, ROPE_DIM, MOBA_HEAD_DIM, MOBA_HEAD_DIM ** -0.5 * math.log2(math.e)),
        _rotary_tables(seq, moba_inv, ROPE_DIM, MOBA_HEAD_DIM, 1.0),
        _rotary_tables(seq, ret_inv, RET_QK_DIM, RET_QK_DIM, 1.0),
        _rotary_tables(seq, ret_inv, RET_QK_DIM, RET_QK_DIM, RET_QK_DIM ** -0.5),
    )

    xf = x.reshape(batch * seq, d)
    mem2d = mem.reshape(batch * n_mem, d)
    row = lambda g: g.reshape(1, d)
    for l in range(depth):
        proj = _in_proj(xf, row(g_pre_mix[l]), w_in[l], tabs, seq, sections)
        mo, (b_ckv, b_out, b_cq, b_co, b_gate_up, b_down) = _moba(
            proj, batch, seq, c_mq, c_mk, c_mv,
            (w_ckv[l], w_out[l], w_cq[l], w_co[l], w_gate_up[l], w_down[l]))
        ro, kv = _retention(proj, batch, seq, c_rq, c_rk, c_rv, c_rg, mem2d, row(g_mem[l]), b_ckv)
        x2 = _mix_cross(mo, ro, xf, kv, b_out, b_cq, b_co, row(g_post_mix[l]),
                        row(g_pre_cross[l]), row(g_post_cross[l]), seq, n_mem)
        xf = _ffn(x2, b_gate_up, b_down, row(g_pre_ffn[l]), row(g_post_ffn[l]))
    return xf.reshape(batch, seq, d)
```

```python
import functools
import math

import jax
import jax.numpy as jnp
import numpy as np
from jax import lax
from jax.experimental import pallas as pl
from jax.experimental.pallas import tpu as pltpu

F32 = jnp.float32
BF16 = jnp.bfloat16

NORM_EPS = 1e-6
NEG_INF = -1e30

LANES = 128
BF16_SUBLANES = 16
MXU_DIM = 256
VMEM_LIMIT_BYTES = 56 * 1024 * 1024

MOBA_HEAD_DIM = 64
MOBA_HEADS = 8
MOBA_BLOCK = 256
MOBA_TOPK = 3
MOBA_ONES_ROWS = 16
MOBA_PREFETCH = 4
MOBA_PAIRS_PER_STEP = 2
ROPE_THETA = 500000.0
ROPE_DIM = MOBA_HEAD_DIM // 4

RET_HEADS = 4
RET_QK_DIM = 64
RET_V_DIM = 128
RET_THETA = 10000.0
RET_CHUNK = 256

CROSS_HEADS = 4

IN_PROJ_ROWS = 1024
IN_PROJ_PARTS = 2
MIX_ROWS = 1024
MIX_PARTS = 2
FFN_ROWS = 1024
FFN_PARTS = 4
FFN_CHUNK = 1536


def _dot(a, b):
    return jnp.dot(a, b, preferred_element_type=F32)


def _wdot(a, w):
    return jnp.dot(a, w.astype(BF16), preferred_element_type=F32)


def _dot_nt(a, b):
    return lax.dot_general(a, b, (((1,), (1,)), ((), ())), preferred_element_type=F32)


def _dot_tn(a, b):
    return lax.dot_general(a, b, (((0,), (0,)), ((), ())), preferred_element_type=F32)


def _rms(x, g, scale=None):
    inv = lax.rsqrt(jnp.mean(x * x, axis=-1, keepdims=True) + NORM_EPS)
    if scale is not None:
        inv = inv * scale
    return x * inv * g


def _silu(x):
    h = 0.5 * x
    return h + h * jnp.tanh(h)


def _params(n_grid_dims):
    return pltpu.CompilerParams(
        dimension_semantics=("arbitrary",) * n_grid_dims,
        vmem_limit_bytes=VMEM_LIMIT_BYTES,
    )


def _rotary_tables(seq, inv_freq, rot_dim, head_dim, scale):
    half = rot_dim // 2
    ang = np.arange(seq, dtype=np.float64)[:, None] * inv_freq[None, :]
    cos, sin = np.cos(ang), np.sin(ang)
    pad = head_dim - rot_dim
    a = np.concatenate([cos, cos, np.ones((seq, pad))], axis=1)
    bm = np.concatenate([-sin, np.zeros((seq, half + pad))], axis=1)
    bp = np.concatenate([np.zeros((seq, half)), sin, np.zeros((seq, pad))], axis=1)
    reps = LANES // head_dim
    tabs = np.stack([np.tile(t, (1, reps)) for t in (a, bm, bp)], axis=0)
    return jnp.asarray((tabs * scale).astype(np.float32))


def _rotate(acc, tab_ref, rows, half):
    a, bm, bp = tab_ref[0, rows, :], tab_ref[1, rows, :], tab_ref[2, rows, :]
    outs = []
    for c in range(acc.shape[1] // LANES):
        xs = acc[:, c * LANES:(c + 1) * LANES]
        outs.append(xs * a + pltpu.roll(xs, LANES - half, 1) * bm + pltpu.roll(xs, half, 1) * bp)
    return jnp.concatenate(outs, axis=1)


def _in_proj_kernel(x_ref, g_ref, w_ref, tmq_ref, tmk_ref, trq_ref, trk_ref, o_ref, *, sections):
    rows = x_ref.shape[0] // IN_PROJ_PARTS
    sl = [slice(r * rows, (r + 1) * rows) for r in range(IN_PROJ_PARTS)]
    h = [_rms(x_ref[s, :], g_ref[...]).astype(BF16) for s in sl]
    tabs = {"mq": (tmq_ref, ROPE_DIM // 2), "mk": (tmk_ref, ROPE_DIM // 2),
            "rq": (trq_ref, RET_QK_DIM // 2), "rk": (trk_ref, RET_QK_DIM // 2)}
    for kind, c0, c1 in sections:
        for r in range(IN_PROJ_PARTS):
            acc = _wdot(h[r], w_ref[:, c0:c1])
            if kind in tabs:
                tab_ref, half = tabs[kind]
                acc = _rotate(acc, tab_ref, sl[r], half)
            o_ref[sl[r], c0:c1] = acc.astype(BF16)


def _in_proj(x2d, g, w, tabs, seq, sections):
    t, d = x2d.shape
    n = w.shape[1]
    tm = IN_PROJ_ROWS
    tiles_per_seq = seq // tm
    tab_spec = pl.BlockSpec((3, tm, LANES), lambda i: (0, i % tiles_per_seq, 0))
    return pl.pallas_call(
        functools.partial(_in_proj_kernel, sections=sections),
        grid=(t // tm,),
        in_specs=[
            pl.BlockSpec((tm, d), lambda i: (i, 0)),
            pl.BlockSpec((1, d), lambda i: (0, 0)),
            pl.BlockSpec((d, n), lambda i: (0, 0), pipeline_mode=pl.Buffered(1)),
            tab_spec, tab_spec, tab_spec, tab_spec,
        ],
        out_specs=pl.BlockSpec((tm, n), lambda i: (i, 0)),
        out_shape=jax.ShapeDtypeStruct((t, n), BF16),
        compiler_params=_params(1),
        name="in_proj",
    )(x2d, g, w, *tabs)


def _moba_kernel(q_ref, k_ref, v_ref, *refs, n_weights):
    w_refs, o_ref, wb_refs = refs[:n_weights], refs[n_weights], refs[n_weights + 1:2 * n_weights + 1]
    qp_ref, vt_ref, masked_ref = refs[2 * n_weights + 1:]
    for w_ref, wb_ref in zip(w_refs, wb_refs):
        wb_ref[...] = w_ref[...].astype(BF16)
    seq = q_ref.shape[0]
    blk = MOBA_BLOCK
    nb = seq // blk
    hd = MOBA_HEAD_DIM
    lane = lax.broadcasted_iota(jnp.int32, (seq, LANES), 1)
    lane8 = lax.broadcasted_iota(jnp.int32, (nb, LANES), 1)
    jidx = lax.broadcasted_iota(jnp.int32, (nb, seq), 0)
    qblk = lax.broadcasted_iota(jnp.int32, (nb, seq), 1) // blk
    ones = jnp.ones((MOBA_ONES_ROWS, seq), F32)

    def prepare(pp):
        ln = slice(pp * LANES, (pp + 1) * LANES)
        q2 = q_ref[:, ln]
        kmean2 = jnp.mean(k_ref[:, ln].astype(F32).reshape(nb, blk, LANES), axis=1)
        km = jnp.concatenate([jnp.where(lane8 < hd, kmean2, 0.0), jnp.where(lane8 >= hd, kmean2, 0.0)], axis=0)
        km_hi = km.astype(BF16)
        km_lo = (km - km_hi.astype(F32)).astype(BF16)
        gate2 = _dot_nt(jnp.concatenate([km_hi, km_lo], axis=0), q2)
        v2t = v_ref[:, ln].astype(F32).T
        for hh in range(2):
            h = 2 * pp + hh
            gate = gate2[hh * nb:(hh + 1) * nb] + gate2[(2 + hh) * nb:(3 + hh) * nb]
            rank = jnp.zeros((nb, seq), jnp.int32)
            for i in range(nb):
                gi = gate[i:i + 1, :]
                beats = ((gi > gate) | ((gi == gate) & (i < jidx))) & (i < qblk)
                rank = rank + beats.astype(jnp.int32)
            masked_ref[h] = ((jidx >= qblk) | (rank >= MOBA_TOPK)).astype(F32)
            mine = (lane >= hh * hd) & (lane < (hh + 1) * hd)
            qp_ref[h] = jnp.where(mine, q2, jnp.zeros_like(q2))
            vt_ref[h] = jnp.concatenate([v2t[hh * hd:(hh + 1) * hd], ones], axis=0).astype(BF16)

    kr = lax.broadcasted_iota(jnp.int32, (blk, blk), 0)
    qcol = lax.broadcasted_iota(jnp.int32, (blk, blk), 1)

    def scores(t):
        pp, c, hh, j = tiles[t]
        s = _dot_nt(k_ref[j * blk:(j + 1) * blk, pp * LANES:(pp + 1) * LANES],
                    qp_ref[2 * pp + hh, c * blk:(c + 1) * blk, :])
        return jnp.where(kr <= qcol, s, NEG_INF) if j == c else s

    tiles = [(pp, c, hh, j) for pp in range(MOBA_PAIRS_PER_STEP) for c in range(nb)
             for hh in range(2) for j in range(c + 1)]
    for pp in range(MOBA_PAIRS_PER_STEP):
        prepare(pp)
    pending = {t: scores(t) for t in range(min(MOBA_PREFETCH, len(tiles)))}
    res = [None, None]
    for t, (pp, c, hh, j) in enumerate(tiles):
        r0, r1 = c * blk, (c + 1) * blk
        if t + MOBA_PREFETCH < len(tiles):
            pending[t + MOBA_PREFETCH] = scores(t + MOBA_PREFETCH)
        s_cur = pending.pop(t)
        h = 2 * pp + hh
        m_t = jnp.max(s_cur, axis=0, keepdims=True)
        if j < c:
            off = masked_ref[h, j:j + 1, r0:r1] > 0.5
            m_t = jnp.where(off, NEG_INF, m_t)
        m_new = m_t if j == 0 else jnp.maximum(m_run, m_t)
        shift = jnp.where(off, -NEG_INF, m_new) if j < c else m_new
        pv = _dot(vt_ref[h, :, j * blk:(j + 1) * blk], jnp.exp2(s_cur - shift).astype(BF16))
        acc = pv if j == 0 else acc * jnp.exp2(m_run - m_new) + pv
        m_run = m_new
        if j == c:
            res[hh] = acc[0:hd, :] * (1.0 / acc[hd:hd + 1, :])
            if hh == 1:
                o_ref[r0:r1, pp * LANES:(pp + 1) * LANES] = jnp.concatenate(res, axis=0).T.astype(BF16)


def _moba(proj, batch, seq, q_col, k_col, v_col, weights):
    t = proj.shape[0]
    width = MOBA_PAIRS_PER_STEP * LANES
    pairs = MOBA_HEADS * MOBA_HEAD_DIM // width
    n_heads = 2 * MOBA_PAIRS_PER_STEP
    steps = batch * pairs

    def spec(col0):
        assert col0 % width == 0
        return pl.BlockSpec((seq, width), lambda b, p: (b, col0 // width + p))

    def weight_spec(w):
        share = 1
        while (w.shape[0] * share) % (steps * BF16_SUBLANES):
            share *= 2
        rows = w.shape[0] * share // steps
        return pl.BlockSpec((rows, w.shape[1]), lambda b, p, share=share: ((b * pairs + p) // share, 0))

    w_specs = [weight_spec(w) for w in weights]
    outs = pl.pallas_call(
        functools.partial(_moba_kernel, n_weights=len(weights)),
        grid=(batch, pairs),
        in_specs=[spec(q_col), spec(k_col), spec(v_col)] + w_specs,
        out_specs=[pl.BlockSpec((seq, width), lambda b, p: (b, p))] + w_specs,
        out_shape=[jax.ShapeDtypeStruct((t, pairs * width), BF16)]
        + [jax.ShapeDtypeStruct(w.shape, BF16) for w in weights],
        scratch_shapes=[pltpu.VMEM((n_heads, seq, LANES), BF16),
                        pltpu.VMEM((n_heads, MOBA_HEAD_DIM + MOBA_ONES_ROWS, seq), BF16),
                        pltpu.VMEM((n_heads, seq // MOBA_BLOCK, seq), F32)],
        compiler_params=_params(2),
        name="moba",
    )(proj, proj, proj, *weights)
    return outs[0], outs[1:]


_RET_LOG_G = [math.log(1.0 - 2.0 ** (-5.0 - h)) for h in range(RET_HEADS)]


def _retention_kernel(q_ref, k_ref, v_ref, g_ref, mem_ref, g_mem_ref, w_ckv_ref, o_ref, kv_ref):
    kv_ref[...] = _wdot(_rms(mem_ref[...], g_mem_ref[...]).astype(BF16), w_ckv_ref[...]).astype(BF16)
    seq = q_ref.shape[0]
    ch = RET_CHUNK
    dv = RET_V_DIM
    lane = lax.broadcasted_iota(jnp.int32, (ch, LANES), 1)
    rowf = lax.broadcasted_iota(jnp.int32, (ch, LANES), 0).astype(F32)
    ri = lax.broadcasted_iota(jnp.int32, (ch, ch), 0)
    ci = lax.broadcasted_iota(jnp.int32, (ch, ch), 1)
    diff = (ri - ci).astype(F32)
    heads = []
    for h in range(RET_HEADS):
        log_g = _RET_LOG_G[h]
        hh = h % 2
        heads.append(dict(
            inner_decay=jnp.where(diff >= 0, jnp.exp(log_g * jnp.maximum(diff, 0.0)), 0.0),
            q_decay=jnp.exp(log_g * (rowf + 1.0)),
            chunk_decay=math.exp(log_g * ch),
            mine=(lane >= hh * RET_QK_DIM) & (lane < (hh + 1) * RET_QK_DIM),
            qk=slice((h // 2) * LANES, (h // 2 + 1) * LANES),
            v=slice(h * dv, (h + 1) * dv),
            state=jnp.zeros((LANES, dv), F32),
        ))
    k_decay = [jnp.exp(jnp.where(lane < RET_QK_DIM, _RET_LOG_G[2 * p], _RET_LOG_G[2 * p + 1])
                       * (ch - 1.0 - rowf)) for p in range(RET_HEADS // 2)]
    for n in range(seq // ch):
        r0, r1 = n * ch, (n + 1) * ch
        k_scaled = [(k_ref[r0:r1, p * LANES:(p + 1) * LANES].astype(F32) * k_decay[p]).astype(BF16)
                    for p in range(RET_HEADS // 2)]
        for h, hd in enumerate(heads):
            qc = jnp.where(hd["mine"], q_ref[r0:r1, hd["qk"]], jnp.zeros((), BF16))
            kc = k_ref[r0:r1, hd["qk"]]
            vc = v_ref[r0:r1, hd["v"]]
            attn = _dot_nt(qc, kc) * hd["inner_decay"]
            out = _dot(attn.astype(BF16), vc)
            out = out + _dot(qc, hd["state"].astype(BF16)) * hd["q_decay"]
            hd["state"] = hd["state"] * hd["chunk_decay"] + _dot_tn(k_scaled[h // 2], vc)
            out = out * lax.rsqrt(jnp.mean(out * out, axis=-1, keepdims=True) + NORM_EPS)
            gate = g_ref[r0:r1, hd["v"]].astype(F32)
            o_ref[r0:r1, hd["v"]] = (_silu(gate) * out).astype(BF16)


def _retention(proj, batch, seq, q_col, k_col, v_col, g_col, mem2d, g_mem, w_ckv):
    t = proj.shape[0]
    wqk = RET_HEADS * RET_QK_DIM
    wv = RET_HEADS * RET_V_DIM
    n_mem = mem2d.shape[0] // batch
    d, n_kv = w_ckv.shape

    def spec(col0, width):
        assert col0 % width == 0
        return pl.BlockSpec((seq, width), lambda b: (b, col0 // width))

    return pl.pallas_call(
        _retention_kernel,
        grid=(batch,),
        in_specs=[spec(q_col, wqk), spec(k_col, wqk), spec(v_col, wv), spec(g_col, wv),
                  pl.BlockSpec((n_mem, d), lambda b: (b, 0)),
                  pl.BlockSpec((1, d), lambda b: (0, 0)),
                  pl.BlockSpec((d, n_kv), lambda b: (0, 0), pipeline_mode=pl.Buffered(1))],
        out_specs=[pl.BlockSpec((seq, wv), lambda b: (b, 0)),
                   pl.BlockSpec((n_mem, n_kv), lambda b: (b, 0))],
        out_shape=[jax.ShapeDtypeStruct((t, wv), BF16),
                   jax.ShapeDtypeStruct((batch * n_mem, n_kv), BF16)],
        compiler_params=_params(1),
        name="retention",
    )(proj, proj, proj, proj, mem2d, g_mem, w_ckv)


def _mix_cross_kernel(mo_ref, ro_ref, x_ref, kv_ref, wo_ref, wq_ref, wc_ref,
                      g_mix_ref, g_pre_ref, g_post_ref, o_ref):
    d = x_ref.shape[1]
    half = mo_ref.shape[1]
    dc = d // CROSS_HEADS
    rows = x_ref.shape[0] // MIX_PARTS
    sl = [slice(r * rows, (r + 1) * rows) for r in range(MIX_PARTS)]
    mix = [_wdot(mo_ref[s, :], wo_ref[0:half, :]) + _wdot(ro_ref[s, :], wo_ref[half:, :]) for s in sl]
    x1 = [x_ref[s, :] + _rms(m, g_mix_ref[...]) for s, m in zip(sl, mix)]
    h = [_rms(x, g_pre_ref[...], scale=dc ** -0.5).astype(BF16) for x in x1]
    cq = [_wdot(hh, wq_ref[...]).astype(BF16) for hh in h]
    att = []
    for r in range(MIX_PARTS):
        heads = []
        for hc in range(CROSS_HEADS):
            c0, c1 = hc * dc, (hc + 1) * dc
            s = _dot_nt(cq[r][:, c0:c1], kv_ref[:, c0:c1])
            e = jnp.exp(s - jnp.max(s, axis=1, keepdims=True))
            o = _dot(e.astype(BF16), kv_ref[:, d + c0:d + c1])
            heads.append((o * (1.0 / jnp.sum(e, axis=1, keepdims=True))).astype(BF16))
        att.append(jnp.concatenate(heads, axis=1))
    c = [_wdot(a, wc_ref[...]) for a in att]
    for r in range(MIX_PARTS):
        o_ref[sl[r], :] = x1[r] + _rms(c[r], g_post_ref[...])


def _mix_cross(mo, ro, x2d, kv, w_out, w_cq, w_co, g_mix, g_pre, g_post, seq, n_mem):
    t, d = x2d.shape
    tm = MIX_ROWS
    tiles_per_seq = seq // tm
    half = mo.shape[1]
    full = lambda i: (0, 0)
    weight = pl.BlockSpec((d, d), full, pipeline_mode=pl.Buffered(1))
    return pl.pallas_call(
        _mix_cross_kernel,
        grid=(t // tm,),
        in_specs=[
            pl.BlockSpec((tm, half), lambda i: (i, 0)),
            pl.BlockSpec((tm, half), lambda i: (i, 0)),
            pl.BlockSpec((tm, d), lambda i: (i, 0)),
            pl.BlockSpec((n_mem, 2 * d), lambda i: (i // tiles_per_seq, 0)),
            weight, weight, weight,
            pl.BlockSpec((1, d), full), pl.BlockSpec((1, d), full), pl.BlockSpec((1, d), full),
        ],
        out_specs=pl.BlockSpec((tm, d), lambda i: (i, 0)),
        out_shape=jax.ShapeDtypeStruct((t, d), F32),
        compiler_params=_params(1),
        name="mix_cross",
    )(mo, ro, x2d, kv, w_out, w_cq, w_co, g_mix, g_pre, g_post)


def _ffn_kernel(x_ref, wgu_ref, wd_ref, g_pre_ref, g_post_ref, o_ref, *, d_ff, bounds):
    rows = x_ref.shape[0] // FFN_PARTS
    xs = [x_ref[r * rows:(r + 1) * rows, :] for r in range(FFN_PARTS)]
    hs = [_rms(x, g_pre_ref[...]).astype(BF16) for x in xs]
    fs = [None] * FFN_PARTS
    for c0, c1 in bounds:
        for r in range(FFN_PARTS):
            gate = _wdot(hs[r], wgu_ref[:, c0:c1])
            up = _wdot(hs[r], wgu_ref[:, d_ff + c0:d_ff + c1])
            act = (_silu(gate) * up).astype(BF16)
            down = _wdot(act, wd_ref[c0:c1, :])
            fs[r] = down if fs[r] is None else fs[r] + down
    for r in range(FFN_PARTS):
        o_ref[r * rows:(r + 1) * rows, :] = xs[r] + _rms(fs[r], g_post_ref[...])


def _ffn(x2d, w_gate_up, w_down, g_pre, g_post):
    t, d = x2d.shape
    d_ff = w_down.shape[0]
    tm = FFN_ROWS
    assert d_ff % MXU_DIM == 0 and FFN_CHUNK % MXU_DIM == 0
    edges = list(range(0, d_ff, FFN_CHUNK)) + [d_ff]
    bounds = tuple(zip(edges[:-1], edges[1:]))
    full = lambda i: (0, 0)
    return pl.pallas_call(
        functools.partial(_ffn_kernel, d_ff=d_ff, bounds=bounds),
        grid=(t // tm,),
        in_specs=[
            pl.BlockSpec((tm, d), lambda i: (i, 0)),
            pl.BlockSpec((d, 2 * d_ff), full, pipeline_mode=pl.Buffered(1)),
            pl.BlockSpec((d_ff, d), full, pipeline_mode=pl.Buffered(1)),
            pl.BlockSpec((1, d), full), pl.BlockSpec((1, d), full),
        ],
        out_specs=pl.BlockSpec((tm, d), lambda i: (i, 0)),
        out_shape=jax.ShapeDtypeStruct((t, d), F32),
        compiler_params=_params(1),
        name="ffn",
    )(x2d, w_gate_up, w_down, g_pre, g_post)


def kernel(x, mem, g_pre_mix, w_in, w_out, g_post_mix, g_pre_cross, g_mem, w_cq, w_ckv, w_co,
           g_post_cross, g_pre_ffn, w_gate_up, w_down, g_post_ffn):
    batch, seq, d = x.shape
    n_mem = mem.shape[1]
    depth = w_in.shape[0]
    moba_w = MOBA_HEADS * MOBA_HEAD_DIM
    ret_qk_w = RET_HEADS * RET_QK_DIM
    ret_v_w = RET_HEADS * RET_V_DIM
    c_mq, c_mk, c_mv = 0, moba_w, 2 * moba_w
    c_rq = 3 * moba_w
    c_rk = c_rq + ret_qk_w
    c_rv = c_rk + ret_qk_w
    c_rg = c_rv + ret_v_w
    n_proj = c_rg + ret_v_w
    assert w_in.shape[2] == n_proj
    sections = (("mq", c_mq, c_mk), ("mk", c_mk, c_mv), ("mv", c_mv, c_rq), ("rq", c_rq, c_rk),
                ("rk", c_rk, c_rv), ("rv", c_rv, c_rg), ("rg", c_rg, n_proj))

    moba_inv = np.power(ROPE_THETA, -np.arange(ROPE_DIM // 2, dtype=np.float64) * 2.0 / ROPE_DIM)
    ret_inv = 1.0 / np.power(RET_THETA, np.linspace(0.0, 1.0, RET_QK_DIM // 2))
    tabs = (
        _rotary_tables(seq, moba_inv, ROPE_DIM, MOBA_HEAD_DIM, MOBA_HEAD_DIM ** -0.5 * math.log2(math.e)),
        _rotary_tables(seq, moba_inv, ROPE_DIM, MOBA_HEAD_DIM, 1.0),
        _rotary_tables(seq, ret_inv, RET_QK_DIM, RET_QK_DIM, 1.0),
        _rotary_tables(seq, ret_inv, RET_QK_DIM, RET_QK_DIM, RET_QK_DIM ** -0.5),
    )

    xf = x.reshape(batch * seq, d)
    mem2d = mem.reshape(batch * n_mem, d)
    row = lambda g: g.reshape(1, d)
    for l in range(depth):
        proj = _in_proj(xf, row(g_pre_mix[l]), w_in[l], tabs, seq, sections)
        mo, (b_ckv, b_out, b_cq, b_co, b_gate_up, b_down) = _moba(
            proj, batch, seq, c_mq, c_mk, c_mv,
            (w_ckv[l], w_out[l], w_cq[l], w_co[l], w_gate_up[l], w_down[l]))
        ro, kv = _retention(proj, batch, seq, c_rq, c_rk, c_rv, c_rg, mem2d, row(g_mem[l]), b_ckv)
        x2 = _mix_cross(mo, ro, xf, kv, b_out, b_cq, b_co, row(g_post_mix[l]),
                        row(g_pre_cross[l]), row(g_post_cross[l]), seq, n_mem)
        xf = _ffn(x2, b_gate_up, b_down, row(g_pre_ffn[l]), row(g_post_ffn[l]))
    return xf.reshape(batch, seq, d)
```

```python
import functools
import math

import jax
import jax.numpy as jnp
import numpy as np
from jax import lax
from jax.experimental import pallas as pl
from jax.experimental.pallas import tpu as pltpu

F32 = jnp.float32
BF16 = jnp.bfloat16

NORM_EPS = 1e-6
NEG_INF = -1e30

LANES = 128
BF16_SUBLANES = 16
MXU_DIM = 256
VMEM_LIMIT_BYTES = 56 * 1024 * 1024

MOBA_HEAD_DIM = 64
MOBA_HEADS = 8
MOBA_BLOCK = 256
MOBA_TOPK = 3
MOBA_ONES_ROWS = 16
MOBA_PREFETCH = 4
MOBA_PAIRS_PER_STEP = 4
ROPE_THETA = 500000.0
ROPE_DIM = MOBA_HEAD_DIM // 4

RET_HEADS = 4
RET_QK_DIM = 64
RET_V_DIM = 128
RET_THETA = 10000.0
RET_CHUNK = 256

CROSS_HEADS = 4

IN_PROJ_ROWS = 1024
IN_PROJ_PARTS = 2
MIX_ROWS = 1024
MIX_PARTS = 2
FFN_ROWS = 1024
FFN_PARTS = 4
FFN_CHUNK = 1536


def _dot(a, b):
    return jnp.dot(a, b, preferred_element_type=F32)


def _wdot(a, w):
    return jnp.dot(a, w.astype(BF16), preferred_element_type=F32)


def _dot_nt(a, b):
    return lax.dot_general(a, b, (((1,), (1,)), ((), ())), preferred_element_type=F32)


def _dot_tn(a, b):
    return lax.dot_general(a, b, (((0,), (0,)), ((), ())), preferred_element_type=F32)


def _rms(x, g, scale=None):
    inv = lax.rsqrt(jnp.mean(x * x, axis=-1, keepdims=True) + NORM_EPS)
    if scale is not None:
        inv = inv * scale
    return x * inv * g


def _silu(x):
    h = 0.5 * x
    return h + h * jnp.tanh(h)


def _params(n_grid_dims):
    return pltpu.CompilerParams(
        dimension_semantics=("arbitrary",) * n_grid_dims,
        vmem_limit_bytes=VMEM_LIMIT_BYTES,
    )


def _rotary_tables(seq, inv_freq, rot_dim, head_dim, scale):
    half = rot_dim // 2
    ang = np.arange(seq, dtype=np.float64)[:, None] * inv_freq[None, :]
    cos, sin = np.cos(ang), np.sin(ang)
    pad = head_dim - rot_dim
    a = np.concatenate([cos, cos, np.ones((seq, pad))], axis=1)
    bm = np.concatenate([-sin, np.zeros((seq, half + pad))], axis=1)
    bp = np.concatenate([np.zeros((seq, half)), sin, np.zeros((seq, pad))], axis=1)
    reps = LANES // head_dim
    tabs = np.stack([np.tile(t, (1, reps)) for t in (a, bm, bp)], axis=0)
    return jnp.asarray((tabs * scale).astype(np.float32))


def _rotate(acc, tab_ref, rows, half):
    a, bm, bp = tab_ref[0, rows, :], tab_ref[1, rows, :], tab_ref[2, rows, :]
    outs = []
    for c in range(acc.shape[1] // LANES):
        xs = acc[:, c * LANES:(c + 1) * LANES]
        outs.append(xs * a + pltpu.roll(xs, LANES - half, 1) * bm + pltpu.roll(xs, half, 1) * bp)
    return jnp.concatenate(outs, axis=1)


def _in_proj_kernel(x_ref, g_ref, w_ref, tmq_ref, tmk_ref, trq_ref, trk_ref, o_ref, *, sections):
    rows = x_ref.shape[0] // IN_PROJ_PARTS
    sl = [slice(r * rows, (r + 1) * rows) for r in range(IN_PROJ_PARTS)]
    h = [_rms(x_ref[s, :], g_ref[...]).astype(BF16) for s in sl]
    tabs = {"mq": (tmq_ref, ROPE_DIM // 2), "mk": (tmk_ref, ROPE_DIM // 2),
            "rq": (trq_ref, RET_QK_DIM // 2), "rk": (trk_ref, RET_QK_DIM // 2)}
    for kind, c0, c1 in sections:
        for r in range(IN_PROJ_PARTS):
            acc = _wdot(h[r], w_ref[:, c0:c1])
            if kind in tabs:
                tab_ref, half = tabs[kind]
                acc = _rotate(acc, tab_ref, sl[r], half)
            o_ref[sl[r], c0:c1] = acc.astype(BF16)


def _in_proj(x2d, g, w, tabs, seq, sections):
    t, d = x2d.shape
    n = w.shape[1]
    tm = IN_PROJ_ROWS
    tiles_per_seq = seq // tm
    tab_spec = pl.BlockSpec((3, tm, LANES), lambda i: (0, i % tiles_per_seq, 0))
    return pl.pallas_call(
        functools.partial(_in_proj_kernel, sections=sections),
        grid=(t // tm,),
        in_specs=[
            pl.BlockSpec((tm, d), lambda i: (i, 0)),
            pl.BlockSpec((1, d), lambda i: (0, 0)),
            pl.BlockSpec((d, n), lambda i: (0, 0), pipeline_mode=pl.Buffered(1)),
            tab_spec, tab_spec, tab_spec, tab_spec,
        ],
        out_specs=pl.BlockSpec((tm, n), lambda i: (i, 0)),
        out_shape=jax.ShapeDtypeStruct((t, n), BF16),
        compiler_params=_params(1),
        name="in_proj",
    )(x2d, g, w, *tabs)


def _moba_kernel(q_ref, k_ref, v_ref, *refs, n_weights):
    w_refs, o_ref, wb_refs = refs[:n_weights], refs[n_weights], refs[n_weights + 1:2 * n_weights + 1]
    qp_ref, vt_ref, masked_ref = refs[2 * n_weights + 1:]
    for w_ref, wb_ref in zip(w_refs, wb_refs):
        wb_ref[...] = w_ref[...].astype(BF16)
    seq = q_ref.shape[0]
    blk = MOBA_BLOCK
    nb = seq // blk
    hd = MOBA_HEAD_DIM
    lane = lax.broadcasted_iota(jnp.int32, (seq, LANES), 1)
    lane8 = lax.broadcasted_iota(jnp.int32, (nb, LANES), 1)
    jidx = lax.broadcasted_iota(jnp.int32, (nb, seq), 0)
    qblk = lax.broadcasted_iota(jnp.int32, (nb, seq), 1) // blk
    ones = jnp.ones((MOBA_ONES_ROWS, seq), F32)

    def prepare(pp):
        ln = slice(pp * LANES, (pp + 1) * LANES)
        q2 = q_ref[:, ln]
        kmean2 = jnp.mean(k_ref[:, ln].astype(F32).reshape(nb, blk, LANES), axis=1)
        km = jnp.concatenate([jnp.where(lane8 < hd, kmean2, 0.0), jnp.where(lane8 >= hd, kmean2, 0.0)], axis=0)
        km_hi = km.astype(BF16)
        km_lo = (km - km_hi.astype(F32)).astype(BF16)
        gate2 = _dot_nt(jnp.concatenate([km_hi, km_lo], axis=0), q2)
        v2t = v_ref[:, ln].astype(F32).T
        for hh in range(2):
            h = 2 * pp + hh
            gate = gate2[hh * nb:(hh + 1) * nb] + gate2[(2 + hh) * nb:(3 + hh) * nb]
            rank = jnp.zeros((nb, seq), jnp.int32)
            for i in range(nb):
                gi = gate[i:i + 1, :]
                beats = ((gi > gate) | ((gi == gate) & (i < jidx))) & (i < qblk)
                rank = rank + beats.astype(jnp.int32)
            masked_ref[h] = ((jidx >= qblk) | (rank >= MOBA_TOPK)).astype(F32)
            mine = (lane >= hh * hd) & (lane < (hh + 1) * hd)
            qp_ref[h] = jnp.where(mine, q2, jnp.zeros_like(q2))
            vt_ref[h] = jnp.concatenate([v2t[hh * hd:(hh + 1) * hd], ones], axis=0).astype(BF16)

    kr = lax.broadcasted_iota(jnp.int32, (blk, blk), 0)
    qcol = lax.broadcasted_iota(jnp.int32, (blk, blk), 1)

    def scores(t):
        pp, c, hh, j = tiles[t]
        s = _dot_nt(k_ref[j * blk:(j + 1) * blk, pp * LANES:(pp + 1) * LANES],
                    qp_ref[2 * pp + hh, c * blk:(c + 1) * blk, :])
        return jnp.where(kr <= qcol, s, NEG_INF) if j == c else s

    tiles = [(pp, c, hh, j) for pp in range(MOBA_PAIRS_PER_STEP) for c in range(nb)
             for hh in range(2) for j in range(c + 1)]
    for pp in range(MOBA_PAIRS_PER_STEP):
        prepare(pp)
    pending = {t: scores(t) for t in range(min(MOBA_PREFETCH, len(tiles)))}
    res = [None, None]
    for t, (pp, c, hh, j) in enumerate(tiles):
        r0, r1 = c * blk, (c + 1) * blk
        if t + MOBA_PREFETCH < len(tiles):
            pending[t + MOBA_PREFETCH] = scores(t + MOBA_PREFETCH)
        s_cur = pending.pop(t)
        h = 2 * pp + hh
        m_t = jnp.max(s_cur, axis=0, keepdims=True)
        if j < c:
            off = masked_ref[h, j:j + 1, r0:r1] > 0.5
            m_t = jnp.where(off, NEG_INF, m_t)
        m_new = m_t if j == 0 else jnp.maximum(m_run, m_t)
        shift = jnp.where(off, -NEG_INF, m_new) if j < c else m_new
        pv = _dot(vt_ref[h, :, j * blk:(j + 1) * blk], jnp.exp2(s_cur - shift).astype(BF16))
        acc = pv if j == 0 else acc * jnp.exp2(m_run - m_new) + pv
        m_run = m_new
        if j == c:
            res[hh] = acc[0:hd, :] * (1.0 / acc[hd:hd + 1, :])
            if hh == 1:
                o_ref[r0:r1, pp * LANES:(pp + 1) * LANES] = jnp.concatenate(res, axis=0).T.astype(BF16)


def _moba(proj, batch, seq, q_col, k_col, v_col, weights):
    t = proj.shape[0]
    width = MOBA_PAIRS_PER_STEP * LANES
    pairs = MOBA_HEADS * MOBA_HEAD_DIM // width
    n_heads = 2 * MOBA_PAIRS_PER_STEP
    steps = batch * pairs

    def spec(col0):
        assert col0 % width == 0
        return pl.BlockSpec((seq, width), lambda b, p: (b, col0 // width + p))

    def weight_spec(w):
        share = 1
        while (w.shape[0] * share) % (steps * BF16_SUBLANES):
            share *= 2
        rows = w.shape[0] * share // steps
        return pl.BlockSpec((rows, w.shape[1]), lambda b, p, share=share: ((b * pairs + p) // share, 0))

    w_specs = [weight_spec(w) for w in weights]
    outs = pl.pallas_call(
        functools.partial(_moba_kernel, n_weights=len(weights)),
        grid=(batch, pairs),
        in_specs=[spec(q_col), spec(k_col), spec(v_col)] + w_specs,
        out_specs=[pl.BlockSpec((seq, width), lambda b, p: (b, p))] + w_specs,
        out_shape=[jax.ShapeDtypeStruct((t, pairs * width), BF16)]
        + [jax.ShapeDtypeStruct(w.shape, BF16) for w in weights],
        scratch_shapes=[pltpu.VMEM((n_heads, seq, LANES), BF16),
                        pltpu.VMEM((n_heads, MOBA_HEAD_DIM + MOBA_ONES_ROWS, seq), BF16),
                        pltpu.VMEM((n_heads, seq // MOBA_BLOCK, seq), F32)],
        compiler_params=_params(2),
        name="moba",
    )(proj, proj, proj, *weights)
    return outs[0], outs[1:]


_RET_LOG_G = [math.log(1.0 - 2.0 ** (-5.0 - h)) for h in range(RET_HEADS)]


def _retention_kernel(q_ref, k_ref, v_ref, g_ref, mem_ref, g_mem_ref, w_ckv_ref, o_ref, kv_ref):
    kv_ref[...] = _wdot(_rms(mem_ref[...], g_mem_ref[...]).astype(BF16), w_ckv_ref[...]).astype(BF16)
    seq = q_ref.shape[0]
    ch = RET_CHUNK
    dv = RET_V_DIM
    lane = lax.broadcasted_iota(jnp.int32, (ch, LANES), 1)
    rowf = lax.broadcasted_iota(jnp.int32, (ch, LANES), 0).astype(F32)
    ri = lax.broadcasted_iota(jnp.int32, (ch, ch), 0)
    ci = lax.broadcasted_iota(jnp.int32, (ch, ch), 1)
    diff = (ri - ci).astype(F32)
    heads = []
    for h in range(RET_HEADS):
        log_g = _RET_LOG_G[h]
        hh = h % 2
        heads.append(dict(
            inner_decay=jnp.where(diff >= 0, jnp.exp(log_g * jnp.maximum(diff, 0.0)), 0.0),
            q_decay=jnp.exp(log_g * (rowf + 1.0)),
            chunk_decay=math.exp(log_g * ch),
            mine=(lane >= hh * RET_QK_DIM) & (lane < (hh + 1) * RET_QK_DIM),
            qk=slice((h // 2) * LANES, (h // 2 + 1) * LANES),
            v=slice(h * dv, (h + 1) * dv),
            state=jnp.zeros((LANES, dv), F32),
        ))
    k_decay = [jnp.exp(jnp.where(lane < RET_QK_DIM, _RET_LOG_G[2 * p], _RET_LOG_G[2 * p + 1])
                       * (ch - 1.0 - rowf)) for p in range(RET_HEADS // 2)]
    for n in range(seq // ch):
        r0, r1 = n * ch, (n + 1) * ch
        k_scaled = [(k_ref[r0:r1, p * LANES:(p + 1) * LANES].astype(F32) * k_decay[p]).astype(BF16)
                    for p in range(RET_HEADS // 2)]
        for h, hd in enumerate(heads):
            qc = jnp.where(hd["mine"], q_ref[r0:r1, hd["qk"]], jnp.zeros((), BF16))
            kc = k_ref[r0:r1, hd["qk"]]
            vc = v_ref[r0:r1, hd["v"]]
            attn = _dot_nt(qc, kc) * hd["inner_decay"]
            out = _dot(attn.astype(BF16), vc)
            out = out + _dot(qc, hd["state"].astype(BF16)) * hd["q_decay"]
            hd["state"] = hd["state"] * hd["chunk_decay"] + _dot_tn(k_scaled[h // 2], vc)
            out = out * lax.rsqrt(jnp.mean(out * out, axis=-1, keepdims=True) + NORM_EPS)
            gate = g_ref[r0:r1, hd["v"]].astype(F32)
            o_ref[r0:r1, hd["v"]] = (_silu(gate) * out).astype(BF16)


def _retention(proj, batch, seq, q_col, k_col, v_col, g_col, mem2d, g_mem, w_ckv):
    t = proj.shape[0]
    wqk = RET_HEADS * RET_QK_DIM
    wv = RET_HEADS * RET_V_DIM
    n_mem = mem2d.shape[0] // batch
    d, n_kv = w_ckv.shape

    def spec(col0, width):
        assert col0 % width == 0
        return pl.BlockSpec((seq, width), lambda b: (b, col0 // width))

    return pl.pallas_call(
        _retention_kernel,
        grid=(batch,),
        in_specs=[spec(q_col, wqk), spec(k_col, wqk), spec(v_col, wv), spec(g_col, wv),
                  pl.BlockSpec((n_mem, d), lambda b: (b, 0)),
                  pl.BlockSpec((1, d), lambda b: (0, 0)),
                  pl.BlockSpec((d, n_kv), lambda b: (0, 0), pipeline_mode=pl.Buffered(1))],
        out_specs=[pl.BlockSpec((seq, wv), lambda b: (b, 0)),
                   pl.BlockSpec((n_mem, n_kv), lambda b: (b, 0))],
        out_shape=[jax.ShapeDtypeStruct((t, wv), BF16),
                   jax.ShapeDtypeStruct((batch * n_mem, n_kv), BF16)],
        compiler_params=_params(1),
        name="retention",
    )(proj, proj, proj, proj, mem2d, g_mem, w_ckv)


def _mix_cross_kernel(mo_ref, ro_ref, x_ref, kv_ref, wo_ref, wq_ref, wc_ref,
                      g_mix_ref, g_pre_ref, g_post_ref, o_ref):
    d = x_ref.shape[1]
    half = mo_ref.shape[1]
    dc = d // CROSS_HEADS
    rows = x_ref.shape[0] // MIX_PARTS
    sl = [slice(r * rows, (r + 1) * rows) for r in range(MIX_PARTS)]
    mix = [_wdot(mo_ref[s, :], wo_ref[0:half, :]) + _wdot(ro_ref[s, :], wo_ref[half:, :]) for s in sl]
    x1 = [x_ref[s, :] + _rms(m, g_mix_ref[...]) for s, m in zip(sl, mix)]
    h = [_rms(x, g_pre_ref[...], scale=dc ** -0.5).astype(BF16) for x in x1]
    cq = [_wdot(hh, wq_ref[...]).astype(BF16) for hh in h]
    att = []
    for r in range(MIX_PARTS):
        heads = []
        for hc in range(CROSS_HEADS):
            c0, c1 = hc * dc, (hc + 1) * dc
            s = _dot_nt(cq[r][:, c0:c1], kv_ref[:, c0:c1])
            e = jnp.exp(s - jnp.max(s, axis=1, keepdims=True))
            o = _dot(e.astype(BF16), kv_ref[:, d + c0:d + c1])
            heads.append((o * (1.0 / jnp.sum(e, axis=1, keepdims=True))).astype(BF16))
        att.append(jnp.concatenate(heads, axis=1))
    c = [_wdot(a, wc_ref[...]) for a in att]
    for r in range(MIX_PARTS):
        o_ref[sl[r], :] = x1[r] + _rms(c[r], g_post_ref[...])


def _mix_cross(mo, ro, x2d, kv, w_out, w_cq, w_co, g_mix, g_pre, g_post, seq, n_mem):
    t, d = x2d.shape
    tm = MIX_ROWS
    tiles_per_seq = seq // tm
    half = mo.shape[1]
    full = lambda i: (0, 0)
    weight = pl.BlockSpec((d, d), full, pipeline_mode=pl.Buffered(1))
    return pl.pallas_call(
        _mix_cross_kernel,
        grid=(t // tm,),
        in_specs=[
            pl.BlockSpec((tm, half), lambda i: (i, 0)),
            pl.BlockSpec((tm, half), lambda i: (i, 0)),
            pl.BlockSpec((tm, d), lambda i: (i, 0)),
            pl.BlockSpec((n_mem, 2 * d), lambda i: (i // tiles_per_seq, 0)),
            weight, weight, weight,
            pl.BlockSpec((1, d), full), pl.BlockSpec((1, d), full), pl.BlockSpec((1, d), full),
        ],
        out_specs=pl.BlockSpec((tm, d), lambda i: (i, 0)),
        out_shape=jax.ShapeDtypeStruct((t, d), F32),
        compiler_params=_params(1),
        name="mix_cross",
    )(mo, ro, x2d, kv, w_out, w_cq, w_co, g_mix, g_pre, g_post)


def _ffn_kernel(x_ref, wgu_ref, wd_ref, g_pre_ref, g_post_ref, o_ref, *, d_ff, bounds):
    rows = x_ref.shape[0] // FFN_PARTS
    xs = [x_ref[r * rows:(r + 1) * rows, :] for r in range(FFN_PARTS)]
    hs = [_rms(x, g_pre_ref[...]).astype(BF16) for x in xs]
    fs = [None] * FFN_PARTS
    for c0, c1 in bounds:
        for r in range(FFN_PARTS):
            gate = _wdot(hs[r], wgu_ref[:, c0:c1])
            up = _wdot(hs[r], wgu_ref[:, d_ff + c0:d_ff + c1])
            act = (_silu(gate) * up).astype(BF16)
            down = _wdot(act, wd_ref[c0:c1, :])
            fs[r] = down if fs[r] is None else fs[r] + down
    for r in range(FFN_PARTS):
        o_ref[r * rows:(r + 1) * rows, :] = xs[r] + _rms(fs[r], g_post_ref[...])


def _ffn(x2d, w_gate_up, w_down, g_pre, g_post):
    t, d = x2d.shape
    d_ff = w_down.shape[0]
    tm = FFN_ROWS
    assert d_ff % MXU_DIM == 0 and FFN_CHUNK % MXU_DIM == 0
    edges = list(range(0, d_ff, FFN_CHUNK)) + [d_ff]
    bounds = tuple(zip(edges[:-1], edges[1:]))
    full = lambda i: (0, 0)
    return pl.pallas_call(
        functools.partial(_ffn_kernel, d_ff=d_ff, bounds=bounds),
        grid=(t // tm,),
        in_specs=[
            pl.BlockSpec((tm, d), lambda i: (i, 0)),
            pl.BlockSpec((d, 2 * d_ff), full, pipeline_mode=pl.Buffered(1)),
            pl.BlockSpec((d_ff, d), full, pipeline_mode=pl.Buffered(1)),
            pl.BlockSpec((1, d), full), pl.BlockSpec((1, d), full),
        ],
        out_specs=pl.BlockSpec((tm, d), lambda i: (i, 0)),
        out_shape=jax.ShapeDtypeStruct((t, d), F32),
        compiler_params=_params(1),
        name="ffn",
    )(x2d, w_gate_up, w_down, g_pre, g_post)


def kernel(x, mem, g_pre_mix, w_in, w_out, g_post_mix, g_pre_cross, g_mem, w_cq, w_ckv, w_co,
           g_post_cross, g_pre_ffn, w_gate_up, w_down, g_post_ffn):
    batch, seq, d = x.shape
    n_mem = mem.shape[1]
    depth = w_in.shape[0]
    moba_w = MOBA_HEADS * MOBA_HEAD_DIM
    ret_qk_w = RET_HEADS * RET_QK_DIM
    ret_v_w = RET_HEADS * RET_V_DIM
    c_mq, c_mk, c_mv = 0, moba_w, 2 * moba_w
    c_rq = 3 * moba_w
    c_rk = c_rq + ret_qk_w
    c_rv = c_rk + ret_qk_w
    c_rg = c_rv + ret_v_w
    n_proj = c_rg + ret_v_w
    assert w_in.shape[2] == n_proj
    sections = (("mq", c_mq, c_mk), ("mk", c_mk, c_mv), ("mv", c_mv, c_rq), ("rq", c_rq, c_rk),
                ("rk", c_rk, c_rv), ("rv", c_rv, c_rg), ("rg", c_rg, n_proj))

    moba_inv = np.power(ROPE_THETA, -np.arange(ROPE_DIM // 2, dtype=np.float64) * 2.0 / ROPE_DIM)
    ret_inv = 1.0 / np.power(RET_THETA, np.linspace(0.0, 1.0, RET_QK_DIM // 2))
    tabs = (
        _rotary_tables(seq, moba_inv, ROPE_DIM, MOBA_HEAD_DIM, MOBA_HEAD_DIM ** -0.5 * math.log2(math.e)),
        _rotary_tables(seq, moba_inv, ROPE_DIM, MOBA_HEAD_DIM, 1.0),
        _rotary_tables(seq, ret_inv, RET_QK_DIM, RET_QK_DIM, 1.0),
        _rotary_tables(seq, ret_inv, RET_QK_DIM, RET_QK_DIM, RET_QK_DIM ** -0.5),
    )

    xf = x.reshape(batch * seq, d)
    mem2d = mem.reshape(batch * n_mem, d)
    row = lambda g: g.reshape(1, d)
    for l in range(depth):
        proj = _in_proj(xf, row(g_pre_mix[l]), w_in[l], tabs, seq, sections)
        mo, (b_ckv, b_out, b_cq, b_co, b_gate_up, b_down) = _moba(
            proj, batch, seq, c_mq, c_mk, c_mv,
            (w_ckv[l], w_out[l], w_cq[l], w_co[l], w_gate_up[l], w_down[l]))
        ro, kv = _retention(proj, batch, seq, c_rq, c_rk, c_rv, c_rg, mem2d, row(g_mem[l]), b_ckv)
        x2 = _mix_cross(mo, ro, xf, kv, b_out, b_cq, b_co, row(g_post_mix[l]),
                        row(g_pre_cross[l]), row(g_post_cross[l]), seq, n_mem)
        xf = _ffn(x2, b_gate_up, b_down, row(g_pre_ffn[l]), row(g_post_ffn[l]))
    return xf.reshape(batch, seq, d)
```

```python
import functools
import math

import jax
import jax.numpy as jnp
import numpy as np
from jax import lax
from jax.experimental import pallas as pl
from jax.experimental.pallas import tpu as pltpu

F32 = jnp.float32
BF16 = jnp.bfloat16

NORM_EPS = 1e-6
NEG_INF = -1e30

LANES = 128
BF16_SUBLANES = 16
MXU_DIM = 256
VMEM_LIMIT_BYTES = 56 * 1024 * 1024

MOBA_HEAD_DIM = 64
MOBA_HEADS = 8
MOBA_BLOCK = 256
MOBA_TOPK = 3
MOBA_ONES_ROWS = 16
MOBA_PREFETCH = 4
MOBA_PAIRS_PER_STEP = 1
ROPE_THETA = 500000.0
ROPE_DIM = MOBA_HEAD_DIM // 4

RET_HEADS = 4
RET_QK_DIM = 64
RET_V_DIM = 128
RET_THETA = 10000.0
RET_CHUNK = 256

CROSS_HEADS = 4

IN_PROJ_ROWS = 1024
IN_PROJ_PARTS = 2
MIX_ROWS = 1024
MIX_PARTS = 2
FFN_ROWS = 1024
FFN_PARTS = 4
FFN_CHUNK = 1536


def _dot(a, b):
    return jnp.dot(a, b, preferred_element_type=F32)


def _wdot(a, w):
    return jnp.dot(a, w.astype(BF16), preferred_element_type=F32)


def _dot_nt(a, b):
    return lax.dot_general(a, b, (((1,), (1,)), ((), ())), preferred_element_type=F32)


def _dot_tn(a, b):
    return lax.dot_general(a, b, (((0,), (0,)), ((), ())), preferred_element_type=F32)


def _rms(x, g, scale=None):
    inv = lax.rsqrt(jnp.mean(x * x, axis=-1, keepdims=True) + NORM_EPS)
    if scale is not None:
        inv = inv * scale
    return x * inv * g


def _silu(x):
    h = 0.5 * x
    return h + h * jnp.tanh(h)


def _params(n_grid_dims):
    return pltpu.CompilerParams(
        dimension_semantics=("arbitrary",) * n_grid_dims,
        vmem_limit_bytes=VMEM_LIMIT_BYTES,
    )


def _rotary_tables(seq, inv_freq, rot_dim, head_dim, scale):
    half = rot_dim // 2
    ang = np.arange(seq, dtype=np.float64)[:, None] * inv_freq[None, :]
    cos, sin = np.cos(ang), np.sin(ang)
    pad = head_dim - rot_dim
    a = np.concatenate([cos, cos, np.ones((seq, pad))], axis=1)
    bm = np.concatenate([-sin, np.zeros((seq, half + pad))], axis=1)
    bp = np.concatenate([np.zeros((seq, half)), sin, np.zeros((seq, pad))], axis=1)
    reps = LANES // head_dim
    tabs = np.stack([np.tile(t, (1, reps)) for t in (a, bm, bp)], axis=0)
    return jnp.asarray((tabs * scale).astype(np.float32))


def _rotate(acc, tab_ref, rows, half):
    a, bm, bp = tab_ref[0, rows, :], tab_ref[1, rows, :], tab_ref[2, rows, :]
    outs = []
    for c in range(acc.shape[1] // LANES):
        xs = acc[:, c * LANES:(c + 1) * LANES]
        outs.append(xs * a + pltpu.roll(xs, LANES - half, 1) * bm + pltpu.roll(xs, half, 1) * bp)
    return jnp.concatenate(outs, axis=1)


def _in_proj_kernel(x_ref, g_ref, w_ref, tmq_ref, tmk_ref, trq_ref, trk_ref, o_ref, *, sections):
    rows = x_ref.shape[0] // IN_PROJ_PARTS
    sl = [slice(r * rows, (r + 1) * rows) for r in range(IN_PROJ_PARTS)]
    h = [_rms(x_ref[s, :], g_ref[...]).astype(BF16) for s in sl]
    tabs = {"mq": (tmq_ref, ROPE_DIM // 2), "mk": (tmk_ref, ROPE_DIM // 2),
            "rq": (trq_ref, RET_QK_DIM // 2), "rk": (trk_ref, RET_QK_DIM // 2)}
    for kind, c0, c1 in sections:
        for r in range(IN_PROJ_PARTS):
            acc = _wdot(h[r], w_ref[:, c0:c1])
            if kind in tabs:
                tab_ref, half = tabs[kind]
                acc = _rotate(acc, tab_ref, sl[r], half)
            o_ref[sl[r], c0:c1] = acc.astype(BF16)


def _in_proj(x2d, g, w, tabs, seq, sections):
    t, d = x2d.shape
    n = w.shape[1]
    tm = IN_PROJ_ROWS
    tiles_per_seq = seq // tm
    tab_spec = pl.BlockSpec((3, tm, LANES), lambda i: (0, i % tiles_per_seq, 0))
    return pl.pallas_call(
        functools.partial(_in_proj_kernel, sections=sections),
        grid=(t // tm,),
        in_specs=[
            pl.BlockSpec((tm, d), lambda i: (i, 0)),
            pl.BlockSpec((1, d), lambda i: (0, 0)),
            pl.BlockSpec((d, n), lambda i: (0, 0), pipeline_mode=pl.Buffered(1)),
            tab_spec, tab_spec, tab_spec, tab_spec,
        ],
        out_specs=pl.BlockSpec((tm, n), lambda i: (i, 0)),
        out_shape=jax.ShapeDtypeStruct((t, n), BF16),
        compiler_params=_params(1),
        name="in_proj",
    )(x2d, g, w, *tabs)


def _moba_kernel(q_ref, k_ref, v_ref, *refs, n_weights):
    w_refs, o_ref, wb_refs = refs[:n_weights], refs[n_weights], refs[n_weights + 1:2 * n_weights + 1]
    qp_ref, vt_ref, masked_ref = refs[2 * n_weights + 1:]
    for w_ref, wb_ref in zip(w_refs, wb_refs):
        wb_ref[...] = w_ref[...].astype(BF16)
    seq = q_ref.shape[0]
    blk = MOBA_BLOCK
    nb = seq // blk
    hd = MOBA_HEAD_DIM
    lane = lax.broadcasted_iota(jnp.int32, (seq, LANES), 1)
    lane8 = lax.broadcasted_iota(jnp.int32, (nb, LANES), 1)
    jidx = lax.broadcasted_iota(jnp.int32, (nb, seq), 0)
    qblk = lax.broadcasted_iota(jnp.int32, (nb, seq), 1) // blk
    ones = jnp.ones((MOBA_ONES_ROWS, seq), F32)

    def prepare(pp):
        ln = slice(pp * LANES, (pp + 1) * LANES)
        q2 = q_ref[:, ln]
        kmean2 = jnp.mean(k_ref[:, ln].astype(F32).reshape(nb, blk, LANES), axis=1)
        km = jnp.concatenate([jnp.where(lane8 < hd, kmean2, 0.0), jnp.where(lane8 >= hd, kmean2, 0.0)], axis=0)
        km_hi = km.astype(BF16)
        km_lo = (km - km_hi.astype(F32)).astype(BF16)
        gate2 = _dot_nt(jnp.concatenate([km_hi, km_lo], axis=0), q2)
        v2t = v_ref[:, ln].astype(F32).T
        for hh in range(2):
            h = 2 * pp + hh
            gate = gate2[hh * nb:(hh + 1) * nb] + gate2[(2 + hh) * nb:(3 + hh) * nb]
            rank = jnp.zeros((nb, seq), jnp.int32)
            for i in range(nb):
                gi = gate[i:i + 1, :]
                beats = ((gi > gate) | ((gi == gate) & (i < jidx))) & (i < qblk)
                rank = rank + beats.astype(jnp.int32)
            masked_ref[h] = ((jidx >= qblk) | (rank >= MOBA_TOPK)).astype(F32)
            mine = (lane >= hh * hd) & (lane < (hh + 1) * hd)
            qp_ref[h] = jnp.where(mine, q2, jnp.zeros_like(q2))
            vt_ref[h] = jnp.concatenate([v2t[hh * hd:(hh + 1) * hd], ones], axis=0).astype(BF16)

    kr = lax.broadcasted_iota(jnp.int32, (blk, blk), 0)
    qcol = lax.broadcasted_iota(jnp.int32, (blk, blk), 1)

    def scores(t):
        pp, c, hh, j = tiles[t]
        s = _dot_nt(k_ref[j * blk:(j + 1) * blk, pp * LANES:(pp + 1) * LANES],
                    qp_ref[2 * pp + hh, c * blk:(c + 1) * blk, :])
        return jnp.where(kr <= qcol, s, NEG_INF) if j == c else s

    tiles = [(pp, c, hh, j) for pp in range(MOBA_PAIRS_PER_STEP) for c in range(nb)
             for hh in range(2) for j in range(c + 1)]
    for pp in range(MOBA_PAIRS_PER_STEP):
        prepare(pp)
    pending = {t: scores(t) for t in range(min(MOBA_PREFETCH, len(tiles)))}
    res = [None, None]
    for t, (pp, c, hh, j) in enumerate(tiles):
        r0, r1 = c * blk, (c + 1) * blk
        if t + MOBA_PREFETCH < len(tiles):
            pending[t + MOBA_PREFETCH] = scores(t + MOBA_PREFETCH)
        s_cur = pending.pop(t)
        h = 2 * pp + hh
        m_t = jnp.max(s_cur, axis=0, keepdims=True)
        if j < c:
            off = masked_ref[h, j:j + 1, r0:r1] > 0.5
            m_t = jnp.where(off, NEG_INF, m_t)
        m_new = m_t if j == 0 else jnp.maximum(m_run, m_t)
        shift = jnp.where(off, -NEG_INF, m_new) if j < c else m_new
        pv = _dot(vt_ref[h, :, j * blk:(j + 1) * blk], jnp.exp2(s_cur - shift).astype(BF16))
        acc = pv if j == 0 else acc * jnp.exp2(m_run - m_new) + pv
        m_run = m_new
        if j == c:
            res[hh] = acc[0:hd, :] * (1.0 / acc[hd:hd + 1, :])
            if hh == 1:
                o_ref[r0:r1, pp * LANES:(pp + 1) * LANES] = jnp.concatenate(res, axis=0).T.astype(BF16)


def _moba(proj, batch, seq, q_col, k_col, v_col, weights):
    t = proj.shape[0]
    width = MOBA_PAIRS_PER_STEP * LANES
    pairs = MOBA_HEADS * MOBA_HEAD_DIM // width
    n_heads = 2 * MOBA_PAIRS_PER_STEP
    steps = batch * pairs

    def spec(col0):
        assert col0 % width == 0
        return pl.BlockSpec((seq, width), lambda b, p: (b, col0 // width + p))

    def weight_spec(w):
        share = 1
        while (w.shape[0] * share) % (steps * BF16_SUBLANES):
            share *= 2
        rows = w.shape[0] * share // steps
        return pl.BlockSpec((rows, w.shape[1]), lambda b, p, share=share: ((b * pairs + p) // share, 0))

    w_specs = [weight_spec(w) for w in weights]
    outs = pl.pallas_call(
        functools.partial(_moba_kernel, n_weights=len(weights)),
        grid=(batch, pairs),
        in_specs=[spec(q_col), spec(k_col), spec(v_col)] + w_specs,
        out_specs=[pl.BlockSpec((seq, width), lambda b, p: (b, p))] + w_specs,
        out_shape=[jax.ShapeDtypeStruct((t, pairs * width), BF16)]
        + [jax.ShapeDtypeStruct(w.shape, BF16) for w in weights],
        scratch_shapes=[pltpu.VMEM((n_heads, seq, LANES), BF16),
                        pltpu.VMEM((n_heads, MOBA_HEAD_DIM + MOBA_ONES_ROWS, seq), BF16),
                        pltpu.VMEM((n_heads, seq // MOBA_BLOCK, seq), F32)],
        compiler_params=_params(2),
        name="moba",
    )(proj, proj, proj, *weights)
    return outs[0], outs[1:]


_RET_LOG_G = [math.log(1.0 - 2.0 ** (-5.0 - h)) for h in range(RET_HEADS)]


def _retention_kernel(q_ref, k_ref, v_ref, g_ref, mem_ref, g_mem_ref, w_ckv_ref, o_ref, kv_ref):
    kv_ref[...] = _wdot(_rms(mem_ref[...], g_mem_ref[...]).astype(BF16), w_ckv_ref[...]).astype(BF16)
    seq = q_ref.shape[0]
    ch = RET_CHUNK
    dv = RET_V_DIM
    lane = lax.broadcasted_iota(jnp.int32, (ch, LANES), 1)
    rowf = lax.broadcasted_iota(jnp.int32, (ch, LANES), 0).astype(F32)
    ri = lax.broadcasted_iota(jnp.int32, (ch, ch), 0)
    ci = lax.broadcasted_iota(jnp.int32, (ch, ch), 1)
    diff = (ri - ci).astype(F32)
    heads = []
    for h in range(RET_HEADS):
        log_g = _RET_LOG_G[h]
        hh = h % 2
        heads.append(dict(
            inner_decay=jnp.where(diff >= 0, jnp.exp(log_g * jnp.maximum(diff, 0.0)), 0.0),
            q_decay=jnp.exp(log_g * (rowf + 1.0)),
            chunk_decay=math.exp(log_g * ch),
            mine=(lane >= hh * RET_QK_DIM) & (lane < (hh + 1) * RET_QK_DIM),
            qk=slice((h // 2) * LANES, (h // 2 + 1) * LANES),
            v=slice(h * dv, (h + 1) * dv),
            state=jnp.zeros((LANES, dv), F32),
        ))
    k_decay = [jnp.exp(jnp.where(lane < RET_QK_DIM, _RET_LOG_G[2 * p], _RET_LOG_G[2 * p + 1])
                       * (ch - 1.0 - rowf)) for p in range(RET_HEADS // 2)]
    for n in range(seq // ch):
        r0, r1 = n * ch, (n + 1) * ch
        k_scaled = [(k_ref[r0:r1, p * LANES:(p + 1) * LANES].astype(F32) * k_decay[p]).astype(BF16)
                    for p in range(RET_HEADS // 2)]
        for h, hd in enumerate(heads):
            qc = jnp.where(hd["mine"], q_ref[r0:r1, hd["qk"]], jnp.zeros((), BF16))
            kc = k_ref[r0:r1, hd["qk"]]
            vc = v_ref[r0:r1, hd["v"]]
            attn = _dot_nt(qc, kc) * hd["inner_decay"]
            out = _dot(attn.astype(BF16), vc)
            out = out + _dot(qc, hd["state"].astype(BF16)) * hd["q_decay"]
            hd["state"] = hd["state"] * hd["chunk_decay"] + _dot_tn(k_scaled[h // 2], vc)
            out = out * lax.rsqrt(jnp.mean(out * out, axis=-1, keepdims=True) + NORM_EPS)
            gate = g_ref[r0:r1, hd["v"]].astype(F32)
            o_ref[r0:r1, hd["v"]] = (_silu(gate) * out).astype(BF16)


def _retention(proj, batch, seq, q_col, k_col, v_col, g_col, mem2d, g_mem, w_ckv):
    t = proj.shape[0]
    wqk = RET_HEADS * RET_QK_DIM
    wv = RET_HEADS * RET_V_DIM
    n_mem = mem2d.shape[0] // batch
    d, n_kv = w_ckv.shape

    def spec(col0, width):
        assert col0 % width == 0
        return pl.BlockSpec((seq, width), lambda b: (b, col0 // width))

    return pl.pallas_call(
        _retention_kernel,
        grid=(batch,),
        in_specs=[spec(q_col, wqk), spec(k_col, wqk), spec(v_col, wv), spec(g_col, wv),
                  pl.BlockSpec((n_mem, d), lambda b: (b, 0)),
                  pl.BlockSpec((1, d), lambda b: (0, 0)),
                  pl.BlockSpec((d, n_kv), lambda b: (0, 0), pipeline_mode=pl.Buffered(1))],
        out_specs=[pl.BlockSpec((seq, wv), lambda b: (b, 0)),
                   pl.BlockSpec((n_mem, n_kv), lambda b: (b, 0))],
        out_shape=[jax.ShapeDtypeStruct((t, wv), BF16),
                   jax.ShapeDtypeStruct((batch * n_mem, n_kv), BF16)],
        compiler_params=_params(1),
        name="retention",
    )(proj, proj, proj, proj, mem2d, g_mem, w_ckv)


def _mix_cross_kernel(mo_ref, ro_ref, x_ref, kv_ref, wo_ref, wq_ref, wc_ref,
                      g_mix_ref, g_pre_ref, g_post_ref, o_ref):
    d = x_ref.shape[1]
    half = mo_ref.shape[1]
    dc = d // CROSS_HEADS
    rows = x_ref.shape[0] // MIX_PARTS
    sl = [slice(r * rows, (r + 1) * rows) for r in range(MIX_PARTS)]
    mix = [_wdot(mo_ref[s, :], wo_ref[0:half, :]) + _wdot(ro_ref[s, :], wo_ref[half:, :]) for s in sl]
    x1 = [x_ref[s, :] + _rms(m, g_mix_ref[...]) for s, m in zip(sl, mix)]
    h = [_rms(x, g_pre_ref[...], scale=dc ** -0.5).astype(BF16) for x in x1]
    cq = [_wdot(hh, wq_ref[...]).astype(BF16) for hh in h]
    att = []
    for r in range(MIX_PARTS):
        heads = []
        for hc in range(CROSS_HEADS):
            c0, c1 = hc * dc, (hc + 1) * dc
            s = _dot_nt(cq[r][:, c0:c1], kv_ref[:, c0:c1])
            e = jnp.exp(s - jnp.max(s, axis=1, keepdims=True))
            o = _dot(e.astype(BF16), kv_ref[:, d + c0:d + c1])
            heads.append((o * (1.0 / jnp.sum(e, axis=1, keepdims=True))).astype(BF16))
        att.append(jnp.concatenate(heads, axis=1))
    c = [_wdot(a, wc_ref[...]) for a in att]
    for r in range(MIX_PARTS):
        o_ref[sl[r], :] = x1[r] + _rms(c[r], g_post_ref[...])


def _mix_cross(mo, ro, x2d, kv, w_out, w_cq, w_co, g_mix, g_pre, g_post, seq, n_mem):
    t, d = x2d.shape
    tm = MIX_ROWS
    tiles_per_seq = seq // tm
    half = mo.shape[1]
    full = lambda i: (0, 0)
    weight = pl.BlockSpec((d, d), full, pipeline_mode=pl.Buffered(1))
    return pl.pallas_call(
        _mix_cross_kernel,
        grid=(t // tm,),
        in_specs=[
            pl.BlockSpec((tm, half), lambda i: (i, 0)),
            pl.BlockSpec((tm, half), lambda i: (i, 0)),
            pl.BlockSpec((tm, d), lambda i: (i, 0)),
            pl.BlockSpec((n_mem, 2 * d), lambda i: (i // tiles_per_seq, 0)),
            weight, weight, weight,
            pl.BlockSpec((1, d), full), pl.BlockSpec((1, d), full), pl.BlockSpec((1, d), full),
        ],
        out_specs=pl.BlockSpec((tm, d), lambda i: (i, 0)),
        out_shape=jax.ShapeDtypeStruct((t, d), F32),
        compiler_params=_params(1),
        name="mix_cross",
    )(mo, ro, x2d, kv, w_out, w_cq, w_co, g_mix, g_pre, g_post)


def _ffn_kernel(x_ref, wgu_ref, wd_ref, g_pre_ref, g_post_ref, o_ref, *, d_ff, bounds):
    rows = x_ref.shape[0] // FFN_PARTS
    xs = [x_ref[r * rows:(r + 1) * rows, :] for r in range(FFN_PARTS)]
    hs = [_rms(x, g_pre_ref[...]).astype(BF16) for x in xs]
    fs = [None] * FFN_PARTS
    for c0, c1 in bounds:
        for r in range(FFN_PARTS):
            gate = _wdot(hs[r], wgu_ref[:, c0:c1])
            up = _wdot(hs[r], wgu_ref[:, d_ff + c0:d_ff + c1])
            act = (_silu(gate) * up).astype(BF16)
            down = _wdot(act, wd_ref[c0:c1, :])
            fs[r] = down if fs[r] is None else fs[r] + down
    for r in range(FFN_PARTS):
        o_ref[r * rows:(r + 1) * rows, :] = xs[r] + _rms(fs[r], g_post_ref[...])


def _ffn(x2d, w_gate_up, w_down, g_pre, g_post):
    t, d = x2d.shape
    d_ff = w_down.shape[0]
    tm = FFN_ROWS
    assert d_ff % MXU_DIM == 0 and FFN_CHUNK % MXU_DIM == 0
    edges = list(range(0, d_ff, FFN_CHUNK)) + [d_ff]
    bounds = tuple(zip(edges[:-1], edges[1:]))
    full = lambda i: (0, 0)
    return pl.pallas_call(
        functools.partial(_ffn_kernel, d_ff=d_ff, bounds=bounds),
        grid=(t // tm,),
        in_specs=[
            pl.BlockSpec((tm, d), lambda i: (i, 0)),
            pl.BlockSpec((d, 2 * d_ff), full, pipeline_mode=pl.Buffered(1)),
            pl.BlockSpec((d_ff, d), full, pipeline_mode=pl.Buffered(1)),
            pl.BlockSpec((1, d), full), pl.BlockSpec((1, d), full),
        ],
        out_specs=pl.BlockSpec((tm, d), lambda i: (i, 0)),
        out_shape=jax.ShapeDtypeStruct((t, d), F32),
        compiler_params=_params(1),
        name="ffn",
    )(x2d, w_gate_up, w_down, g_pre, g_post)


def kernel(x, mem, g_pre_mix, w_in, w_out, g_post_mix, g_pre_cross, g_mem, w_cq, w_ckv, w_co,
           g_post_cross, g_pre_ffn, w_gate_up, w_down, g_post_ffn):
    batch, seq, d = x.shape
    n_mem = mem.shape[1]
    depth = w_in.shape[0]
    moba_w = MOBA_HEADS * MOBA_HEAD_DIM
    ret_qk_w = RET_HEADS * RET_QK_DIM
    ret_v_w = RET_HEADS * RET_V_DIM
    c_mq, c_mk, c_mv = 0, moba_w, 2 * moba_w
    c_rq = 3 * moba_w
    c_rk = c_rq + ret_qk_w
    c_rv = c_rk + ret_qk_w
    c_rg = c_rv + ret_v_w
    n_proj = c_rg + ret_v_w
    assert w_in.shape[2] == n_proj
    sections = (("mq", c_mq, c_mk), ("mk", c_mk, c_mv), ("mv", c_mv, c_rq), ("rq", c_rq, c_rk),
                ("rk", c_rk, c_rv), ("rv", c_rv, c_rg), ("rg", c_rg, n_proj))

    moba_inv = np.power(ROPE_THETA, -np.arange(ROPE_DIM // 2, dtype=np.float64) * 2.0 / ROPE_DIM)
    ret_inv = 1.0 / np.power(RET_THETA, np.linspace(0.0, 1.0, RET_QK_DIM // 2))
    tabs = (
        _rotary_tables(seq, moba_inv, ROPE_DIM, MOBA_HEAD_DIM, MOBA_HEAD_DIM ** -0.5 * math.log2(math.e)),
        _rotary_tables(seq, moba_inv, ROPE_DIM, MOBA_HEAD_DIM, 1.0),
        _rotary_tables(seq, ret_inv, RET_QK_DIM, RET_QK_DIM, 1.0),
        _rotary_tables(seq, ret_inv, RET_QK_DIM, RET_QK_DIM, RET_QK_DIM ** -0.5),
    )

    xf = x.reshape(batch * seq, d)
    mem2d = mem.reshape(batch * n_mem, d)
    row = lambda g: g.reshape(1, d)
    for l in range(depth):
        proj = _in_proj(xf, row(g_pre_mix[l]), w_in[l], tabs, seq, sections)
        mo, (b_ckv, b_out, b_cq, b_co, b_gate_up, b_down) = _moba(
            proj, batch, seq, c_mq, c_mk, c_mv,
            (w_ckv[l], w_out[l], w_cq[l], w_co[l], w_gate_up[l], w_down[l]))
        ro, kv = _retention(proj, batch, seq, c_rq, c_rk, c_rv, c_rg, mem2d, row(g_mem[l]), b_ckv)
        x2 = _mix_cross(mo, ro, xf, kv, b_out, b_cq, b_co, row(g_post_mix[l]),
                        row(g_pre_cross[l]), row(g_post_cross[l]), seq, n_mem)
        xf = _ffn(x2, b_gate_up, b_down, row(g_pre_ffn[l]), row(g_post_ffn[l]))
    return xf.reshape(batch, seq, d)
```

```python
import functools
import math

import jax
import jax.numpy as jnp
import numpy as np
from jax import lax
from jax.experimental import pallas as pl
from jax.experimental.pallas import tpu as pltpu

F32 = jnp.float32
BF16 = jnp.bfloat16

NORM_EPS = 1e-6
NEG_INF = -1e30

LANES = 128
BF16_SUBLANES = 16
MXU_DIM = 256
VMEM_LIMIT_BYTES = 56 * 1024 * 1024

MOBA_HEAD_DIM = 64
MOBA_HEADS = 8
MOBA_BLOCK = 256
MOBA_TOPK = 3
MOBA_ONES_ROWS = 16
MOBA_PREFETCH = 4
MOBA_PAIRS_PER_STEP = 2
ROPE_THETA = 500000.0
ROPE_DIM = MOBA_HEAD_DIM // 4

RET_HEADS = 4
RET_QK_DIM = 64
RET_V_DIM = 128
RET_THETA = 10000.0
RET_CHUNK = 256

CROSS_HEADS = 4

IN_PROJ_ROWS = 1024
IN_PROJ_PARTS = 2
MIX_ROWS = 1024
MIX_PARTS = 2
FFN_ROWS = 1024
FFN_PARTS = 4
FFN_CHUNK = 1536


def _dot(a, b):
    return jnp.dot(a, b, preferred_element_type=F32)


def _wdot(a, w):
    return jnp.dot(a, w.astype(BF16), preferred_element_type=F32)


def _dot_nt(a, b):
    return lax.dot_general(a, b, (((1,), (1,)), ((), ())), preferred_element_type=F32)


def _dot_tn(a, b):
    return lax.dot_general(a, b, (((0,), (0,)), ((), ())), preferred_element_type=F32)


def _rms(x, g, scale=None):
    inv = lax.rsqrt(jnp.mean(x * x, axis=-1, keepdims=True) + NORM_EPS)
    if scale is not None:
        inv = inv * scale
    return x * inv * g


def _silu(x):
    h = 0.5 * x
    return h + h * jnp.tanh(h)


def _params(n_grid_dims):
    return pltpu.CompilerParams(
        dimension_semantics=("arbitrary",) * n_grid_dims,
        vmem_limit_bytes=VMEM_LIMIT_BYTES,
    )


def _rotary_tables(seq, inv_freq, rot_dim, head_dim, scale):
    half = rot_dim // 2
    ang = np.arange(seq, dtype=np.float64)[:, None] * inv_freq[None, :]
    cos, sin = np.cos(ang), np.sin(ang)
    pad = head_dim - rot_dim
    a = np.concatenate([cos, cos, np.ones((seq, pad))], axis=1)
    bm = np.concatenate([-sin, np.zeros((seq, half + pad))], axis=1)
    bp = np.concatenate([np.zeros((seq, half)), sin, np.zeros((seq, pad))], axis=1)
    reps = LANES // head_dim
    tabs = np.stack([np.tile(t, (1, reps)) for t in (a, bm, bp)], axis=0)
    return jnp.asarray((tabs * scale).astype(np.float32))


def _rotate(acc, tab_ref, rows, half):
    a, bm, bp = tab_ref[0, rows, :], tab_ref[1, rows, :], tab_ref[2, rows, :]
    outs = []
    for c in range(acc.shape[1] // LANES):
        xs = acc[:, c * LANES:(c + 1) * LANES]
        outs.append(xs * a + pltpu.roll(xs, LANES - half, 1) * bm + pltpu.roll(xs, half, 1) * bp)
    return jnp.concatenate(outs, axis=1)


def _in_proj_kernel(x_ref, g_ref, w_ref, tmq_ref, tmk_ref, trq_ref, trk_ref, o_ref, kmean_ref, *, sections):
    rows = x_ref.shape[0] // IN_PROJ_PARTS
    blocks = rows // MOBA_BLOCK
    sl = [slice(r * rows, (r + 1) * rows) for r in range(IN_PROJ_PARTS)]
    h = [_rms(x_ref[s, :], g_ref[...]).astype(BF16) for s in sl]
    tabs = {"mq": (tmq_ref, ROPE_DIM // 2), "mk": (tmk_ref, ROPE_DIM // 2),
            "rq": (trq_ref, RET_QK_DIM // 2), "rk": (trk_ref, RET_QK_DIM // 2)}
    for kind, c0, c1 in sections:
        for r in range(IN_PROJ_PARTS):
            acc = _wdot(h[r], w_ref[:, c0:c1])
            if kind in tabs:
                tab_ref, half = tabs[kind]
                acc = _rotate(acc, tab_ref, sl[r], half)
            if kind == "mk":
                kmean_ref[r * blocks:(r + 1) * blocks, 0, :] = jnp.mean(
                    acc.reshape(blocks, MOBA_BLOCK, c1 - c0), axis=1)
            o_ref[sl[r], c0:c1] = acc.astype(BF16)


def _in_proj(x2d, g, w, tabs, seq, sections):
    t, d = x2d.shape
    n = w.shape[1]
    tm = IN_PROJ_ROWS
    tiles_per_seq = seq // tm
    tab_spec = pl.BlockSpec((3, tm, LANES), lambda i: (0, i % tiles_per_seq, 0))
    mk_width = next(c1 - c0 for kind, c0, c1 in sections if kind == "mk")
    return pl.pallas_call(
        functools.partial(_in_proj_kernel, sections=sections),
        grid=(t // tm,),
        in_specs=[
            pl.BlockSpec((tm, d), lambda i: (i, 0)),
            pl.BlockSpec((1, d), lambda i: (0, 0)),
            pl.BlockSpec((d, n), lambda i: (0, 0), pipeline_mode=pl.Buffered(1)),
            tab_spec, tab_spec, tab_spec, tab_spec,
        ],
        out_specs=[pl.BlockSpec((tm, n), lambda i: (i, 0)),
                   pl.BlockSpec((tm // MOBA_BLOCK, 1, mk_width), lambda i: (i, 0, 0))],
        out_shape=[jax.ShapeDtypeStruct((t, n), BF16),
                   jax.ShapeDtypeStruct((t // MOBA_BLOCK, 1, mk_width), F32)],
        compiler_params=_params(1),
        name="in_proj",
    )(x2d, g, w, *tabs)


def _moba_kernel(q_ref, k_ref, v_ref, km_ref, *refs, n_weights):
    w_refs, o_ref, wb_refs = refs[:n_weights], refs[n_weights], refs[n_weights + 1:2 * n_weights + 1]
    qp_ref, vt_ref, masked_ref = refs[2 * n_weights + 1:]
    for w_ref, wb_ref in zip(w_refs, wb_refs):
        wb_ref[...] = w_ref[...].astype(BF16)
    seq = q_ref.shape[0]
    blk = MOBA_BLOCK
    nb = seq // blk
    hd = MOBA_HEAD_DIM
    lane = lax.broadcasted_iota(jnp.int32, (seq, LANES), 1)
    lane8 = lax.broadcasted_iota(jnp.int32, (nb, LANES), 1)
    jidx = lax.broadcasted_iota(jnp.int32, (nb, seq), 0)
    qblk = lax.broadcasted_iota(jnp.int32, (nb, seq), 1) // blk
    ones = jnp.ones((MOBA_ONES_ROWS, seq), F32)

    def prepare(pp):
        ln = slice(pp * LANES, (pp + 1) * LANES)
        q2 = q_ref[:, ln]
        kmean2 = km_ref[:, 0, ln]
        km = jnp.concatenate([jnp.where(lane8 < hd, kmean2, 0.0), jnp.where(lane8 >= hd, kmean2, 0.0)], axis=0)
        km_hi = km.astype(BF16)
        km_lo = (km - km_hi.astype(F32)).astype(BF16)
        gate2 = _dot_nt(jnp.concatenate([km_hi, km_lo], axis=0), q2)
        v2t = v_ref[:, ln].astype(F32).T
        for hh in range(2):
            h = 2 * pp + hh
            gate = gate2[hh * nb:(hh + 1) * nb] + gate2[(2 + hh) * nb:(3 + hh) * nb]
            rank = jnp.zeros((nb, seq), jnp.int32)
            for i in range(nb):
                gi = gate[i:i + 1, :]
                beats = ((gi > gate) | ((gi == gate) & (i < jidx))) & (i < qblk)
                rank = rank + beats.astype(jnp.int32)
            masked_ref[h] = ((jidx >= qblk) | (rank >= MOBA_TOPK)).astype(F32)
            mine = (lane >= hh * hd) & (lane < (hh + 1) * hd)
            qp_ref[h] = jnp.where(mine, q2, jnp.zeros_like(q2))
            vt_ref[h] = jnp.concatenate([v2t[hh * hd:(hh + 1) * hd], ones], axis=0).astype(BF16)

    kr = lax.broadcasted_iota(jnp.int32, (blk, blk), 0)
    qcol = lax.broadcasted_iota(jnp.int32, (blk, blk), 1)

    def scores(t):
        pp, c, hh, j = tiles[t]
        s = _dot_nt(k_ref[j * blk:(j + 1) * blk, pp * LANES:(pp + 1) * LANES],
                    qp_ref[2 * pp + hh, c * blk:(c + 1) * blk, :])
        return jnp.where(kr <= qcol, s, NEG_INF) if j == c else s

    tiles = [(pp, c, hh, j) for pp in range(MOBA_PAIRS_PER_STEP) for c in range(nb)
             for hh in range(2) for j in range(c + 1)]
    for pp in range(MOBA_PAIRS_PER_STEP):
        prepare(pp)
    pending = {t: scores(t) for t in range(min(MOBA_PREFETCH, len(tiles)))}
    res = [None, None]
    for t, (pp, c, hh, j) in enumerate(tiles):
        r0, r1 = c * blk, (c + 1) * blk
        if t + MOBA_PREFETCH < len(tiles):
            pending[t + MOBA_PREFETCH] = scores(t + MOBA_PREFETCH)
        s_cur = pending.pop(t)
        h = 2 * pp + hh
        m_t = jnp.max(s_cur, axis=0, keepdims=True)
        if j < c:
            off = masked_ref[h, j:j + 1, r0:r1] > 0.5
            m_t = jnp.where(off, NEG_INF, m_t)
        m_new = m_t if j == 0 else jnp.maximum(m_run, m_t)
        shift = jnp.where(off, -NEG_INF, m_new) if j < c else m_new
        pv = _dot(vt_ref[h, :, j * blk:(j + 1) * blk], jnp.exp2(s_cur - shift).astype(BF16))
        acc = pv if j == 0 else acc * jnp.exp2(m_run - m_new) + pv
        m_run = m_new
        if j == c:
            res[hh] = acc[0:hd, :] * (1.0 / acc[hd:hd + 1, :])
            if hh == 1:
                o_ref[r0:r1, pp * LANES:(pp + 1) * LANES] = jnp.concatenate(res, axis=0).T.astype(BF16)


def _moba(proj, kmean, batch, seq, q_col, k_col, v_col, weights):
    t = proj.shape[0]
    width = MOBA_PAIRS_PER_STEP * LANES
    pairs = MOBA_HEADS * MOBA_HEAD_DIM // width
    n_heads = 2 * MOBA_PAIRS_PER_STEP
    steps = batch * pairs

    def spec(col0):
        assert col0 % width == 0
        return pl.BlockSpec((seq, width), lambda b, p: (b, col0 // width + p))

    def weight_spec(w):
        share = 1
        while (w.shape[0] * share) % (steps * BF16_SUBLANES):
            share *= 2
        rows = w.shape[0] * share // steps
        return pl.BlockSpec((rows, w.shape[1]), lambda b, p, share=share: ((b * pairs + p) // share, 0))

    w_specs = [weight_spec(w) for w in weights]
    outs = pl.pallas_call(
        functools.partial(_moba_kernel, n_weights=len(weights)),
        grid=(batch, pairs),
        in_specs=[spec(q_col), spec(k_col), spec(v_col),
                  pl.BlockSpec((seq // MOBA_BLOCK, 1, width), lambda b, p: (b, 0, p))] + w_specs,
        out_specs=[pl.BlockSpec((seq, width), lambda b, p: (b, p))] + w_specs,
        out_shape=[jax.ShapeDtypeStruct((t, pairs * width), BF16)]
        + [jax.ShapeDtypeStruct(w.shape, BF16) for w in weights],
        scratch_shapes=[pltpu.VMEM((n_heads, seq, LANES), BF16),
                        pltpu.VMEM((n_heads, MOBA_HEAD_DIM + MOBA_ONES_ROWS, seq), BF16),
                        pltpu.VMEM((n_heads, seq // MOBA_BLOCK, seq), F32)],
        compiler_params=_params(2),
        name="moba",
    )(proj, proj, proj, kmean, *weights)
    return outs[0], outs[1:]


_RET_LOG_G = [math.log(1.0 - 2.0 ** (-5.0 - h)) for h in range(RET_HEADS)]


def _retention_kernel(q_ref, k_ref, v_ref, g_ref, mem_ref, g_mem_ref, w_ckv_ref, o_ref, kv_ref):
    kv_ref[...] = _wdot(_rms(mem_ref[...], g_mem_ref[...]).astype(BF16), w_ckv_ref[...]).astype(BF16)
    seq = q_ref.shape[0]
    ch = RET_CHUNK
    dv = RET_V_DIM
    lane = lax.broadcasted_iota(jnp.int32, (ch, LANES), 1)
    rowf = lax.broadcasted_iota(jnp.int32, (ch, LANES), 0).astype(F32)
    ri = lax.broadcasted_iota(jnp.int32, (ch, ch), 0)
    ci = lax.broadcasted_iota(jnp.int32, (ch, ch), 1)
    diff = (ri - ci).astype(F32)
    heads = []
    for h in range(RET_HEADS):
        log_g = _RET_LOG_G[h]
        hh = h % 2
        heads.append(dict(
            inner_decay=jnp.where(diff >= 0, jnp.exp(log_g * jnp.maximum(diff, 0.0)), 0.0),
            q_decay=jnp.exp(log_g * (rowf + 1.0)),
            chunk_decay=math.exp(log_g * ch),
            mine=(lane >= hh * RET_QK_DIM) & (lane < (hh + 1) * RET_QK_DIM),
            qk=slice((h // 2) * LANES, (h // 2 + 1) * LANES),
            v=slice(h * dv, (h + 1) * dv),
            state=jnp.zeros((LANES, dv), F32),
        ))
    k_decay = [jnp.exp(jnp.where(lane < RET_QK_DIM, _RET_LOG_G[2 * p], _RET_LOG_G[2 * p + 1])
                       * (ch - 1.0 - rowf)) for p in range(RET_HEADS // 2)]
    for n in range(seq // ch):
        r0, r1 = n * ch, (n + 1) * ch
        k_scaled = [(k_ref[r0:r1, p * LANES:(p + 1) * LANES].astype(F32) * k_decay[p]).astype(BF16)
                    for p in range(RET_HEADS // 2)]
        for h, hd in enumerate(heads):
            qc = jnp.where(hd["mine"], q_ref[r0:r1, hd["qk"]], jnp.zeros((), BF16))
            kc = k_ref[r0:r1, hd["qk"]]
            vc = v_ref[r0:r1, hd["v"]]
            attn = _dot_nt(qc, kc) * hd["inner_decay"]
            out = _dot(attn.astype(BF16), vc)
            out = out + _dot(qc, hd["state"].astype(BF16)) * hd["q_decay"]
            hd["state"] = hd["state"] * hd["chunk_decay"] + _dot_tn(k_scaled[h // 2], vc)
            out = out * lax.rsqrt(jnp.mean(out * out, axis=-1, keepdims=True) + NORM_EPS)
            gate = g_ref[r0:r1, hd["v"]].astype(F32)
            o_ref[r0:r1, hd["v"]] = (_silu(gate) * out).astype(BF16)


def _retention(proj, batch, seq, q_col, k_col, v_col, g_col, mem2d, g_mem, w_ckv):
    t = proj.shape[0]
    wqk = RET_HEADS * RET_QK_DIM
    wv = RET_HEADS * RET_V_DIM
    n_mem = mem2d.shape[0] // batch
    d, n_kv = w_ckv.shape

    def spec(col0, width):
        assert col0 % width == 0
        return pl.BlockSpec((seq, width), lambda b: (b, col0 // width))

    return pl.pallas_call(
        _retention_kernel,
        grid=(batch,),
        in_specs=[spec(q_col, wqk), spec(k_col, wqk), spec(v_col, wv), spec(g_col, wv),
                  pl.BlockSpec((n_mem, d), lambda b: (b, 0)),
                  pl.BlockSpec((1, d), lambda b: (0, 0)),
                  pl.BlockSpec((d, n_kv), lambda b: (0, 0), pipeline_mode=pl.Buffered(1))],
        out_specs=[pl.BlockSpec((seq, wv), lambda b: (b, 0)),
                   pl.BlockSpec((n_mem, n_kv), lambda b: (b, 0))],
        out_shape=[jax.ShapeDtypeStruct((t, wv), BF16),
                   jax.ShapeDtypeStruct((batch * n_mem, n_kv), BF16)],
        compiler_params=_params(1),
        name="retention",
    )(proj, proj, proj, proj, mem2d, g_mem, w_ckv)


def _mix_cross_kernel(mo_ref, ro_ref, x_ref, kv_ref, wo_ref, wq_ref, wc_ref,
                      g_mix_ref, g_pre_ref, g_post_ref, o_ref):
    d = x_ref.shape[1]
    half = mo_ref.shape[1]
    dc = d // CROSS_HEADS
    rows = x_ref.shape[0] // MIX_PARTS
    sl = [slice(r * rows, (r + 1) * rows) for r in range(MIX_PARTS)]
    mix = [_wdot(mo_ref[s, :], wo_ref[0:half, :]) + _wdot(ro_ref[s, :], wo_ref[half:, :]) for s in sl]
    x1 = [x_ref[s, :] + _rms(m, g_mix_ref[...]) for s, m in zip(sl, mix)]
    h = [_rms(x, g_pre_ref[...], scale=dc ** -0.5).astype(BF16) for x in x1]
    cq = [_wdot(hh, wq_ref[...]).astype(BF16) for hh in h]
    att = []
    for r in range(MIX_PARTS):
        heads = []
        for hc in range(CROSS_HEADS):
            c0, c1 = hc * dc, (hc + 1) * dc
            s = _dot_nt(cq[r][:, c0:c1], kv_ref[:, c0:c1])
            e = jnp.exp(s - jnp.max(s, axis=1, keepdims=True))
            o = _dot(e.astype(BF16), kv_ref[:, d + c0:d + c1])
            heads.append((o * (1.0 / jnp.sum(e, axis=1, keepdims=True))).astype(BF16))
        att.append(jnp.concatenate(heads, axis=1))
    c = [_wdot(a, wc_ref[...]) for a in att]
    for r in range(MIX_PARTS):
        o_ref[sl[r], :] = x1[r] + _rms(c[r], g_post_ref[...])


def _mix_cross(mo, ro, x2d, kv, w_out, w_cq, w_co, g_mix, g_pre, g_post, seq, n_mem):
    t, d = x2d.shape
    tm = MIX_ROWS
    tiles_per_seq = seq // tm
    half = mo.shape[1]
    full = lambda i: (0, 0)
    weight = pl.BlockSpec((d, d), full, pipeline_mode=pl.Buffered(1))
    return pl.pallas_call(
        _mix_cross_kernel,
        grid=(t // tm,),
        in_specs=[
            pl.BlockSpec((tm, half), lambda i: (i, 0)),
            pl.BlockSpec((tm, half), lambda i: (i, 0)),
            pl.BlockSpec((tm, d), lambda i: (i, 0)),
            pl.BlockSpec((n_mem, 2 * d), lambda i: (i // tiles_per_seq, 0)),
            weight, weight, weight,
            pl.BlockSpec((1, d), full), pl.BlockSpec((1, d), full), pl.BlockSpec((1, d), full),
        ],
        out_specs=pl.BlockSpec((tm, d), lambda i: (i, 0)),
        out_shape=jax.ShapeDtypeStruct((t, d), F32),
        compiler_params=_params(1),
        name="mix_cross",
    )(mo, ro, x2d, kv, w_out, w_cq, w_co, g_mix, g_pre, g_post)


def _ffn_kernel(x_ref, wgu_ref, wd_ref, g_pre_ref, g_post_ref, o_ref, *, d_ff, bounds):
    rows = x_ref.shape[0] // FFN_PARTS
    xs = [x_ref[r * rows:(r + 1) * rows, :] for r in range(FFN_PARTS)]
    hs = [_rms(x, g_pre_ref[...]).astype(BF16) for x in xs]
    fs = [None] * FFN_PARTS
    for c0, c1 in bounds:
        for r in range(FFN_PARTS):
            gate = _wdot(hs[r], wgu_ref[:, c0:c1])
            up = _wdot(hs[r], wgu_ref[:, d_ff + c0:d_ff + c1])
            act = (_silu(gate) * up).astype(BF16)
            down = _wdot(act, wd_ref[c0:c1, :])
            fs[r] = down if fs[r] is None else fs[r] + down
    for r in range(FFN_PARTS):
        o_ref[r * rows:(r + 1) * rows, :] = xs[r] + _rms(fs[r], g_post_ref[...])


def _ffn(x2d, w_gate_up, w_down, g_pre, g_post):
    t, d = x2d.shape
    d_ff = w_down.shape[0]
    tm = FFN_ROWS
    assert d_ff % MXU_DIM == 0 and FFN_CHUNK % MXU_DIM == 0
    edges = list(range(0, d_ff, FFN_CHUNK)) + [d_ff]
    bounds = tuple(zip(edges[:-1], edges[1:]))
    full = lambda i: (0, 0)
    return pl.pallas_call(
        functools.partial(_ffn_kernel, d_ff=d_ff, bounds=bounds),
        grid=(t // tm,),
        in_specs=[
            pl.BlockSpec((tm, d), lambda i: (i, 0)),
            pl.BlockSpec((d, 2 * d_ff), full, pipeline_mode=pl.Buffered(1)),
            pl.BlockSpec((d_ff, d), full, pipeline_mode=pl.Buffered(1)),
            pl.BlockSpec((1, d), full), pl.BlockSpec((1, d), full),
        ],
        out_specs=pl.BlockSpec((tm, d), lambda i: (i, 0)),
        out_shape=jax.ShapeDtypeStruct((t, d), F32),
        compiler_params=_params(1),
        name="ffn",
    )(x2d, w_gate_up, w_down, g_pre, g_post)


def kernel(x, mem, g_pre_mix, w_in, w_out, g_post_mix, g_pre_cross, g_mem, w_cq, w_ckv, w_co,
           g_post_cross, g_pre_ffn, w_gate_up, w_down, g_post_ffn):
    batch, seq, d = x.shape
    n_mem = mem.shape[1]
    depth = w_in.shape[0]
    moba_w = MOBA_HEADS * MOBA_HEAD_DIM
    ret_qk_w = RET_HEADS * RET_QK_DIM
    ret_v_w = RET_HEADS * RET_V_DIM
    c_mq, c_mk, c_mv = 0, moba_w, 2 * moba_w
    c_rq = 3 * moba_w
    c_rk = c_rq + ret_qk_w
    c_rv = c_rk + ret_qk_w
    c_rg = c_rv + ret_v_w
    n_proj = c_rg + ret_v_w
    assert w_in.shape[2] == n_proj
    sections = (("mq", c_mq, c_mk), ("mk", c_mk, c_mv), ("mv", c_mv, c_rq), ("rq", c_rq, c_rk),
                ("rk", c_rk, c_rv), ("rv", c_rv, c_rg), ("rg", c_rg, n_proj))

    moba_inv = np.power(ROPE_THETA, -np.arange(ROPE_DIM // 2, dtype=np.float64) * 2.0 / ROPE_DIM)
    ret_inv = 1.0 / np.power(RET_THETA, np.linspace(0.0, 1.0, RET_QK_DIM // 2))
    tabs = (
        _rotary_tables(seq, moba_inv, ROPE_DIM, MOBA_HEAD_DIM, MOBA_HEAD_DIM ** -0.5 * math.log2(math.e)),
        _rotary_tables(seq, moba_inv, ROPE_DIM, MOBA_HEAD_DIM, 1.0),
        _rotary_tables(seq, ret_inv, RET_QK_DIM, RET_QK_DIM, 1.0),
        _rotary_tables(seq, ret_inv, RET_QK_DIM, RET_QK_DIM, RET_QK_DIM ** -0.5),
    )

    xf = x.reshape(batch * seq, d)
    mem2d = mem.reshape(batch * n_mem, d)
    row = lambda g: g.reshape(1, d)
    for l in range(depth):
        proj, kmean = _in_proj(xf, row(g_pre_mix[l]), w_in[l], tabs, seq, sections)
        mo, (b_ckv, b_out, b_cq, b_co, b_gate_up, b_down) = _moba(
            proj, kmean, batch, seq, c_mq, c_mk, c_mv,
            (w_ckv[l], w_out[l], w_cq[l], w_co[l], w_gate_up[l], w_down[l]))
        ro, kv = _retention(proj, batch, seq, c_rq, c_rk, c_rv, c_rg, mem2d, row(g_mem[l]), b_ckv)
        x2 = _mix_cross(mo, ro, xf, kv, b_out, b_cq, b_co, row(g_post_mix[l]),
                        row(g_pre_cross[l]), row(g_post_cross[l]), seq, n_mem)
        xf = _ffn(x2, b_gate_up, b_down, row(g_pre_ffn[l]), row(g_post_ffn[l]))
    return xf.reshape(batch, seq, d)
```

```python
import functools
import math

import jax
import jax.numpy as jnp
import numpy as np
from jax import lax
from jax.experimental import pallas as pl
from jax.experimental.pallas import tpu as pltpu

F32 = jnp.float32
BF16 = jnp.bfloat16

NORM_EPS = 1e-6
NEG_INF = -1e30

LANES = 128
BF16_SUBLANES = 16
MXU_DIM = 256
VMEM_LIMIT_BYTES = 56 * 1024 * 1024

MOBA_HEAD_DIM = 64
MOBA_HEADS = 8
MOBA_BLOCK = 256
MOBA_TOPK = 3
MOBA_ONES_ROWS = 16
MOBA_PREFETCH = 5
MOBA_PAIRS_PER_STEP = 2
ROPE_THETA = 500000.0
ROPE_DIM = MOBA_HEAD_DIM // 4

RET_HEADS = 4
RET_QK_DIM = 64
RET_V_DIM = 128
RET_THETA = 10000.0
RET_CHUNK = 256

CROSS_HEADS = 4

IN_PROJ_ROWS = 1024
IN_PROJ_PARTS = 2
MIX_ROWS = 1024
MIX_PARTS = 2
FFN_ROWS = 1024
FFN_PARTS = 4
FFN_CHUNK = 1536


def _dot(a, b):
    return jnp.dot(a, b, preferred_element_type=F32)


def _wdot(a, w):
    return jnp.dot(a, w.astype(BF16), preferred_element_type=F32)


def _dot_nt(a, b):
    return lax.dot_general(a, b, (((1,), (1,)), ((), ())), preferred_element_type=F32)


def _dot_tn(a, b):
    return lax.dot_general(a, b, (((0,), (0,)), ((), ())), preferred_element_type=F32)


def _rms(x, g, scale=None):
    inv = lax.rsqrt(jnp.mean(x * x, axis=-1, keepdims=True) + NORM_EPS)
    if scale is not None:
        inv = inv * scale
    return x * inv * g


def _silu(x):
    h = 0.5 * x
    return h + h * jnp.tanh(h)


def _params(n_grid_dims):
    return pltpu.CompilerParams(
        dimension_semantics=("arbitrary",) * n_grid_dims,
        vmem_limit_bytes=VMEM_LIMIT_BYTES,
    )


def _rotary_tables(seq, inv_freq, rot_dim, head_dim, scale):
    half = rot_dim // 2
    ang = np.arange(seq, dtype=np.float64)[:, None] * inv_freq[None, :]
    cos, sin = np.cos(ang), np.sin(ang)
    pad = head_dim - rot_dim
    a = np.concatenate([cos, cos, np.ones((seq, pad))], axis=1)
    bm = np.concatenate([-sin, np.zeros((seq, half + pad))], axis=1)
    bp = np.concatenate([np.zeros((seq, half)), sin, np.zeros((seq, pad))], axis=1)
    reps = LANES // head_dim
    tabs = np.stack([np.tile(t, (1, reps)) for t in (a, bm, bp)], axis=0)
    return jnp.asarray((tabs * scale).astype(np.float32))


def _rotate(acc, tab_ref, rows, half):
    a, bm, bp = tab_ref[0, rows, :], tab_ref[1, rows, :], tab_ref[2, rows, :]
    outs = []
    for c in range(acc.shape[1] // LANES):
        xs = acc[:, c * LANES:(c + 1) * LANES]
        outs.append(xs * a + pltpu.roll(xs, LANES - half, 1) * bm + pltpu.roll(xs, half, 1) * bp)
    return jnp.concatenate(outs, axis=1)


def _in_proj_kernel(x_ref, g_ref, w_ref, tmq_ref, tmk_ref, trq_ref, trk_ref, o_ref, kmean_ref, *, sections):
    rows = x_ref.shape[0] // IN_PROJ_PARTS
    blocks = rows // MOBA_BLOCK
    sl = [slice(r * rows, (r + 1) * rows) for r in range(IN_PROJ_PARTS)]
    h = [_rms(x_ref[s, :], g_ref[...]).astype(BF16) for s in sl]
    tabs = {"mq": (tmq_ref, ROPE_DIM // 2), "mk": (tmk_ref, ROPE_DIM // 2),
            "rq": (trq_ref, RET_QK_DIM // 2), "rk": (trk_ref, RET_QK_DIM // 2)}
    for kind, c0, c1 in sections:
        for r in range(IN_PROJ_PARTS):
            acc = _wdot(h[r], w_ref[:, c0:c1])
            if kind in tabs:
                tab_ref, half = tabs[kind]
                acc = _rotate(acc, tab_ref, sl[r], half)
            if kind == "mk":
                kmean_ref[r * blocks:(r + 1) * blocks, 0, :] = jnp.mean(
                    acc.reshape(blocks, MOBA_BLOCK, c1 - c0), axis=1)
            o_ref[sl[r], c0:c1] = acc.astype(BF16)


def _in_proj(x2d, g, w, tabs, seq, sections):
    t, d = x2d.shape
    n = w.shape[1]
    tm = IN_PROJ_ROWS
    tiles_per_seq = seq // tm
    tab_spec = pl.BlockSpec((3, tm, LANES), lambda i: (0, i % tiles_per_seq, 0))
    mk_width = next(c1 - c0 for kind, c0, c1 in sections if kind == "mk")
    return pl.pallas_call(
        functools.partial(_in_proj_kernel, sections=sections),
        grid=(t // tm,),
        in_specs=[
            pl.BlockSpec((tm, d), lambda i: (i, 0)),
            pl.BlockSpec((1, d), lambda i: (0, 0)),
            pl.BlockSpec((d, n), lambda i: (0, 0), pipeline_mode=pl.Buffered(1)),
            tab_spec, tab_spec, tab_spec, tab_spec,
        ],
        out_specs=[pl.BlockSpec((tm, n), lambda i: (i, 0)),
                   pl.BlockSpec((tm // MOBA_BLOCK, 1, mk_width), lambda i: (i, 0, 0))],
        out_shape=[jax.ShapeDtypeStruct((t, n), BF16),
                   jax.ShapeDtypeStruct((t // MOBA_BLOCK, 1, mk_width), F32)],
        compiler_params=_params(1),
        name="in_proj",
    )(x2d, g, w, *tabs)


def _moba_kernel(q_ref, k_ref, v_ref, km_ref, *refs, n_weights):
    w_refs, o_ref, wb_refs = refs[:n_weights], refs[n_weights], refs[n_weights + 1:2 * n_weights + 1]
    qp_ref, vt_ref, masked_ref = refs[2 * n_weights + 1:]
    for w_ref, wb_ref in zip(w_refs, wb_refs):
        wb_ref[...] = w_ref[...].astype(BF16)
    seq = q_ref.shape[0]
    blk = MOBA_BLOCK
    nb = seq // blk
    hd = MOBA_HEAD_DIM
    lane = lax.broadcasted_iota(jnp.int32, (seq, LANES), 1)
    lane8 = lax.broadcasted_iota(jnp.int32, (nb, LANES), 1)
    jidx = lax.broadcasted_iota(jnp.int32, (nb, seq), 0)
    qblk = lax.broadcasted_iota(jnp.int32, (nb, seq), 1) // blk
    ones = jnp.ones((MOBA_ONES_ROWS, seq), F32)

    def prepare(pp):
        ln = slice(pp * LANES, (pp + 1) * LANES)
        q2 = q_ref[:, ln]
        kmean2 = km_ref[:, 0, ln]
        km = jnp.concatenate([jnp.where(lane8 < hd, kmean2, 0.0), jnp.where(lane8 >= hd, kmean2, 0.0)], axis=0)
        km_hi = km.astype(BF16)
        km_lo = (km - km_hi.astype(F32)).astype(BF16)
        gate2 = _dot_nt(jnp.concatenate([km_hi, km_lo], axis=0), q2)
        v2t = v_ref[:, ln].astype(F32).T
        for hh in range(2):
            h = 2 * pp + hh
            gate = gate2[hh * nb:(hh + 1) * nb] + gate2[(2 + hh) * nb:(3 + hh) * nb]
            rank = jnp.zeros((nb, seq), jnp.int32)
            for i in range(nb):
                gi = gate[i:i + 1, :]
                beats = ((gi > gate) | ((gi == gate) & (i < jidx))) & (i < qblk)
                rank = rank + beats.astype(jnp.int32)
            masked_ref[h] = ((jidx >= qblk) | (rank >= MOBA_TOPK)).astype(F32)
            mine = (lane >= hh * hd) & (lane < (hh + 1) * hd)
            qp_ref[h] = jnp.where(mine, q2, jnp.zeros_like(q2))
            vt_ref[h] = jnp.concatenate([v2t[hh * hd:(hh + 1) * hd], ones], axis=0).astype(BF16)

    kr = lax.broadcasted_iota(jnp.int32, (blk, blk), 0)
    qcol = lax.broadcasted_iota(jnp.int32, (blk, blk), 1)

    def scores(t):
        pp, c, hh, j = tiles[t]
        s = _dot_nt(k_ref[j * blk:(j + 1) * blk, pp * LANES:(pp + 1) * LANES],
                    qp_ref[2 * pp + hh, c * blk:(c + 1) * blk, :])
        return jnp.where(kr <= qcol, s, NEG_INF) if j == c else s

    tiles = [(pp, c, hh, j) for pp in range(MOBA_PAIRS_PER_STEP) for c in range(nb)
             for hh in range(2) for j in range(c + 1)]
    for pp in range(MOBA_PAIRS_PER_STEP):
        prepare(pp)
    pending = {t: scores(t) for t in range(min(MOBA_PREFETCH, len(tiles)))}
    res = [None, None]
    for t, (pp, c, hh, j) in enumerate(tiles):
        r0, r1 = c * blk, (c + 1) * blk
        if t + MOBA_PREFETCH < len(tiles):
            pending[t + MOBA_PREFETCH] = scores(t + MOBA_PREFETCH)
        s_cur = pending.pop(t)
        h = 2 * pp + hh
        m_t = jnp.max(s_cur, axis=0, keepdims=True)
        if j < c:
            off = masked_ref[h, j:j + 1, r0:r1] > 0.5
            m_t = jnp.where(off, NEG_INF, m_t)
        m_new = m_t if j == 0 else jnp.maximum(m_run, m_t)
        shift = jnp.where(off, -NEG_INF, m_new) if j < c else m_new
        pv = _dot(vt_ref[h, :, j * blk:(j + 1) * blk], jnp.exp2(s_cur - shift).astype(BF16))
        acc = pv if j == 0 else acc * jnp.exp2(m_run - m_new) + pv
        m_run = m_new
        if j == c:
            res[hh] = acc[0:hd, :] * (1.0 / acc[hd:hd + 1, :])
            if hh == 1:
                o_ref[r0:r1, pp * LANES:(pp + 1) * LANES] = jnp.concatenate(res, axis=0).T.astype(BF16)


def _moba(proj, kmean, batch, seq, q_col, k_col, v_col, weights):
    t = proj.shape[0]
    width = MOBA_PAIRS_PER_STEP * LANES
    pairs = MOBA_HEADS * MOBA_HEAD_DIM // width
    n_heads = 2 * MOBA_PAIRS_PER_STEP
    steps = batch * pairs

    def spec(col0):
        assert col0 % width == 0
        return pl.BlockSpec((seq, width), lambda b, p: (b, col0 // width + p))

    def weight_spec(w):
        share = 1
        while (w.shape[0] * share) % (steps * BF16_SUBLANES):
            share *= 2
        rows = w.shape[0] * share // steps
        return pl.BlockSpec((rows, w.shape[1]), lambda b, p, share=share: ((b * pairs + p) // share, 0))

    w_specs = [weight_spec(w) for w in weights]
    outs = pl.pallas_call(
        functools.partial(_moba_kernel, n_weights=len(weights)),
        grid=(batch, pairs),
        in_specs=[spec(q_col), spec(k_col), spec(v_col),
                  pl.BlockSpec((seq // MOBA_BLOCK, 1, width), lambda b, p: (b, 0, p))] + w_specs,
        out_specs=[pl.BlockSpec((seq, width), lambda b, p: (b, p))] + w_specs,
        out_shape=[jax.ShapeDtypeStruct((t, pairs * width), BF16)]
        + [jax.ShapeDtypeStruct(w.shape, BF16) for w in weights],
        scratch_shapes=[pltpu.VMEM((n_heads, seq, LANES), BF16),
                        pltpu.VMEM((n_heads, MOBA_HEAD_DIM + MOBA_ONES_ROWS, seq), BF16),
                        pltpu.VMEM((n_heads, seq // MOBA_BLOCK, seq), F32)],
        compiler_params=_params(2),
        name="moba",
    )(proj, proj, proj, kmean, *weights)
    return outs[0], outs[1:]


_RET_LOG_G = [math.log(1.0 - 2.0 ** (-5.0 - h)) for h in range(RET_HEADS)]


def _retention_kernel(q_ref, k_ref, v_ref, g_ref, mem_ref, g_mem_ref, w_ckv_ref, o_ref, kv_ref):
    kv_ref[...] = _wdot(_rms(mem_ref[...], g_mem_ref[...]).astype(BF16), w_ckv_ref[...]).astype(BF16)
    seq = q_ref.shape[0]
    ch = RET_CHUNK
    dv = RET_V_DIM
    lane = lax.broadcasted_iota(jnp.int32, (ch, LANES), 1)
    rowf = lax.broadcasted_iota(jnp.int32, (ch, LANES), 0).astype(F32)
    ri = lax.broadcasted_iota(jnp.int32, (ch, ch), 0)
    ci = lax.broadcasted_iota(jnp.int32, (ch, ch), 1)
    diff = (ri - ci).astype(F32)
    heads = []
    for h in range(RET_HEADS):
        log_g = _RET_LOG_G[h]
        hh = h % 2
        heads.append(dict(
            inner_decay=jnp.where(diff >= 0, jnp.exp(log_g * jnp.maximum(diff, 0.0)), 0.0),
            q_decay=jnp.exp(log_g * (rowf + 1.0)),
            chunk_decay=math.exp(log_g * ch),
            mine=(lane >= hh * RET_QK_DIM) & (lane < (hh + 1) * RET_QK_DIM),
            qk=slice((h // 2) * LANES, (h // 2 + 1) * LANES),
            v=slice(h * dv, (h + 1) * dv),
            state=jnp.zeros((LANES, dv), F32),
        ))
    k_decay = [jnp.exp(jnp.where(lane < RET_QK_DIM, _RET_LOG_G[2 * p], _RET_LOG_G[2 * p + 1])
                       * (ch - 1.0 - rowf)) for p in range(RET_HEADS // 2)]
    for n in range(seq // ch):
        r0, r1 = n * ch, (n + 1) * ch
        k_scaled = [(k_ref[r0:r1, p * LANES:(p + 1) * LANES].astype(F32) * k_decay[p]).astype(BF16)
                    for p in range(RET_HEADS // 2)]
        for h, hd in enumerate(heads):
            qc = jnp.where(hd["mine"], q_ref[r0:r1, hd["qk"]], jnp.zeros((), BF16))
            kc = k_ref[r0:r1, hd["qk"]]
            vc = v_ref[r0:r1, hd["v"]]
            attn = _dot_nt(qc, kc) * hd["inner_decay"]
            out = _dot(attn.astype(BF16), vc)
            out = out + _dot(qc, hd["state"].astype(BF16)) * hd["q_decay"]
            hd["state"] = hd["state"] * hd["chunk_decay"] + _dot_tn(k_scaled[h // 2], vc)
            out = out * lax.rsqrt(jnp.mean(out * out, axis=-1, keepdims=True) + NORM_EPS)
            gate = g_ref[r0:r1, hd["v"]].astype(F32)
            o_ref[r0:r1, hd["v"]] = (_silu(gate) * out).astype(BF16)


def _retention(proj, batch, seq, q_col, k_col, v_col, g_col, mem2d, g_mem, w_ckv):
    t = proj.shape[0]
    wqk = RET_HEADS * RET_QK_DIM
    wv = RET_HEADS * RET_V_DIM
    n_mem = mem2d.shape[0] // batch
    d, n_kv = w_ckv.shape

    def spec(col0, width):
        assert col0 % width == 0
        return pl.BlockSpec((seq, width), lambda b: (b, col0 // width))

    return pl.pallas_call(
        _retention_kernel,
        grid=(batch,),
        in_specs=[spec(q_col, wqk), spec(k_col, wqk), spec(v_col, wv), spec(g_col, wv),
                  pl.BlockSpec((n_mem, d), lambda b: (b, 0)),
                  pl.BlockSpec((1, d), lambda b: (0, 0)),
                  pl.BlockSpec((d, n_kv), lambda b: (0, 0), pipeline_mode=pl.Buffered(1))],
        out_specs=[pl.BlockSpec((seq, wv), lambda b: (b, 0)),
                   pl.BlockSpec((n_mem, n_kv), lambda b: (b, 0))],
        out_shape=[jax.ShapeDtypeStruct((t, wv), BF16),
                   jax.ShapeDtypeStruct((batch * n_mem, n_kv), BF16)],
        compiler_params=_params(1),
        name="retention",
    )(proj, proj, proj, proj, mem2d, g_mem, w_ckv)


def _mix_cross_kernel(mo_ref, ro_ref, x_ref, kv_ref, wo_ref, wq_ref, wc_ref,
                      g_mix_ref, g_pre_ref, g_post_ref, o_ref):
    d = x_ref.shape[1]
    half = mo_ref.shape[1]
    dc = d // CROSS_HEADS
    rows = x_ref.shape[0] // MIX_PARTS
    sl = [slice(r * rows, (r + 1) * rows) for r in range(MIX_PARTS)]
    mix = [_wdot(mo_ref[s, :], wo_ref[0:half, :]) + _wdot(ro_ref[s, :], wo_ref[half:, :]) for s in sl]
    x1 = [x_ref[s, :] + _rms(m, g_mix_ref[...]) for s, m in zip(sl, mix)]
    h = [_rms(x, g_pre_ref[...], scale=dc ** -0.5).astype(BF16) for x in x1]
    cq = [_wdot(hh, wq_ref[...]).astype(BF16) for hh in h]
    att = []
    for r in range(MIX_PARTS):
        heads = []
        for hc in range(CROSS_HEADS):
            c0, c1 = hc * dc, (hc + 1) * dc
            s = _dot_nt(cq[r][:, c0:c1], kv_ref[:, c0:c1])
            e = jnp.exp(s - jnp.max(s, axis=1, keepdims=True))
            o = _dot(e.astype(BF16), kv_ref[:, d + c0:d + c1])
            heads.append((o * (1.0 / jnp.sum(e, axis=1, keepdims=True))).astype(BF16))
        att.append(jnp.concatenate(heads, axis=1))
    c = [_wdot(a, wc_ref[...]) for a in att]
    for r in range(MIX_PARTS):
        o_ref[sl[r], :] = x1[r] + _rms(c[r], g_post_ref[...])


def _mix_cross(mo, ro, x2d, kv, w_out, w_cq, w_co, g_mix, g_pre, g_post, seq, n_mem):
    t, d = x2d.shape
    tm = MIX_ROWS
    tiles_per_seq = seq // tm
    half = mo.shape[1]
    full = lambda i: (0, 0)
    weight = pl.BlockSpec((d, d), full, pipeline_mode=pl.Buffered(1))
    return pl.pallas_call(
        _mix_cross_kernel,
        grid=(t // tm,),
        in_specs=[
            pl.BlockSpec((tm, half), lambda i: (i, 0)),
            pl.BlockSpec((tm, half), lambda i: (i, 0)),
            pl.BlockSpec((tm, d), lambda i: (i, 0)),
            pl.BlockSpec((n_mem, 2 * d), lambda i: (i // tiles_per_seq, 0)),
            weight, weight, weight,
            pl.BlockSpec((1, d), full), pl.BlockSpec((1, d), full), pl.BlockSpec((1, d), full),
        ],
        out_specs=pl.BlockSpec((tm, d), lambda i: (i, 0)),
        out_shape=jax.ShapeDtypeStruct((t, d), F32),
        compiler_params=_params(1),
        name="mix_cross",
    )(mo, ro, x2d, kv, w_out, w_cq, w_co, g_mix, g_pre, g_post)


def _ffn_kernel(x_ref, wgu_ref, wd_ref, g_pre_ref, g_post_ref, o_ref, *, d_ff, bounds):
    rows = x_ref.shape[0] // FFN_PARTS
    xs = [x_ref[r * rows:(r + 1) * rows, :] for r in range(FFN_PARTS)]
    hs = [_rms(x, g_pre_ref[...]).astype(BF16) for x in xs]
    fs = [None] * FFN_PARTS
    for c0, c1 in bounds:
        for r in range(FFN_PARTS):
            gate = _wdot(hs[r], wgu_ref[:, c0:c1])
            up = _wdot(hs[r], wgu_ref[:, d_ff + c0:d_ff + c1])
            act = (_silu(gate) * up).astype(BF16)
            down = _wdot(act, wd_ref[c0:c1, :])
            fs[r] = down if fs[r] is None else fs[r] + down
    for r in range(FFN_PARTS):
        o_ref[r * rows:(r + 1) * rows, :] = xs[r] + _rms(fs[r], g_post_ref[...])


def _ffn(x2d, w_gate_up, w_down, g_pre, g_post):
    t, d = x2d.shape
    d_ff = w_down.shape[0]
    tm = FFN_ROWS
    assert d_ff % MXU_DIM == 0 and FFN_CHUNK % MXU_DIM == 0
    edges = list(range(0, d_ff, FFN_CHUNK)) + [d_ff]
    bounds = tuple(zip(edges[:-1], edges[1:]))
    full = lambda i: (0, 0)
    return pl.pallas_call(
        functools.partial(_ffn_kernel, d_ff=d_ff, bounds=bounds),
        grid=(t // tm,),
        in_specs=[
            pl.BlockSpec((tm, d), lambda i: (i, 0)),
            pl.BlockSpec((d, 2 * d_ff), full, pipeline_mode=pl.Buffered(1)),
            pl.BlockSpec((d_ff, d), full, pipeline_mode=pl.Buffered(1)),
            pl.BlockSpec((1, d), full), pl.BlockSpec((1, d), full),
        ],
        out_specs=pl.BlockSpec((tm, d), lambda i: (i, 0)),
        out_shape=jax.ShapeDtypeStruct((t, d), F32),
        compiler_params=_params(1),
        name="ffn",
    )(x2d, w_gate_up, w_down, g_pre, g_post)


def kernel(x, mem, g_pre_mix, w_in, w_out, g_post_mix, g_pre_cross, g_mem, w_cq, w_ckv, w_co,
           g_post_cross, g_pre_ffn, w_gate_up, w_down, g_post_ffn):
    batch, seq, d = x.shape
    n_mem = mem.shape[1]
    depth = w_in.shape[0]
    moba_w = MOBA_HEADS * MOBA_HEAD_DIM
    ret_qk_w = RET_HEADS * RET_QK_DIM
    ret_v_w = RET_HEADS * RET_V_DIM
    c_mq, c_mk, c_mv = 0, moba_w, 2 * moba_w
    c_rq = 3 * moba_w
    c_rk = c_rq + ret_qk_w
    c_rv = c_rk + ret_qk_w
    c_rg = c_rv + ret_v_w
    n_proj = c_rg + ret_v_w
    assert w_in.shape[2] == n_proj
    sections = (("mq", c_mq, c_mk), ("mk", c_mk, c_mv), ("mv", c_mv, c_rq), ("rq", c_rq, c_rk),
                ("rk", c_rk, c_rv), ("rv", c_rv, c_rg), ("rg", c_rg, n_proj))

    moba_inv = np.power(ROPE_THETA, -np.arange(ROPE_DIM // 2, dtype=np.float64) * 2.0 / ROPE_DIM)
    ret_inv = 1.0 / np.power(RET_THETA, np.linspace(0.0, 1.0, RET_QK_DIM // 2))
    tabs = (
        _rotary_tables(seq, moba_inv, ROPE_DIM, MOBA_HEAD_DIM, MOBA_HEAD_DIM ** -0.5 * math.log2(math.e)),
        _rotary_tables(seq, moba_inv, ROPE_DIM, MOBA_HEAD_DIM, 1.0),
        _rotary_tables(seq, ret_inv, RET_QK_DIM, RET_QK_DIM, 1.0),
        _rotary_tables(seq, ret_inv, RET_QK_DIM, RET_QK_DIM, RET_QK_DIM ** -0.5),
    )

    xf = x.reshape(batch * seq, d)
    mem2d = mem.reshape(batch * n_mem, d)
    row = lambda g: g.reshape(1, d)
    for l in range(depth):
        proj, kmean = _in_proj(xf, row(g_pre_mix[l]), w_in[l], tabs, seq, sections)
        mo, (b_ckv, b_out, b_cq, b_co, b_gate_up, b_down) = _moba(
            proj, kmean, batch, seq, c_mq, c_mk, c_mv,
            (w_ckv[l], w_out[l], w_cq[l], w_co[l], w_gate_up[l], w_down[l]))
        ro, kv = _retention(proj, batch, seq, c_rq, c_rk, c_rv, c_rg, mem2d, row(g_mem[l]), b_ckv)
        x2 = _mix_cross(mo, ro, xf, kv, b_out, b_cq, b_co, row(g_post_mix[l]),
                        row(g_pre_cross[l]), row(g_post_cross[l]), seq, n_mem)
        xf = _ffn(x2, b_gate_up, b_down, row(g_pre_ffn[l]), row(g_post_ffn[l]))
    return xf.reshape(batch, seq, d)
```

```python
import functools
import math

import jax
import jax.numpy as jnp
import numpy as np
from jax import lax
from jax.experimental import pallas as pl
from jax.experimental.pallas import tpu as pltpu

F32 = jnp.float32
BF16 = jnp.bfloat16

NORM_EPS = 1e-6
NEG_INF = -1e30

LANES = 128
BF16_SUBLANES = 16
MXU_DIM = 256
VMEM_LIMIT_BYTES = 56 * 1024 * 1024

MOBA_HEAD_DIM = 64
MOBA_HEADS = 8
MOBA_BLOCK = 256
MOBA_TOPK = 3
MOBA_ONES_ROWS = 16
MOBA_PREFETCH = 5
MOBA_PAIRS_PER_STEP = 2
ROPE_THETA = 500000.0
ROPE_DIM = MOBA_HEAD_DIM // 4

RET_HEADS = 4
RET_QK_DIM = 64
RET_V_DIM = 128
RET_THETA = 10000.0
RET_CHUNK = 256

CROSS_HEADS = 4

IN_PROJ_ROWS = 1024
IN_PROJ_PARTS = 2
MIX_ROWS = 1024
MIX_PARTS = 2
FFN_ROWS = 1024
FFN_PARTS = 4
FFN_CHUNK = 1536


def _dot(a, b):
    return jnp.dot(a, b, preferred_element_type=F32)


def _wdot(a, w):
    return jnp.dot(a, w.astype(BF16), preferred_element_type=F32)


def _dot_nt(a, b):
    return lax.dot_general(a, b, (((1,), (1,)), ((), ())), preferred_element_type=F32)


def _dot_tn(a, b):
    return lax.dot_general(a, b, (((0,), (0,)), ((), ())), preferred_element_type=F32)


def _rms(x, g, scale=None):
    inv = lax.rsqrt(jnp.mean(x * x, axis=-1, keepdims=True) + NORM_EPS)
    if scale is not None:
        inv = inv * scale
    return x * inv * g


def _silu(x):
    h = 0.5 * x
    return h + h * jnp.tanh(h)


def _params(n_grid_dims):
    return pltpu.CompilerParams(
        dimension_semantics=("arbitrary",) * n_grid_dims,
        vmem_limit_bytes=VMEM_LIMIT_BYTES,
    )


def _rotary_tables(seq, inv_freq, rot_dim, head_dim, scale):
    half = rot_dim // 2
    ang = np.arange(seq, dtype=np.float64)[:, None] * inv_freq[None, :]
    cos, sin = np.cos(ang), np.sin(ang)
    pad = head_dim - rot_dim
    a = np.concatenate([cos, cos, np.ones((seq, pad))], axis=1)
    bm = np.concatenate([-sin, np.zeros((seq, half + pad))], axis=1)
    bp = np.concatenate([np.zeros((seq, half)), sin, np.zeros((seq, pad))], axis=1)
    reps = LANES // head_dim
    tabs = np.stack([np.tile(t, (1, reps)) for t in (a, bm, bp)], axis=0)
    return jnp.asarray((tabs * scale).astype(np.float32))


def _rotate(acc, tab_ref, rows, half):
    a, bm, bp = tab_ref[0, rows, :], tab_ref[1, rows, :], tab_ref[2, rows, :]
    outs = []
    for c in range(acc.shape[1] // LANES):
        xs = acc[:, c * LANES:(c + 1) * LANES]
        outs.append(xs * a + pltpu.roll(xs, LANES - half, 1) * bm + pltpu.roll(xs, half, 1) * bp)
    return jnp.concatenate(outs, axis=1)


def _in_proj_kernel(x_ref, g_ref, w_ref, tmq_ref, tmk_ref, trq_ref, trk_ref, o_ref, kmean_ref, *, sections):
    rows = x_ref.shape[0] // IN_PROJ_PARTS
    blocks = rows // MOBA_BLOCK
    sl = [slice(r * rows, (r + 1) * rows) for r in range(IN_PROJ_PARTS)]
    h = [_rms(x_ref[s, :], g_ref[...]).astype(BF16) for s in sl]
    tabs = {"mq": (tmq_ref, ROPE_DIM // 2), "mk": (tmk_ref, ROPE_DIM // 2),
            "rq": (trq_ref, RET_QK_DIM // 2), "rk": (trk_ref, RET_QK_DIM // 2)}
    for kind, c0, c1 in sections:
        for r in range(IN_PROJ_PARTS):
            acc = _wdot(h[r], w_ref[:, c0:c1])
            if kind in tabs:
                tab_ref, half = tabs[kind]
                acc = _rotate(acc, tab_ref, sl[r], half)
            if kind == "mk":
                kmean_ref[r * blocks:(r + 1) * blocks, 0, :] = jnp.mean(
                    acc.reshape(blocks, MOBA_BLOCK, c1 - c0), axis=1)
            o_ref[sl[r], c0:c1] = acc.astype(BF16)


def _in_proj(x2d, g, w, tabs, seq, sections):
    t, d = x2d.shape
    n = w.shape[1]
    tm = IN_PROJ_ROWS
    tiles_per_seq = seq // tm
    tab_spec = pl.BlockSpec((3, tm, LANES), lambda i: (0, i % tiles_per_seq, 0))
    mk_width = next(c1 - c0 for kind, c0, c1 in sections if kind == "mk")
    return pl.pallas_call(
        functools.partial(_in_proj_kernel, sections=sections),
        grid=(t // tm,),
        in_specs=[
            pl.BlockSpec((tm, d), lambda i: (i, 0)),
            pl.BlockSpec((1, d), lambda i: (0, 0)),
            pl.BlockSpec((d, n), lambda i: (0, 0), pipeline_mode=pl.Buffered(1)),
            tab_spec, tab_spec, tab_spec, tab_spec,
        ],
        out_specs=[pl.BlockSpec((tm, n), lambda i: (i, 0)),
                   pl.BlockSpec((tm // MOBA_BLOCK, 1, mk_width), lambda i: (i, 0, 0))],
        out_shape=[jax.ShapeDtypeStruct((t, n), BF16),
                   jax.ShapeDtypeStruct((t // MOBA_BLOCK, 1, mk_width), F32)],
        compiler_params=_params(1),
        name="in_proj",
    )(x2d, g, w, *tabs)


def _moba_kernel(q_ref, k_ref, v_ref, km_ref, *refs, n_weights):
    w_refs, o_ref, wb_refs = refs[:n_weights], refs[n_weights], refs[n_weights + 1:2 * n_weights + 1]
    qp_ref, vt_ref, masked_ref = refs[2 * n_weights + 1:]
    for w_ref, wb_ref in zip(w_refs, wb_refs):
        wb_ref[...] = w_ref[...].astype(BF16)
    seq = q_ref.shape[0]
    blk = MOBA_BLOCK
    nb = seq // blk
    hd = MOBA_HEAD_DIM
    lane = lax.broadcasted_iota(jnp.int32, (seq, LANES), 1)
    lane8 = lax.broadcasted_iota(jnp.int32, (nb, LANES), 1)
    jidx = lax.broadcasted_iota(jnp.int32, (nb, seq), 0)
    qblk = lax.broadcasted_iota(jnp.int32, (nb, seq), 1) // blk
    ones = jnp.ones((MOBA_ONES_ROWS, seq), F32)

    def prepare(pp):
        ln = slice(pp * LANES, (pp + 1) * LANES)
        q2 = q_ref[:, ln]
        kmean2 = km_ref[:, 0, ln]
        km = jnp.concatenate([jnp.where(lane8 < hd, kmean2, 0.0), jnp.where(lane8 >= hd, kmean2, 0.0)], axis=0)
        km_hi = km.astype(BF16)
        km_lo = (km - km_hi.astype(F32)).astype(BF16)
        gate2 = _dot_nt(jnp.concatenate([km_hi, km_lo], axis=0), q2)
        v2t = v_ref[:, ln].astype(F32).T
        for hh in range(2):
            h = 2 * pp + hh
            gate = gate2[hh * nb:(hh + 1) * nb] + gate2[(2 + hh) * nb:(3 + hh) * nb]
            rank = jnp.zeros((nb, seq), jnp.int32)
            for i in range(nb):
                gi = gate[i:i + 1, :]
                beats = ((gi > gate) | ((gi == gate) & (i < jidx))) & (i < qblk)
                rank = rank + beats.astype(jnp.int32)
            masked_ref[h] = ((jidx >= qblk) | (rank >= MOBA_TOPK)).astype(F32)
            mine = (lane >= hh * hd) & (lane < (hh + 1) * hd)
            qp_ref[h] = jnp.where(mine, q2, jnp.zeros_like(q2))
            vt_ref[h] = jnp.concatenate([v2t[hh * hd:(hh + 1) * hd], ones], axis=0).astype(BF16)

    kr = lax.broadcasted_iota(jnp.int32, (blk, blk), 0)
    qcol = lax.broadcasted_iota(jnp.int32, (blk, blk), 1)

    def scores(t):
        pp, c, hh, j = tiles[t]
        s = _dot_nt(k_ref[j * blk:(j + 1) * blk, pp * LANES:(pp + 1) * LANES],
                    qp_ref[2 * pp + hh, c * blk:(c + 1) * blk, :])
        return jnp.where(kr <= qcol, s, NEG_INF) if j == c else s

    tiles = [(pp, c, hh, j) for pp in range(MOBA_PAIRS_PER_STEP) for c in range(nb)
             for hh in range(2) for j in range(c + 1)]
    for pp in range(MOBA_PAIRS_PER_STEP):
        prepare(pp)
    pending = {t: scores(t) for t in range(min(MOBA_PREFETCH, len(tiles)))}
    res = [None, None]
    for t, (pp, c, hh, j) in enumerate(tiles):
        r0, r1 = c * blk, (c + 1) * blk
        if t + MOBA_PREFETCH < len(tiles):
            pending[t + MOBA_PREFETCH] = scores(t + MOBA_PREFETCH)
        s_cur = pending.pop(t)
        h = 2 * pp + hh
        m_t = jnp.max(s_cur, axis=0, keepdims=True)
        if j < c:
            off = masked_ref[h, j:j + 1, r0:r1] > 0.5
            m_t = jnp.where(off, NEG_INF, m_t)
        m_new = m_t if j == 0 else jnp.maximum(m_run, m_t)
        shift = jnp.where(off, -NEG_INF, m_new) if j < c else m_new
        pv = _dot(vt_ref[h, :, j * blk:(j + 1) * blk], jnp.exp2(s_cur - shift).astype(BF16))
        acc = pv if j == 0 else acc * jnp.exp2(m_run - m_new) + pv
        m_run = m_new
        if j == c:
            res[hh] = acc[0:hd, :] * (1.0 / acc[hd:hd + 1, :])
            if hh == 1:
                o_ref[r0:r1, pp * LANES:(pp + 1) * LANES] = jnp.concatenate(res, axis=0).T.astype(BF16)


def _moba(proj, kmean, batch, seq, q_col, k_col, v_col, weights):
    t = proj.shape[0]
    width = MOBA_PAIRS_PER_STEP * LANES
    pairs = MOBA_HEADS * MOBA_HEAD_DIM // width
    n_heads = 2 * MOBA_PAIRS_PER_STEP
    steps = batch * pairs

    def spec(col0):
        assert col0 % width == 0
        return pl.BlockSpec((seq, width), lambda b, p: (b, col0 // width + p))

    def weight_spec(w):
        share = 1
        while (w.shape[0] * share) % (steps * BF16_SUBLANES):
            share *= 2
        rows = w.shape[0] * share // steps
        return pl.BlockSpec((rows, w.shape[1]), lambda b, p, share=share: ((b * pairs + p) // share, 0))

    w_specs = [weight_spec(w) for w in weights]
    outs = pl.pallas_call(
        functools.partial(_moba_kernel, n_weights=len(weights)),
        grid=(batch, pairs),
        in_specs=[spec(q_col), spec(k_col), spec(v_col),
                  pl.BlockSpec((seq // MOBA_BLOCK, 1, width), lambda b, p: (b, 0, p))] + w_specs,
        out_specs=[pl.BlockSpec((seq, width), lambda b, p: (b, p))] + w_specs,
        out_shape=[jax.ShapeDtypeStruct((t, pairs * width), BF16)]
        + [jax.ShapeDtypeStruct(w.shape, BF16) for w in weights],
        scratch_shapes=[pltpu.VMEM((n_heads, seq, LANES), BF16),
                        pltpu.VMEM((n_heads, MOBA_HEAD_DIM + MOBA_ONES_ROWS, seq), BF16),
                        pltpu.VMEM((n_heads, seq // MOBA_BLOCK, seq), F32)],
        compiler_params=_params(2),
        name="moba",
    )(proj, proj, proj, kmean, *weights)
    return outs[0], outs[1:]


_RET_LOG_G = [math.log(1.0 - 2.0 ** (-5.0 - h)) for h in range(RET_HEADS)]


def _retention_kernel(q_ref, k_ref, v_ref, g_ref, mem_ref, g_mem_ref, w_ckv_ref, o_ref, kv_ref):
    kv_ref[...] = _wdot(_rms(mem_ref[...], g_mem_ref[...]).astype(BF16), w_ckv_ref[...]).astype(BF16)
    seq = q_ref.shape[0]
    ch = RET_CHUNK
    dv = RET_V_DIM
    lane = lax.broadcasted_iota(jnp.int32, (ch, LANES), 1)
    rowf = lax.broadcasted_iota(jnp.int32, (ch, LANES), 0).astype(F32)
    ri = lax.broadcasted_iota(jnp.int32, (ch, ch), 0)
    ci = lax.broadcasted_iota(jnp.int32, (ch, ch), 1)
    diff = (ri - ci).astype(F32)
    heads = []
    for h in range(RET_HEADS):
        log_g = _RET_LOG_G[h]
        hh = h % 2
        heads.append(dict(
            inner_decay=jnp.where(diff >= 0, jnp.exp(log_g * jnp.maximum(diff, 0.0)), 0.0),
            q_decay=jnp.exp(log_g * (rowf + 1.0)),
            chunk_decay=math.exp(log_g * ch),
            mine=(lane >= hh * RET_QK_DIM) & (lane < (hh + 1) * RET_QK_DIM),
            qk=slice((h // 2) * LANES, (h // 2 + 1) * LANES),
            v=slice(h * dv, (h + 1) * dv),
            state=jnp.zeros((LANES, dv), F32),
        ))
    k_decay = [jnp.exp(jnp.where(lane < RET_QK_DIM, _RET_LOG_G[2 * p], _RET_LOG_G[2 * p + 1])
                       * (ch - 1.0 - rowf)) for p in range(RET_HEADS // 2)]
    for n in range(seq // ch):
        r0, r1 = n * ch, (n + 1) * ch
        k_scaled = [(k_ref[r0:r1, p * LANES:(p + 1) * LANES].astype(F32) * k_decay[p]).astype(BF16)
                    for p in range(RET_HEADS // 2)]
        for h, hd in enumerate(heads):
            qc = jnp.where(hd["mine"], q_ref[r0:r1, hd["qk"]], jnp.zeros((), BF16))
            kc = k_ref[r0:r1, hd["qk"]]
            vc = v_ref[r0:r1, hd["v"]]
            attn = _dot_nt(qc, kc) * hd["inner_decay"]
            out = _dot(attn.astype(BF16), vc)
            out = out + _dot(qc, hd["state"].astype(BF16)) * hd["q_decay"]
            hd["state"] = hd["state"] * hd["chunk_decay"] + _dot_tn(k_scaled[h // 2], vc)
            out = out * lax.rsqrt(jnp.mean(out * out, axis=-1, keepdims=True) + NORM_EPS)
            gate = g_ref[r0:r1, hd["v"]].astype(F32)
            o_ref[r0:r1, hd["v"]] = (_silu(gate) * out).astype(BF16)


def _retention(proj, batch, seq, q_col, k_col, v_col, g_col, mem2d, g_mem, w_ckv):
    t = proj.shape[0]
    wqk = RET_HEADS * RET_QK_DIM
    wv = RET_HEADS * RET_V_DIM
    n_mem = mem2d.shape[0] // batch
    d, n_kv = w_ckv.shape

    def spec(col0, width):
        assert col0 % width == 0
        return pl.BlockSpec((seq, width), lambda b: (b, col0 // width))

    return pl.pallas_call(
        _retention_kernel,
        grid=(batch,),
        in_specs=[spec(q_col, wqk), spec(k_col, wqk), spec(v_col, wv), spec(g_col, wv),
                  pl.BlockSpec((n_mem, d), lambda b: (b, 0)),
                  pl.BlockSpec((1, d), lambda b: (0, 0)),
                  pl.BlockSpec((d, n_kv), lambda b: (0, 0), pipeline_mode=pl.Buffered(1))],
        out_specs=[pl.BlockSpec((seq, wv), lambda b: (b, 0)),
                   pl.BlockSpec((n_mem, n_kv), lambda b: (b, 0))],
        out_shape=[jax.ShapeDtypeStruct((t, wv), BF16),
                   jax.ShapeDtypeStruct((batch * n_mem, n_kv), BF16)],
        compiler_params=_params(1),
        name="retention",
    )(proj, proj, proj, proj, mem2d, g_mem, w_ckv)


def _mix_cross_kernel(mo_ref, ro_ref, x_ref, kv_ref, wo_ref, wq_ref, wc_ref,
                      g_mix_ref, g_pre_ref, g_post_ref, o_ref):
    d = x_ref.shape[1]
    half = mo_ref.shape[1]
    dc = d // CROSS_HEADS
    rows = x_ref.shape[0] // MIX_PARTS
    sl = [slice(r * rows, (r + 1) * rows) for r in range(MIX_PARTS)]
    mix = [_wdot(mo_ref[s, :], wo_ref[0:half, :]) + _wdot(ro_ref[s, :], wo_ref[half:, :]) for s in sl]
    x1 = [x_ref[s, :] + _rms(m, g_mix_ref[...]) for s, m in zip(sl, mix)]
    h = [_rms(x, g_pre_ref[...], scale=dc ** -0.5).astype(BF16) for x in x1]
    cq = [_wdot(hh, wq_ref[...]).astype(BF16) for hh in h]
    att = []
    for r in range(MIX_PARTS):
        heads = []
        for hc in range(CROSS_HEADS):
            c0, c1 = hc * dc, (hc + 1) * dc
            s = _dot_nt(cq[r][:, c0:c1], kv_ref[:, c0:c1])
            e = jnp.exp(s - jnp.max(s, axis=1, keepdims=True))
            o = _dot(e.astype(BF16), kv_ref[:, d + c0:d + c1])
            heads.append((o * (1.0 / jnp.sum(e, axis=1, keepdims=True))).astype(BF16))
        att.append(jnp.concatenate(heads, axis=1))
    c = [_wdot(a, wc_ref[...]) for a in att]
    for r in range(MIX_PARTS):
        o_ref[sl[r], :] = x1[r] + _rms(c[r], g_post_ref[...])


def _mix_cross(mo, ro, x2d, kv, w_out, w_cq, w_co, g_mix, g_pre, g_post, seq, n_mem):
    t, d = x2d.shape
    tm = MIX_ROWS
    tiles_per_seq = seq // tm
    half = mo.shape[1]
    full = lambda i: (0, 0)
    weight = pl.BlockSpec((d, d), full, pipeline_mode=pl.Buffered(1))
    return pl.pallas_call(
        _mix_cross_kernel,
        grid=(t // tm,),
        in_specs=[
            pl.BlockSpec((tm, half), lambda i: (i, 0)),
            pl.BlockSpec((tm, half), lambda i: (i, 0)),
            pl.BlockSpec((tm, d), lambda i: (i, 0)),
            pl.BlockSpec((n_mem, 2 * d), lambda i: (i // tiles_per_seq, 0)),
            weight, weight, weight,
            pl.BlockSpec((1, d), full), pl.BlockSpec((1, d), full), pl.BlockSpec((1, d), full),
        ],
        out_specs=pl.BlockSpec((tm, d), lambda i: (i, 0)),
        out_shape=jax.ShapeDtypeStruct((t, d), F32),
        compiler_params=_params(1),
        name="mix_cross",
    )(mo, ro, x2d, kv, w_out, w_cq, w_co, g_mix, g_pre, g_post)


def _ffn_kernel(x_ref, wgu_ref, wd_hbm, g_pre_ref, g_post_ref, o_ref, wd_buf, wd_sem, *, d_ff, bounds):
    i = pl.program_id(0)
    slot = i % 2

    def wd_copy(s):
        return pltpu.make_async_copy(wd_hbm, wd_buf.at[s], wd_sem.at[s])

    @pl.when(i == 0)
    def _():
        wd_copy(0).start()

    @pl.when(i + 1 < pl.num_programs(0))
    def _():
        wd_copy(1 - slot).start()

    rows = x_ref.shape[0] // FFN_PARTS
    xs = [x_ref[r * rows:(r + 1) * rows, :] for r in range(FFN_PARTS)]
    hs = [_rms(x, g_pre_ref[...]).astype(BF16) for x in xs]
    fs = [None] * FFN_PARTS
    for c0, c1 in bounds:
        acts = []
        for r in range(FFN_PARTS):
            gate = _wdot(hs[r], wgu_ref[:, c0:c1])
            up = _wdot(hs[r], wgu_ref[:, d_ff + c0:d_ff + c1])
            acts.append((_silu(gate) * up).astype(BF16))
        if c0 == bounds[0][0]:
            wd_copy(slot).wait()
        for r in range(FFN_PARTS):
            down = _wdot(acts[r], wd_buf[slot, c0:c1, :])
            fs[r] = down if fs[r] is None else fs[r] + down
    for r in range(FFN_PARTS):
        o_ref[r * rows:(r + 1) * rows, :] = xs[r] + _rms(fs[r], g_post_ref[...])


def _ffn(x2d, w_gate_up, w_down, g_pre, g_post):
    t, d = x2d.shape
    d_ff = w_down.shape[0]
    tm = FFN_ROWS
    assert d_ff % MXU_DIM == 0 and FFN_CHUNK % MXU_DIM == 0
    edges = list(range(0, d_ff, FFN_CHUNK)) + [d_ff]
    bounds = tuple(zip(edges[:-1], edges[1:]))
    full = lambda i: (0, 0)
    return pl.pallas_call(
        functools.partial(_ffn_kernel, d_ff=d_ff, bounds=bounds),
        grid=(t // tm,),
        in_specs=[
            pl.BlockSpec((tm, d), lambda i: (i, 0)),
            pl.BlockSpec((d, 2 * d_ff), full, pipeline_mode=pl.Buffered(1)),
            pl.BlockSpec(memory_space=pl.ANY),
            pl.BlockSpec((1, d), full), pl.BlockSpec((1, d), full),
        ],
        out_specs=pl.BlockSpec((tm, d), lambda i: (i, 0)),
        out_shape=jax.ShapeDtypeStruct((t, d), F32),
        scratch_shapes=[pltpu.VMEM((2, d_ff, d), w_down.dtype), pltpu.SemaphoreType.DMA((2,))],
        compiler_params=_params(1),
        name="ffn",
    )(x2d, w_gate_up, w_down, g_pre, g_post)


def kernel(x, mem, g_pre_mix, w_in, w_out, g_post_mix, g_pre_cross, g_mem, w_cq, w_ckv, w_co,
           g_post_cross, g_pre_ffn, w_gate_up, w_down, g_post_ffn):
    batch, seq, d = x.shape
    n_mem = mem.shape[1]
    depth = w_in.shape[0]
    moba_w = MOBA_HEADS * MOBA_HEAD_DIM
    ret_qk_w = RET_HEADS * RET_QK_DIM
    ret_v_w = RET_HEADS * RET_V_DIM
    c_mq, c_mk, c_mv = 0, moba_w, 2 * moba_w
    c_rq = 3 * moba_w
    c_rk = c_rq + ret_qk_w
    c_rv = c_rk + ret_qk_w
    c_rg = c_rv + ret_v_w
    n_proj = c_rg + ret_v_w
    assert w_in.shape[2] == n_proj
    sections = (("mq", c_mq, c_mk), ("mk", c_mk, c_mv), ("mv", c_mv, c_rq), ("rq", c_rq, c_rk),
                ("rk", c_rk, c_rv), ("rv", c_rv, c_rg), ("rg", c_rg, n_proj))

    moba_inv = np.power(ROPE_THETA, -np.arange(ROPE_DIM // 2, dtype=np.float64) * 2.0 / ROPE_DIM)
    ret_inv = 1.0 / np.power(RET_THETA, np.linspace(0.0, 1.0, RET_QK_DIM // 2))
    tabs = (
        _rotary_tables(seq, moba_inv, ROPE_DIM, MOBA_HEAD_DIM, MOBA_HEAD_DIM ** -0.5 * math.log2(math.e)),
        _rotary_tables(seq, moba_inv, ROPE_DIM, MOBA_HEAD_DIM, 1.0),
        _rotary_tables(seq, ret_inv, RET_QK_DIM, RET_QK_DIM, 1.0),
        _rotary_tables(seq, ret_inv, RET_QK_DIM, RET_QK_DIM, RET_QK_DIM ** -0.5),
    )

    xf = x.reshape(batch * seq, d)
    mem2d = mem.reshape(batch * n_mem, d)
    row = lambda g: g.reshape(1, d)
    for l in range(depth):
        proj, kmean = _in_proj(xf, row(g_pre_mix[l]), w_in[l], tabs, seq, sections)
        mo, (b_ckv, b_out, b_cq, b_co, b_gate_up, b_down) = _moba(
            proj, kmean, batch, seq, c_mq, c_mk, c_mv,
            (w_ckv[l], w_out[l], w_cq[l], w_co[l], w_gate_up[l], w_down[l]))
        ro, kv = _retention(proj, batch, seq, c_rq, c_rk, c_rv, c_rg, mem2d, row(g_mem[l]), b_ckv)
        x2 = _mix_cross(mo, ro, xf, kv, b_out, b_cq, b_co, row(g_post_mix[l]),
                        row(g_pre_cross[l]), row(g_post_cross[l]), seq, n_mem)
        xf = _ffn(x2, b_gate_up, b_down, row(g_pre_ffn[l]), row(g_post_ffn[l]))
    return xf.reshape(batch, seq, d)
```

```python
import functools
import math

import jax
import jax.numpy as jnp
import numpy as np
from jax import lax
from jax.experimental import pallas as pl
from jax.experimental.pallas import tpu as pltpu

F32 = jnp.float32
BF16 = jnp.bfloat16

NORM_EPS = 1e-6
NEG_INF = -1e30

LANES = 128
BF16_SUBLANES = 16
MXU_DIM = 256
VMEM_LIMIT_BYTES = 56 * 1024 * 1024

MOBA_HEAD_DIM = 64
MOBA_HEADS = 8
MOBA_BLOCK = 256
MOBA_TOPK = 3
MOBA_ONES_ROWS = 16
MOBA_PREFETCH = 5
MOBA_PAIRS_PER_STEP = 2
ROPE_THETA = 500000.0
ROPE_DIM = MOBA_HEAD_DIM // 4

RET_HEADS = 4
RET_QK_DIM = 64
RET_V_DIM = 128
RET_THETA = 10000.0
RET_CHUNK = 256

CROSS_HEADS = 4

IN_PROJ_ROWS = 1024
IN_PROJ_PARTS = 2
MIX_ROWS = 1024
MIX_PARTS = 2
FFN_ROWS = 1024
FFN_PARTS = 4
FFN_CHUNK = 1536


def _dot(a, b):
    return jnp.dot(a, b, preferred_element_type=F32)


def _wdot(a, w):
    return jnp.dot(a, w.astype(BF16), preferred_element_type=F32)


def _dot_nt(a, b):
    return lax.dot_general(a, b, (((1,), (1,)), ((), ())), preferred_element_type=F32)


def _dot_tn(a, b):
    return lax.dot_general(a, b, (((0,), (0,)), ((), ())), preferred_element_type=F32)


def _rms(x, g, scale=None):
    inv = lax.rsqrt(jnp.mean(x * x, axis=-1, keepdims=True) + NORM_EPS)
    if scale is not None:
        inv = inv * scale
    return x * inv * g


def _silu(x):
    h = 0.5 * x
    return h + h * jnp.tanh(h)


def _params(n_grid_dims):
    return pltpu.CompilerParams(
        dimension_semantics=("arbitrary",) * n_grid_dims,
        vmem_limit_bytes=VMEM_LIMIT_BYTES,
    )


def _rotary_tables(seq, inv_freq, rot_dim, head_dim, scale):
    half = rot_dim // 2
    ang = np.arange(seq, dtype=np.float64)[:, None] * inv_freq[None, :]
    cos, sin = np.cos(ang), np.sin(ang)
    pad = head_dim - rot_dim
    a = np.concatenate([cos, cos, np.ones((seq, pad))], axis=1)
    bm = np.concatenate([-sin, np.zeros((seq, half + pad))], axis=1)
    bp = np.concatenate([np.zeros((seq, half)), sin, np.zeros((seq, pad))], axis=1)
    reps = LANES // head_dim
    tabs = np.stack([np.tile(t, (1, reps)) for t in (a, bm, bp)], axis=0)
    return jnp.asarray((tabs * scale).astype(np.float32))


def _rotate(acc, tab_ref, rows, half):
    a, bm, bp = tab_ref[0, rows, :], tab_ref[1, rows, :], tab_ref[2, rows, :]
    outs = []
    for c in range(acc.shape[1] // LANES):
        xs = acc[:, c * LANES:(c + 1) * LANES]
        outs.append(xs * a + pltpu.roll(xs, LANES - half, 1) * bm + pltpu.roll(xs, half, 1) * bp)
    return jnp.concatenate(outs, axis=1)


def _in_proj_kernel(x_ref, g_ref, w_ref, tmq_ref, tmk_ref, trq_ref, trk_ref, o_ref, kmean_ref, *, sections):
    rows = x_ref.shape[0] // IN_PROJ_PARTS
    blocks = rows // MOBA_BLOCK
    sl = [slice(r * rows, (r + 1) * rows) for r in range(IN_PROJ_PARTS)]
    h = [_rms(x_ref[s, :], g_ref[...]).astype(BF16) for s in sl]
    tabs = {"mq": (tmq_ref, ROPE_DIM // 2), "mk": (tmk_ref, ROPE_DIM // 2),
            "rq": (trq_ref, RET_QK_DIM // 2), "rk": (trk_ref, RET_QK_DIM // 2)}
    for kind, c0, c1 in sections:
        for r in range(IN_PROJ_PARTS):
            acc = _wdot(h[r], w_ref[:, c0:c1])
            if kind in tabs:
                tab_ref, half = tabs[kind]
                acc = _rotate(acc, tab_ref, sl[r], half)
            if kind == "mk":
                kmean_ref[r * blocks:(r + 1) * blocks, 0, :] = jnp.mean(
                    acc.reshape(blocks, MOBA_BLOCK, c1 - c0), axis=1)
            o_ref[sl[r], c0:c1] = acc.astype(BF16)


def _in_proj(x2d, g, w, tabs, seq, sections):
    t, d = x2d.shape
    n = w.shape[1]
    tm = IN_PROJ_ROWS
    tiles_per_seq = seq // tm
    tab_spec = pl.BlockSpec((3, tm, LANES), lambda i: (0, i % tiles_per_seq, 0))
    mk_width = next(c1 - c0 for kind, c0, c1 in sections if kind == "mk")
    return pl.pallas_call(
        functools.partial(_in_proj_kernel, sections=sections),
        grid=(t // tm,),
        in_specs=[
            pl.BlockSpec((tm, d), lambda i: (i, 0)),
            pl.BlockSpec((1, d), lambda i: (0, 0)),
            pl.BlockSpec((d, n), lambda i: (0, 0), pipeline_mode=pl.Buffered(1)),
            tab_spec, tab_spec, tab_spec, tab_spec,
        ],
        out_specs=[pl.BlockSpec((tm, n), lambda i: (i, 0)),
                   pl.BlockSpec((tm // MOBA_BLOCK, 1, mk_width), lambda i: (i, 0, 0))],
        out_shape=[jax.ShapeDtypeStruct((t, n), BF16),
                   jax.ShapeDtypeStruct((t // MOBA_BLOCK, 1, mk_width), F32)],
        compiler_params=_params(1),
        name="in_proj",
    )(x2d, g, w, *tabs)


def _moba_kernel(q_ref, k_ref, v_ref, km_ref, *refs, n_weights):
    w_refs, o_ref, wb_refs = refs[:n_weights], refs[n_weights], refs[n_weights + 1:2 * n_weights + 1]
    qp_ref, vt_ref, masked_ref = refs[2 * n_weights + 1:]
    for w_ref, wb_ref in zip(w_refs, wb_refs):
        wb_ref[...] = w_ref[...].astype(BF16)
    seq = q_ref.shape[0]
    blk = MOBA_BLOCK
    nb = seq // blk
    hd = MOBA_HEAD_DIM
    lane = lax.broadcasted_iota(jnp.int32, (seq, LANES), 1)
    lane8 = lax.broadcasted_iota(jnp.int32, (nb, LANES), 1)
    jidx = lax.broadcasted_iota(jnp.int32, (nb, seq), 0)
    qblk = lax.broadcasted_iota(jnp.int32, (nb, seq), 1) // blk
    ones = jnp.ones((MOBA_ONES_ROWS, seq), F32)

    def prepare(pp):
        ln = slice(pp * LANES, (pp + 1) * LANES)
        q2 = q_ref[:, ln]
        kmean2 = km_ref[:, 0, ln]
        km = jnp.concatenate([jnp.where(lane8 < hd, kmean2, 0.0), jnp.where(lane8 >= hd, kmean2, 0.0)], axis=0)
        km_hi = km.astype(BF16)
        km_lo = (km - km_hi.astype(F32)).astype(BF16)
        gate2 = _dot_nt(jnp.concatenate([km_hi, km_lo], axis=0), q2)
        v2t = v_ref[:, ln].astype(F32).T
        for hh in range(2):
            h = 2 * pp + hh
            gate = gate2[hh * nb:(hh + 1) * nb] + gate2[(2 + hh) * nb:(3 + hh) * nb]
            rank = jnp.zeros((nb, seq), jnp.int32)
            for i in range(nb):
                gi = gate[i:i + 1, :]
                beats = ((gi > gate) | ((gi == gate) & (i < jidx))) & (i < qblk)
                rank = rank + beats.astype(jnp.int32)
            masked_ref[h] = ((jidx >= qblk) | (rank >= MOBA_TOPK)).astype(F32)
            mine = (lane >= hh * hd) & (lane < (hh + 1) * hd)
            qp_ref[h] = jnp.where(mine, q2, jnp.zeros_like(q2))
            vt_ref[h] = jnp.concatenate([v2t[hh * hd:(hh + 1) * hd], ones], axis=0).astype(BF16)

    kr = lax.broadcasted_iota(jnp.int32, (blk, blk), 0)
    qcol = lax.broadcasted_iota(jnp.int32, (blk, blk), 1)

    def scores(t):
        pp, c, hh, j = tiles[t]
        s = _dot_nt(k_ref[j * blk:(j + 1) * blk, pp * LANES:(pp + 1) * LANES],
                    qp_ref[2 * pp + hh, c * blk:(c + 1) * blk, :])
        return jnp.where(kr <= qcol, s, NEG_INF) if j == c else s

    tiles = [(pp, c, hh, j) for pp in range(MOBA_PAIRS_PER_STEP) for c in range(nb)
             for hh in range(2) for j in range(c + 1)]
    for pp in range(MOBA_PAIRS_PER_STEP):
        prepare(pp)
    pending = {t: scores(t) for t in range(min(MOBA_PREFETCH, len(tiles)))}
    res = [None, None]
    for t, (pp, c, hh, j) in enumerate(tiles):
        r0, r1 = c * blk, (c + 1) * blk
        if t + MOBA_PREFETCH < len(tiles):
            pending[t + MOBA_PREFETCH] = scores(t + MOBA_PREFETCH)
        s_cur = pending.pop(t)
        h = 2 * pp + hh
        m_t = jnp.max(s_cur, axis=0, keepdims=True)
        if j < c:
            off = masked_ref[h, j:j + 1, r0:r1] > 0.5
            m_t = jnp.where(off, NEG_INF, m_t)
        m_new = m_t if j == 0 else jnp.maximum(m_run, m_t)
        shift = jnp.where(off, -NEG_INF, m_new) if j < c else m_new
        pv = _dot(vt_ref[h, :, j * blk:(j + 1) * blk], jnp.exp2(s_cur - shift).astype(BF16))
        acc = pv if j == 0 else acc * jnp.exp2(m_run - m_new) + pv
        m_run = m_new
        if j == c:
            res[hh] = acc[0:hd, :] * (1.0 / acc[hd:hd + 1, :])
            if hh == 1:
                o_ref[r0:r1, pp * LANES:(pp + 1) * LANES] = jnp.concatenate(res, axis=0).T.astype(BF16)


def _moba(proj, kmean, batch, seq, q_col, k_col, v_col, weights):
    t = proj.shape[0]
    width = MOBA_PAIRS_PER_STEP * LANES
    pairs = MOBA_HEADS * MOBA_HEAD_DIM // width
    n_heads = 2 * MOBA_PAIRS_PER_STEP
    steps = batch * pairs

    def spec(col0):
        assert col0 % width == 0
        return pl.BlockSpec((seq, width), lambda b, p: (b, col0 // width + p))

    def weight_spec(w):
        share = 1
        while (w.shape[0] * share) % (steps * BF16_SUBLANES):
            share *= 2
        rows = w.shape[0] * share // steps
        return pl.BlockSpec((rows, w.shape[1]), lambda b, p, share=share: ((b * pairs + p) // share, 0))

    w_specs = [weight_spec(w) for w in weights]
    outs = pl.pallas_call(
        functools.partial(_moba_kernel, n_weights=len(weights)),
        grid=(batch, pairs),
        in_specs=[spec(q_col), spec(k_col), spec(v_col),
                  pl.BlockSpec((seq // MOBA_BLOCK, 1, width), lambda b, p: (b, 0, p))] + w_specs,
        out_specs=[pl.BlockSpec((seq, width), lambda b, p: (b, p))] + w_specs,
        out_shape=[jax.ShapeDtypeStruct((t, pairs * width), BF16)]
        + [jax.ShapeDtypeStruct(w.shape, BF16) for w in weights],
        scratch_shapes=[pltpu.VMEM((n_heads, seq, LANES), BF16),
                        pltpu.VMEM((n_heads, MOBA_HEAD_DIM + MOBA_ONES_ROWS, seq), BF16),
                        pltpu.VMEM((n_heads, seq // MOBA_BLOCK, seq), F32)],
        compiler_params=_params(2),
        name="moba",
    )(proj, proj, proj, kmean, *weights)
    return outs[0], outs[1:]


_RET_LOG_G = [math.log(1.0 - 2.0 ** (-5.0 - h)) for h in range(RET_HEADS)]


def _retention_kernel(q_ref, k_ref, v_ref, g_ref, mem_ref, g_mem_ref, w_ckv_ref, o_ref, kv_ref):
    kv_ref[...] = _wdot(_rms(mem_ref[...], g_mem_ref[...]).astype(BF16), w_ckv_ref[...]).astype(BF16)
    seq = q_ref.shape[0]
    ch = RET_CHUNK
    dv = RET_V_DIM
    lane = lax.broadcasted_iota(jnp.int32, (ch, LANES), 1)
    rowf = lax.broadcasted_iota(jnp.int32, (ch, LANES), 0).astype(F32)
    ri = lax.broadcasted_iota(jnp.int32, (ch, ch), 0)
    ci = lax.broadcasted_iota(jnp.int32, (ch, ch), 1)
    diff = (ri - ci).astype(F32)
    heads = []
    for h in range(RET_HEADS):
        log_g = _RET_LOG_G[h]
        hh = h % 2
        heads.append(dict(
            inner_decay=jnp.where(diff >= 0, jnp.exp(log_g * jnp.maximum(diff, 0.0)), 0.0),
            q_decay=jnp.exp(log_g * (rowf + 1.0)),
            chunk_decay=math.exp(log_g * ch),
            mine=(lane >= hh * RET_QK_DIM) & (lane < (hh + 1) * RET_QK_DIM),
            qk=slice((h // 2) * LANES, (h // 2 + 1) * LANES),
            v=slice(h * dv, (h + 1) * dv),
            state=jnp.zeros((LANES, dv), F32),
        ))
    k_decay = [jnp.exp(jnp.where(lane < RET_QK_DIM, _RET_LOG_G[2 * p], _RET_LOG_G[2 * p + 1])
                       * (ch - 1.0 - rowf)) for p in range(RET_HEADS // 2)]
    for pr in range(RET_HEADS // 2):
      for n in range(seq // ch):
        r0, r1 = n * ch, (n + 1) * ch
        k_scaled = [(k_ref[r0:r1, p * LANES:(p + 1) * LANES].astype(F32) * k_decay[p]).astype(BF16)
                    if p == pr else None for p in range(RET_HEADS // 2)]
        for h, hd in list(enumerate(heads))[2 * pr:2 * pr + 2]:
            qc = jnp.where(hd["mine"], q_ref[r0:r1, hd["qk"]], jnp.zeros((), BF16))
            kc = k_ref[r0:r1, hd["qk"]]
            vc = v_ref[r0:r1, hd["v"]]
            attn = _dot_nt(qc, kc) * hd["inner_decay"]
            out = _dot(attn.astype(BF16), vc)
            out = out + _dot(qc, hd["state"].astype(BF16)) * hd["q_decay"]
            hd["state"] = hd["state"] * hd["chunk_decay"] + _dot_tn(k_scaled[h // 2], vc)
            out = out * lax.rsqrt(jnp.mean(out * out, axis=-1, keepdims=True) + NORM_EPS)
            gate = g_ref[r0:r1, hd["v"]].astype(F32)
            o_ref[r0:r1, hd["v"]] = (_silu(gate) * out).astype(BF16)


def _retention(proj, batch, seq, q_col, k_col, v_col, g_col, mem2d, g_mem, w_ckv):
    t = proj.shape[0]
    wqk = RET_HEADS * RET_QK_DIM
    wv = RET_HEADS * RET_V_DIM
    n_mem = mem2d.shape[0] // batch
    d, n_kv = w_ckv.shape

    def spec(col0, width):
        assert col0 % width == 0
        return pl.BlockSpec((seq, width), lambda b: (b, col0 // width))

    return pl.pallas_call(
        _retention_kernel,
        grid=(batch,),
        in_specs=[spec(q_col, wqk), spec(k_col, wqk), spec(v_col, wv), spec(g_col, wv),
                  pl.BlockSpec((n_mem, d), lambda b: (b, 0)),
                  pl.BlockSpec((1, d), lambda b: (0, 0)),
                  pl.BlockSpec((d, n_kv), lambda b: (0, 0), pipeline_mode=pl.Buffered(1))],
        out_specs=[pl.BlockSpec((seq, wv), lambda b: (b, 0)),
                   pl.BlockSpec((n_mem, n_kv), lambda b: (b, 0))],
        out_shape=[jax.ShapeDtypeStruct((t, wv), BF16),
                   jax.ShapeDtypeStruct((batch * n_mem, n_kv), BF16)],
        compiler_params=_params(1),
        name="retention",
    )(proj, proj, proj, proj, mem2d, g_mem, w_ckv)


def _mix_cross_kernel(mo_ref, ro_ref, x_ref, kv_ref, wo_ref, wq_ref, wc_ref,
                      g_mix_ref, g_pre_ref, g_post_ref, o_ref):
    d = x_ref.shape[1]
    half = mo_ref.shape[1]
    dc = d // CROSS_HEADS
    rows = x_ref.shape[0] // MIX_PARTS
    sl = [slice(r * rows, (r + 1) * rows) for r in range(MIX_PARTS)]
    mix = [_wdot(mo_ref[s, :], wo_ref[0:half, :]) + _wdot(ro_ref[s, :], wo_ref[half:, :]) for s in sl]
    x1 = [x_ref[s, :] + _rms(m, g_mix_ref[...]) for s, m in zip(sl, mix)]
    h = [_rms(x, g_pre_ref[...], scale=dc ** -0.5).astype(BF16) for x in x1]
    cq = [_wdot(hh, wq_ref[...]).astype(BF16) for hh in h]
    att = []
    for r in range(MIX_PARTS):
        heads = []
        for hc in range(CROSS_HEADS):
            c0, c1 = hc * dc, (hc + 1) * dc
            s = _dot_nt(cq[r][:, c0:c1], kv_ref[:, c0:c1])
            e = jnp.exp(s - jnp.max(s, axis=1, keepdims=True))
            o = _dot(e.astype(BF16), kv_ref[:, d + c0:d + c1])
            heads.append((o * (1.0 / jnp.sum(e, axis=1, keepdims=True))).astype(BF16))
        att.append(jnp.concatenate(heads, axis=1))
    c = [_wdot(a, wc_ref[...]) for a in att]
    for r in range(MIX_PARTS):
        o_ref[sl[r], :] = x1[r] + _rms(c[r], g_post_ref[...])


def _mix_cross(mo, ro, x2d, kv, w_out, w_cq, w_co, g_mix, g_pre, g_post, seq, n_mem):
    t, d = x2d.shape
    tm = MIX_ROWS
    tiles_per_seq = seq // tm
    half = mo.shape[1]
    full = lambda i: (0, 0)
    weight = pl.BlockSpec((d, d), full, pipeline_mode=pl.Buffered(1))
    return pl.pallas_call(
        _mix_cross_kernel,
        grid=(t // tm,),
        in_specs=[
            pl.BlockSpec((tm, half), lambda i: (i, 0)),
            pl.BlockSpec((tm, half), lambda i: (i, 0)),
            pl.BlockSpec((tm, d), lambda i: (i, 0)),
            pl.BlockSpec((n_mem, 2 * d), lambda i: (i // tiles_per_seq, 0)),
            weight, weight, weight,
            pl.BlockSpec((1, d), full), pl.BlockSpec((1, d), full), pl.BlockSpec((1, d), full),
        ],
        out_specs=pl.BlockSpec((tm, d), lambda i: (i, 0)),
        out_shape=jax.ShapeDtypeStruct((t, d), F32),
        compiler_params=_params(1),
        name="mix_cross",
    )(mo, ro, x2d, kv, w_out, w_cq, w_co, g_mix, g_pre, g_post)


def _ffn_kernel(x_ref, wgu_ref, wd_ref, g_pre_ref, g_post_ref, o_ref, *, d_ff, bounds):
    rows = x_ref.shape[0] // FFN_PARTS
    xs = [x_ref[r * rows:(r + 1) * rows, :] for r in range(FFN_PARTS)]
    hs = [_rms(x, g_pre_ref[...]).astype(BF16) for x in xs]
    fs = [None] * FFN_PARTS
    for c0, c1 in bounds:
        for r in range(FFN_PARTS):
            gate = _wdot(hs[r], wgu_ref[:, c0:c1])
            up = _wdot(hs[r], wgu_ref[:, d_ff + c0:d_ff + c1])
            act = (_silu(gate) * up).astype(BF16)
            down = _wdot(act, wd_ref[c0:c1, :])
            fs[r] = down if fs[r] is None else fs[r] + down
    for r in range(FFN_PARTS):
        o_ref[r * rows:(r + 1) * rows, :] = xs[r] + _rms(fs[r], g_post_ref[...])


def _ffn(x2d, w_gate_up, w_down, g_pre, g_post):
    t, d = x2d.shape
    d_ff = w_down.shape[0]
    tm = FFN_ROWS
    assert d_ff % MXU_DIM == 0 and FFN_CHUNK % MXU_DIM == 0
    edges = list(range(0, d_ff, FFN_CHUNK)) + [d_ff]
    bounds = tuple(zip(edges[:-1], edges[1:]))
    full = lambda i: (0, 0)
    return pl.pallas_call(
        functools.partial(_ffn_kernel, d_ff=d_ff, bounds=bounds),
        grid=(t // tm,),
        in_specs=[
            pl.BlockSpec((tm, d), lambda i: (i, 0)),
            pl.BlockSpec((d, 2 * d_ff), full, pipeline_mode=pl.Buffered(1)),
            pl.BlockSpec((d_ff, d), full, pipeline_mode=pl.Buffered(1)),
            pl.BlockSpec((1, d), full), pl.BlockSpec((1, d), full),
        ],
        out_specs=pl.BlockSpec((tm, d), lambda i: (i, 0)),
        out_shape=jax.ShapeDtypeStruct((t, d), F32),
        compiler_params=_params(1),
        name="ffn",
    )(x2d, w_gate_up, w_down, g_pre, g_post)


def kernel(x, mem, g_pre_mix, w_in, w_out, g_post_mix, g_pre_cross, g_mem, w_cq, w_ckv, w_co,
           g_post_cross, g_pre_ffn, w_gate_up, w_down, g_post_ffn):
    batch, seq, d = x.shape
    n_mem = mem.shape[1]
    depth = w_in.shape[0]
    moba_w = MOBA_HEADS * MOBA_HEAD_DIM
    ret_qk_w = RET_HEADS * RET_QK_DIM
    ret_v_w = RET_HEADS * RET_V_DIM
    c_mq, c_mk, c_mv = 0, moba_w, 2 * moba_w
    c_rq = 3 * moba_w
    c_rk = c_rq + ret_qk_w
    c_rv = c_rk + ret_qk_w
    c_rg = c_rv + ret_v_w
    n_proj = c_rg + ret_v_w
    assert w_in.shape[2] == n_proj
    sections = (("mq", c_mq, c_mk), ("mk", c_mk, c_mv), ("mv", c_mv, c_rq), ("rq", c_rq, c_rk),
                ("rk", c_rk, c_rv), ("rv", c_rv, c_rg), ("rg", c_rg, n_proj))

    moba_inv = np.power(ROPE_THETA, -np.arange(ROPE_DIM // 2, dtype=np.float64) * 2.0 / ROPE_DIM)
    ret_inv = 1.0 / np.power(RET_THETA, np.linspace(0.0, 1.0, RET_QK_DIM // 2))
    tabs = (
        _rotary_tables(seq, moba_inv, ROPE_DIM, MOBA_HEAD_DIM, MOBA_HEAD_DIM ** -0.5 * math.log2(math.e)),
        _rotary_tables(seq, moba_inv, ROPE_DIM, MOBA_HEAD_DIM, 1.0),
        _rotary_tables(seq, ret_inv, RET_QK_DIM, RET_QK_DIM, 1.0),
        _rotary_tables(seq, ret_inv, RET_QK_DIM, RET_QK_DIM, RET_QK_DIM ** -0.5),
    )

    xf = x.reshape(batch * seq, d)
    mem2d = mem.reshape(batch * n_mem, d)
    row = lambda g: g.reshape(1, d)
    for l in range(depth):
        proj, kmean = _in_proj(xf, row(g_pre_mix[l]), w_in[l], tabs, seq, sections)
        mo, (b_ckv, b_out, b_cq, b_co, b_gate_up, b_down) = _moba(
            proj, kmean, batch, seq, c_mq, c_mk, c_mv,
            (w_ckv[l], w_out[l], w_cq[l], w_co[l], w_gate_up[l], w_down[l]))
        ro, kv = _retention(proj, batch, seq, c_rq, c_rk, c_rv, c_rg, mem2d, row(g_mem[l]), b_ckv)
        x2 = _mix_cross(mo, ro, xf, kv, b_out, b_cq, b_co, row(g_post_mix[l]),
                        row(g_pre_cross[l]), row(g_post_cross[l]), seq, n_mem)
        xf = _ffn(x2, b_gate_up, b_down, row(g_pre_ffn[l]), row(g_post_ffn[l]))
    return xf.reshape(batch, seq, d)
```
